```python
import jax
import jax.numpy as jnp
from jax import lax
import numpy as np

D_MODEL = 1024
BATCH = 4
SEQ = 4096
DEPTH = 2
DEC_BATCH = 8
DEC_SEQ = 16
PAST_LEN = 4096

CHUNK = 64
Q_BLOCK = 128
EPS = 1e-6
H_A = 6
DK_A = 32
DV_A = 64
GATE_RANK = 16
GATE_TAU = 16.0
H_B = 4
D_B = 64
FORGET_BIAS = 3.0
H_C = 6
NOPE_C = 64
ROPE_C = 32
V_C = 64
Q_LORA = 256
KV_LORA = 128
ROPE_BASE = 10000.0
MIX_WIDTH = H_A * DV_A + H_B * D_B + H_C * V_C
IN_SPLITS = (H_A * DK_A, H_A * DK_A, H_A * DV_A, H_A * DV_A, GATE_RANK,
             H_B * D_B, H_B * D_B, H_B * D_B, H_B,
             Q_LORA, KV_LORA, ROPE_C)
N_IN = sum(IN_SPLITS)
D_FF = 2816
N_EXPERTS = 8
TOP_K = 2
D_EXPERT = 3584
N_DENSE = (DEPTH + 1) // 2
N_MOE = DEPTH // 2

kernel_name = 'hymba_gla_fox_mla_stream_step'


def _rms_norm(x, g):
    xf = x.astype(jnp.float32)
    y = xf * lax.rsqrt(jnp.mean(xf * xf, axis=-1, keepdims=True) + EPS)
    return (y * g.astype(jnp.float32)).astype(x.dtype)


def _split_in(p):
    idx = [int(i) for i in np.cumsum(IN_SPLITS)[:-1]]
    return jnp.split(p, idx, axis=-1)


def _rope(x, pos):
    half = ROPE_C // 2
    inv_freq = ROPE_BASE ** (-jnp.arange(half, dtype=jnp.float32) / half)
    ang = pos.astype(jnp.float32)[:, None] * inv_freq[None, :]
    cos = jnp.cos(ang)[None, :, None, :]
    sin = jnp.sin(ang)[None, :, None, :]
    xf = x.astype(jnp.float32)
    x1, x2 = xf[..., :half], xf[..., half:]
    return jnp.concatenate([x1 * cos - x2 * sin, x2 * cos + x1 * sin], axis=-1).astype(x.dtype)


def _ada(c, w, b, dtype):
    mod = jax.nn.silu(c.astype(jnp.float32)) @ w.astype(jnp.float32) + b.astype(jnp.float32)
    return [m[:, None, :].astype(dtype) for m in jnp.split(mod, 6, axis=-1)]


def _modulate(x, g, shift, scale):
    return _rms_norm(x, g) * (1 + scale) + shift


def _to_blocks(t):
    B, T = t.shape[:2]
    return jnp.swapaxes(t.reshape(B, T // Q_BLOCK, Q_BLOCK, *t.shape[2:]), 0, 1)


def _from_blocks(t):
    nb, B, qb = t.shape[:3]
    return jnp.swapaxes(t, 0, 1).reshape(B, nb * qb, *t.shape[3:])


def _mixer_inputs(h, pos, w_in, gla_w_gate2, gla_b_gate, fox_b_f, mla_q_norm_g, mla_kv_norm_g, mla_w_uq, mla_w_uk):
    B, T, _ = h.shape
    f32 = jnp.float32
    (a_q, a_k, a_v, a_g, a_r, b_q, b_k, b_v, b_f, c_q, c_kv, c_kr) = _split_in(h @ w_in)
    gq = a_q.reshape(B, T, H_A, DK_A).astype(f32) * (DK_A ** -0.5)
    gk = a_k.reshape(B, T, H_A, DK_A).astype(f32)
    gv = a_v.reshape(B, T, H_A, DV_A).astype(f32)
    g_la = jax.nn.log_sigmoid((a_r @ gla_w_gate2 + gla_b_gate).astype(f32)).reshape(B, T, H_A, DK_A) / GATE_TAU
    fq = b_q.reshape(B, T, H_B, D_B).astype(f32)
    fk = b_k.reshape(B, T, H_B, D_B)
    fv = b_v.reshape(B, T, H_B, D_B)
    f_logf = jax.nn.log_sigmoid((b_f + fox_b_f).astype(f32))
    cq = (_rms_norm(c_q, mla_q_norm_g) @ mla_w_uq).reshape(B, T, H_C, NOPE_C + ROPE_C)
    q_rope = _rope(cq[..., NOPE_C:], pos).astype(f32)
    q_lat = jnp.einsum('bthn,lhn->bthl', cq[..., :NOPE_C], mla_w_uk).astype(f32)
    ckv = _rms_norm(c_kv, mla_kv_norm_g)
    krope = _rope(c_kr[:, :, None, :], pos)[:, :, 0, :]
    return (gq, gk, gv, g_la, a_g), (fq, fk, fv, f_logf), (q_lat, q_rope, ckv, krope)


def _gla_chunked(q, k, v, la, S0, chunk):
    B, T, H, _ = q.shape
    n = T // chunk

    def to_chunks(t):
        return t.reshape(B, n, chunk, H, t.shape[-1]).transpose(1, 0, 3, 2, 4)

    tri = jnp.tril(jnp.ones((chunk, chunk), dtype=bool))[None, None, :, :, None]

    def step(S, xs):
        qc, kc, vc, lac = xs
        b = jnp.cumsum(lac, axis=2)
        decay = jnp.exp(jnp.where(tri, b[:, :, :, None, :] - b[:, :, None, :, :], -jnp.inf))
        att = jnp.einsum('bhtk,bhtsk,bhsk->bhts', qc, decay, kc)
        o = jnp.einsum('bhts,bhsv->bhtv', att, vc) + jnp.einsum('bhtk,bhkv->bhtv', qc * jnp.exp(b), S)
        b_last = b[:, :, -1:, :]
        S_new = jnp.exp(b_last[:, :, 0, :])[..., None] * S + jnp.einsum('bhsk,bhsv->bhkv', kc * jnp.exp(b_last - b), vc)
        return S_new, o

    S, o = lax.scan(step, S0, (to_chunks(q), to_chunks(k), to_chunks(v), to_chunks(la)))
    return o.transpose(1, 0, 3, 2, 4).reshape(B, T, H, v.shape[-1]), S


def _fox_attend(q, k, v, Fq, Fk, qpos, kpos):
    s = jnp.einsum('bqhd,bkhd->bhqk', q, k) * (D_B ** -0.5)
    s = s + jnp.swapaxes(Fq, 1, 2)[..., None] - jnp.swapaxes(Fk, 1, 2)[:, :, None, :]
    s = jnp.where(kpos[None, :] <= qpos[:, None], s, -jnp.inf)
    return jnp.einsum('bhqk,bkhd->bqhd', jax.nn.softmax(s, axis=-1), v)


def _mla_attend(q_lat, q_rope, ckv, krope, qpos, kpos):
    s = (jnp.einsum('bqhl,bkl->bhqk', q_lat, ckv) + jnp.einsum('bqhr,bkr->bhqk', q_rope, krope)) * ((NOPE_C + ROPE_C) ** -0.5)
    s = jnp.where((kpos // CHUNK)[None, :] <= (qpos // CHUNK)[:, None], s, -jnp.inf)
    return jnp.einsum('bhqk,bkl->bqhl', jax.nn.softmax(s, axis=-1), ckv)


def _mixer_output(o_gla, a_g, o_fox, o_mla_lat, gla_norm_g, mla_w_uv, w_out, dtype):
    B, T = o_gla.shape[:2]
    gate = jax.nn.silu(a_g.astype(jnp.float32)).reshape(B, T, H_A, DV_A)
    y_a = (_rms_norm(o_gla, gla_norm_g) * gate).reshape(B, T, H_A * DV_A)
    y_b = o_fox.reshape(B, T, H_B * D_B)
    y_c = jnp.einsum('bthl,lhv->bthv', o_mla_lat, mla_w_uv.astype(jnp.float32)).reshape(B, T, H_C * V_C)
    return jnp.concatenate([y_a, y_b, y_c], axis=-1).astype(dtype) @ w_out


def _mix_prompt(h, w_in, gla_w_gate2, gla_b_gate, fox_b_f, mla_q_norm_g, mla_kv_norm_g, mla_w_uq, mla_w_uk, gla_norm_g, mla_w_uv, w_out):
    B, T, _ = h.shape
    f32 = jnp.float32
    pos = jnp.arange(T)
    (gq, gk, gv, g_la, a_g), (fq, fk, fv, f_logf), (q_lat, q_rope, ckv, krope) = _mixer_inputs(
        h, pos, w_in, gla_w_gate2, gla_b_gate, fox_b_f, mla_q_norm_g, mla_kv_norm_g, mla_w_uq, mla_w_uk)
    o_gla, S = _gla_chunked(gq, gk, gv, g_la, jnp.zeros((B, H_A, DK_A, DV_A), f32), CHUNK)
    F = jnp.cumsum(f_logf, axis=1)
    fk32, fv32 = fk.astype(f32), fv.astype(f32)

    def fox_block(args):
        qi, Fqi, qpi = args
        return _fox_attend(qi, fk32, fv32, Fqi, F, qpi, pos)

    o_fox = _from_blocks(lax.map(fox_block, (_to_blocks(fq), _to_blocks(F), pos.reshape(-1, Q_BLOCK))))
    ckv32, kr32 = ckv.astype(f32), krope.astype(f32)

    def mla_block(args):
        qli, qri, qpi = args
        return _mla_attend(qli, qri, ckv32, kr32, qpi, pos)

    o_mla = _from_blocks(lax.map(mla_block, (_to_blocks(q_lat), _to_blocks(q_rope), pos.reshape(-1, Q_BLOCK))))
    out = _mixer_output(o_gla, a_g, o_fox, o_mla, gla_norm_g, mla_w_uv, w_out, h.dtype)
    return out, (S, fk, fv, f_logf, ckv, krope)


def _mix_sample(h, cache_k, cache_v, cache_logf, cache_ckv, cache_krope, S_in,
                w_in, gla_w_gate2, gla_b_gate, fox_b_f, mla_q_norm_g, mla_kv_norm_g, mla_w_uq, mla_w_uk, gla_norm_g, mla_w_uv, w_out):
    B, T, _ = h.shape
    f32 = jnp.float32
    P = cache_k.shape[1]
    pos = P + jnp.arange(T)
    kpos = jnp.arange(P + T)
    (gq, gk, gv, g_la, a_g), (fq, fk, fv, f_logf), (q_lat, q_rope, ckv, krope) = _mixer_inputs(
        h, pos, w_in, gla_w_gate2, gla_b_gate, fox_b_f, mla_q_norm_g, mla_kv_norm_g, mla_w_uq, mla_w_uk)
    o_gla, S = _gla_chunked(gq, gk, gv, g_la, S_in.astype(f32), T)
    K = jnp.concatenate([cache_k.astype(f32), fk.astype(f32)], axis=1)
    V = jnp.concatenate([cache_v.astype(f32), fv.astype(f32)], axis=1)
    F = jnp.cumsum(jnp.concatenate([cache_logf.astype(f32), f_logf], axis=1), axis=1)
    o_fox = _fox_attend(fq, K, V, F[:, P:], F, pos, kpos)
    CK = jnp.concatenate([cache_ckv.astype(f32), ckv.astype(f32)], axis=1)
    CR = jnp.concatenate([cache_krope.astype(f32), krope.astype(f32)], axis=1)
    o_mla = _mla_attend(q_lat, q_rope, CK, CR, pos, kpos)
    out = _mixer_output(o_gla, a_g, o_fox, o_mla, gla_norm_g, mla_w_uv, w_out, h.dtype)
    return out, (S, fk, fv, f_logf, ckv, krope)


def _swiglu(h, wg, wu, wd):
    return (jax.nn.silu(h @ wg) * (h @ wu)) @ wd


def _moe(h, router, wg, wu, wd):
    logits = (h @ router).astype(jnp.float32)
    top_v, top_i = lax.top_k(logits, TOP_K)
    gates = jax.nn.softmax(top_v, axis=-1)
    combine = jnp.sum(jax.nn.one_hot(top_i, N_EXPERTS, dtype=jnp.float32) * gates[..., None], axis=-2)
    y = jnp.zeros(h.shape, jnp.float32)
    for e in range(N_EXPERTS):
        y = y + combine[..., e:e + 1] * _swiglu(h, wg[e], wu[e], wd[e]).astype(jnp.float32)
    return y.astype(h.dtype)


def _channel_mixer(h, l, ffn_w_gate, ffn_w_up, ffn_w_down, moe_router, moe_w_gate, moe_w_up, moe_w_down):
    j = l // 2
    if l % 2 == 0:
        return _swiglu(h, ffn_w_gate[j], ffn_w_up[j], ffn_w_down[j])
    return _moe(h, moe_router[j], moe_w_gate[j], moe_w_up[j], moe_w_down[j])


def setup_inputs(seed: int = 0) -> dict:
    key = jax.random.key(seed)
    keys = jax.random.split(key, 40)
    cnt = [0]

    def nrm(shape, scale=1.0):
        k = keys[cnt[0]]
        cnt[0] += 1
        return jax.random.normal(k, shape, jnp.float32) * scale

    D = D_MODEL
    inp = {}
    inp['x_prompt'] = nrm((BATCH, SEQ, D))
    inp['x_sample'] = nrm((DEC_BATCH, DEC_SEQ, D))
    inp['c_prompt'] = nrm((BATCH, D))
    inp['c_sample'] = nrm((DEC_BATCH, D))
    inp['cache_fox_k'] = nrm((DEPTH, DEC_BATCH, PAST_LEN, H_B, D_B))
    inp['cache_fox_v'] = nrm((DEPTH, DEC_BATCH, PAST_LEN, H_B, D_B))
    inp['cache_fox_logf'] = jax.nn.log_sigmoid(FORGET_BIAS + nrm((DEPTH, DEC_BATCH, PAST_LEN, H_B)))
    inp['cache_mla_ckv'] = nrm((DEPTH, DEC_BATCH, PAST_LEN, KV_LORA))
    inp['cache_mla_krope'] = nrm((DEPTH, DEC_BATCH, PAST_LEN, ROPE_C))
    inp['state_gla'] = nrm((DEPTH, DEC_BATCH, H_A, DK_A, DV_A))
    inp['ada_w'] = nrm((DEPTH, D, 6 * D), 0.5 * D ** -0.5)
    inp['ada_b'] = nrm((DEPTH, 6 * D), 0.02)
    inp['norm_mix_g'] = 1.0 + nrm((DEPTH, D), 0.02)
    inp['norm_ffn_g'] = 1.0 + nrm((DEPTH, D), 0.02)
    inp['w_in'] = nrm((DEPTH, D, N_IN), D ** -0.5)
    inp['gla_w_gate2'] = nrm((DEPTH, GATE_RANK, H_A * DK_A), GATE_RANK ** -0.5)
    inp['gla_b_gate'] = nrm((DEPTH, H_A * DK_A), 0.1)
    inp['gla_norm_g'] = 1.0 + nrm((DEPTH, DV_A), 0.02)
    inp['fox_b_f'] = FORGET_BIAS + nrm((DEPTH, H_B), 0.1)
    inp['mla_q_norm_g'] = 1.0 + nrm((DEPTH, Q_LORA), 0.02)
    inp['mla_kv_norm_g'] = 1.0 + nrm((DEPTH, KV_LORA), 0.02)
    inp['mla_w_uq'] = nrm((DEPTH, Q_LORA, H_C * (NOPE_C + ROPE_C)), Q_LORA ** -0.5)
    inp['mla_w_uk'] = nrm((DEPTH, KV_LORA, H_C, NOPE_C), KV_LORA ** -0.5)
    inp['mla_w_uv'] = nrm((DEPTH, KV_LORA, H_C, V_C), KV_LORA ** -0.5)
    inp['w_out'] = nrm((DEPTH, MIX_WIDTH, D), MIX_WIDTH ** -0.5)
    inp['ffn_w_gate'] = nrm((N_DENSE, D, D_FF), D ** -0.5)
    inp['ffn_w_up'] = nrm((N_DENSE, D, D_FF), D ** -0.5)
    inp['ffn_w_down'] = nrm((N_DENSE, D_FF, D), D_FF ** -0.5)
    inp['moe_router'] = nrm((N_MOE, D, N_EXPERTS), D ** -0.5)
    inp['moe_w_gate'] = nrm((N_MOE, N_EXPERTS, D, D_EXPERT), D ** -0.5)
    inp['moe_w_up'] = nrm((N_MOE, N_EXPERTS, D, D_EXPERT), D ** -0.5)
    inp['moe_w_down'] = nrm((N_MOE, N_EXPERTS, D_EXPERT, D), D_EXPERT ** -0.5)
    inp['final_norm_g'] = 1.0 + nrm((D,), 0.02)
    return inp


def reference(x_prompt, x_sample, c_prompt, c_sample,
              cache_fox_k, cache_fox_v, cache_fox_logf, cache_mla_ckv, cache_mla_krope, state_gla,
              ada_w, ada_b, norm_mix_g, norm_ffn_g, w_in, gla_w_gate2, gla_b_gate, gla_norm_g, fox_b_f,
              mla_q_norm_g, mla_kv_norm_g, mla_w_uq, mla_w_uk, mla_w_uv, w_out,
              ffn_w_gate, ffn_w_up, ffn_w_down, moe_router, moe_w_gate, moe_w_up, moe_w_down, final_norm_g):
    xp, xs = x_prompt, x_sample
    s_dtypes = (state_gla.dtype, cache_fox_k.dtype, cache_fox_v.dtype, cache_fox_logf.dtype,
                cache_mla_ckv.dtype, cache_mla_krope.dtype)
    p_states = [[] for _ in range(6)]
    s_states = [[] for _ in range(6)]
    for l in range(DEPTH):
        mix_w = (w_in[l], gla_w_gate2[l], gla_b_gate[l], fox_b_f[l], mla_q_norm_g[l], mla_kv_norm_g[l],
                 mla_w_uq[l], mla_w_uk[l], gla_norm_g[l], mla_w_uv[l], w_out[l])
        mp = _ada(c_prompt, ada_w[l], ada_b[l], xp.dtype)
        ms = _ada(c_sample, ada_w[l], ada_b[l], xs.dtype)
        out_p, st_p = _mix_prompt(_modulate(xp, norm_mix_g[l], mp[0], mp[1]), *mix_w)
        xp = xp + mp[2] * out_p
        out_s, st_s = _mix_sample(_modulate(xs, norm_mix_g[l], ms[0], ms[1]),
                                  cache_fox_k[l], cache_fox_v[l], cache_fox_logf[l],
                                  cache_mla_ckv[l], cache_mla_krope[l], state_gla[l], *mix_w)
        xs = xs + ms[2] * out_s
        xp = xp + mp[5] * _channel_mixer(_modulate(xp, norm_ffn_g[l], mp[3], mp[4]), l,
                                         ffn_w_gate, ffn_w_up, ffn_w_down, moe_router, moe_w_gate, moe_w_up, moe_w_down)
        xs = xs + ms[5] * _channel_mixer(_modulate(xs, norm_ffn_g[l], ms[3], ms[4]), l,
                                         ffn_w_gate, ffn_w_up, ffn_w_down, moe_router, moe_w_gate, moe_w_up, moe_w_down)
        for i in range(6):
            p_states[i].append(st_p[i].astype(xp.dtype))
            s_states[i].append(st_s[i].astype(s_dtypes[i]))
    p_gla, p_fox_k, p_fox_v, p_fox_logf, p_mla_ckv, p_mla_krope = [jnp.stack(s, axis=0) for s in p_states]
    s_gla, s_fox_k, s_fox_v, s_fox_logf, s_mla_ckv, s_mla_krope = [jnp.stack(s, axis=0) for s in s_states]
    y_prompt = _rms_norm(xp, final_norm_g)
    y_sample = _rms_norm(xs, final_norm_g)
    return (y_prompt, y_sample,
            p_gla, p_fox_k, p_fox_v, p_fox_logf, p_mla_ckv, p_mla_krope,
            s_gla, s_fox_k, s_fox_v, s_fox_logf, s_mla_ckv, s_mla_krope)
```

```python
import functools

import numpy as np
import jax
import jax.numpy as jnp
from jax import lax
from jax.experimental import pallas as pl
from jax.experimental.pallas import tpu as pltpu

F32 = jnp.float32
BF16 = jnp.bfloat16
HIGHEST = lax.Precision.HIGHEST

D_MODEL = 1024
DEPTH = 2
CHUNK = 64
EPS = 1e-6
H_A, DK_A, DV_A = 6, 32, 64
GATE_RANK = 16
GATE_TAU = 16.0
H_B, D_B = 4, 64
H_C, NOPE_C, ROPE_C, V_C = 6, 64, 32, 64
Q_LORA, KV_LORA = 256, 128
ROPE_BASE = 10000.0
N_EXPERTS = 8
IN_SPLITS = (H_A * DK_A, H_A * DK_A, H_A * DV_A, H_A * DV_A, GATE_RANK,
             H_B * D_B, H_B * D_B, H_B * D_B, H_B, Q_LORA, KV_LORA, ROPE_C)

QK_A = H_A * DK_A
V_A = H_A * DV_A
QKV_B = H_B * D_B
LANE = 128
NEG = -1e30

C_GQ, C_GK, C_GV, C_AG = 0, 256, 512, 896
C_FQ, C_FK, C_FV = 1280, 1536, 1792
C_CQ, C_CKV, C_SM = 2048, 2304, 2432
N_PACK = 2560
SM_KR, SM_AR, SM_BF, SM_KRS = 0, 32, 48, 64

GLA_SUB = 16
VMEM_LIMIT = 56 * 1024 * 1024


def _cparams(sem):
    return pltpu.CompilerParams(dimension_semantics=sem, vmem_limit_bytes=VMEM_LIMIT)


def _log_sigmoid(z):
    return jnp.minimum(z, 0.0) - jnp.log1p(jnp.exp(-jnp.abs(z)))


def _silu(z):
    return z * (1.0 / (1.0 + jnp.exp(-z)))


def _rms(x):
    return x * lax.rsqrt(jnp.mean(x * x, axis=-1, keepdims=True) + EPS)


def _dot(a, b):
    return jnp.dot(a, b, preferred_element_type=F32)


def _dot_nt(a, b):
    return lax.dot_general(a, b, (((1,), (1,)), ((), ())), preferred_element_type=F32)


def _dot_tn(a, b):
    return lax.dot_general(a, b, (((0,), (0,)), ((), ())), preferred_element_type=F32)


def _mm(a, b, dims=(((1,), (0,)), ((), ())), *, precise):
    if precise:
        return lax.dot_general(a.astype(F32), b.astype(F32), dims, precision=HIGHEST, preferred_element_type=F32)
    return lax.dot_general(a.astype(BF16), b.astype(BF16), dims, preferred_element_type=F32)


_NT = (((1,), (1,)), ((), ()))
_TN = (((0,), (0,)), ((), ()))


def _ada_kernel(c_ref, w_ref, b_ref, o_ref):
    s = _silu(c_ref[...])
    o_ref[0] = jnp.dot(s, w_ref[0], precision=HIGHEST, preferred_element_type=F32) + b_ref[0]


def _ada(c_all, ada_w, ada_b):
    nc = c_all.shape[0]
    tn = 1536
    return pl.pallas_call(
        _ada_kernel,
        grid=(DEPTH, 6 * D_MODEL // tn),
        in_specs=[pl.BlockSpec((nc, D_MODEL), lambda l, j: (0, 0)),
                  pl.BlockSpec((1, D_MODEL, tn), lambda l, j: (l, 0, j)),
                  pl.BlockSpec((1, 1, tn), lambda l, j: (l, 0, j))],
        out_specs=pl.BlockSpec((1, nc, tn), lambda l, j: (l, 0, j)),
        out_shape=jax.ShapeDtypeStruct((DEPTH, nc, 6 * D_MODEL), F32),
        compiler_params=_cparams(("arbitrary", "arbitrary")),
        name="ada",
    )(c_all, ada_w, ada_b.reshape(DEPTH, 1, 6 * D_MODEL))


def _inproj_kernel(x_ref, sh_ref, sc_ref, g_ref, w_ref, wg2_ref, bg_ref, bf_ref, qng_ref, kvng_ref,
                   wuqn_ref, wuk_ref, wr_ref, wrs_ref, cos_ref, sin_ref,
                   gq_ref, gk_ref, gv_ref, ag_ref, gla_ref, fqh_ref, fkh_ref, fvh_ref, fk_ref, fv_ref,
                   ckv_ref, kc_ref, small_ref, qs_ref, *, precise):
    mm = functools.partial(_mm, precise=precise)
    act = kc_ref.dtype
    x = x_ref[...]
    h = (_rms(x) * g_ref[...]) * (1.0 + sc_ref[0]) + sh_ref[0]
    p = mm(h, w_ref[...])

    gq_ref[...] = p[:, C_GQ:C_GQ + QK_A] * (DK_A ** -0.5)
    gk_ref[...] = p[:, C_GK:C_GK + QK_A]
    gv_ref[...] = p[:, C_GV:C_GV + V_A]
    ag_ref[...] = p[:, C_AG:C_AG + V_A]
    sm = p[:, C_SM:C_SM + LANE]
    z = mm(sm, wg2_ref[...]) + bg_ref[...]
    gla_ref[...] = _log_sigmoid(z[:, :QK_A]) * (1.0 / GATE_TAU)

    fq = p[:, C_FQ:C_FQ + QKV_B] * (D_B ** -0.5)
    fk = p[:, C_FK:C_FK + QKV_B]
    fv = p[:, C_FV:C_FV + QKV_B]
    fk_ref[...] = fk
    fv_ref[...] = fv
    for hh in range(H_B):
        sl = slice(hh * D_B, (hh + 1) * D_B)
        fqh_ref[0, hh] = fq[:, sl].astype(act)
        fkh_ref[0, hh] = fk[:, sl].astype(act)
        fvh_ref[0, hh] = fv[:, sl].astype(act)

    cos = cos_ref[...]
    sin = sin_ref[...]
    lane = lax.broadcasted_iota(jnp.int32, sm.shape, 1)
    kr = sm * cos + pltpu.roll(sm, LANE - SM_KRS, 1) * sin
    logf = _log_sigmoid(sm + bf_ref[...])
    small_ref[...] = jnp.where((lane >= SM_BF) & (lane < SM_BF + H_B), logf, kr)

    ckv = _rms(p[:, C_CKV:C_CKV + KV_LORA]) * kvng_ref[...]
    ckv_ref[...] = ckv
    kc_ref[:, :KV_LORA] = ckv.astype(act)
    kc_ref[:, KV_LORA:] = jnp.where(lane < ROPE_C, kr, 0.0).astype(act)
    cqn = _rms(p[:, C_CQ:C_CQ + Q_LORA]) * qng_ref[...]
    nope = mm(cqn, wuqn_ref[...])
    qlat = mm(nope, wuk_ref[...])
    qa = mm(cqn, wr_ref[...])
    qb = mm(cqn, wrs_ref[...])
    scale = (NOPE_C + ROPE_C) ** -0.5
    for hh in range(H_C):
        sl = slice(hh * LANE, (hh + 1) * LANE)
        qs_ref[0, hh, :, :KV_LORA] = (qlat[:, sl] * scale).astype(act)
        qs_ref[0, hh, :, KV_LORA:] = ((qa[:, sl] * cos + qb[:, sl] * sin) * scale).astype(act)


def _inproj(x2, shift, scale, g, pw, cos_tab, sin_tab, n_seq, t_len, tm, precise):
    n_tok = n_seq * t_len
    nblk = t_len // tm
    act = F32 if precise else BF16
    mod_rows = shift.shape[1]

    def row(i):
        return (i, 0)

    def seq(i):
        return (i // nblk, 0, 0)

    def const2(i):
        return (0, 0)

    def tab(i):
        return (i % nblk, 0)

    def headmajor(i):
        return (i // nblk, 0, i % nblk, 0)

    sds = jax.ShapeDtypeStruct
    out_shape = (
        sds((n_tok, QK_A), F32), sds((n_tok, QK_A), F32), sds((n_tok, V_A), F32), sds((n_tok, V_A), F32),
        sds((n_tok, QK_A), F32),
        sds((n_seq, H_B, t_len, D_B), act), sds((n_seq, H_B, t_len, D_B), act), sds((n_seq, H_B, t_len, D_B), act),
        sds((n_tok, QKV_B), F32), sds((n_tok, QKV_B), F32),
        sds((n_tok, KV_LORA), F32), sds((n_tok, 2 * LANE), act), sds((n_tok, LANE), F32),
        sds((n_tok // tm, H_C, tm, 2 * LANE), act),
    )
    out_specs = (
        pl.BlockSpec((tm, QK_A), row), pl.BlockSpec((tm, QK_A), row), pl.BlockSpec((tm, V_A), row),
        pl.BlockSpec((tm, V_A), row), pl.BlockSpec((tm, QK_A), row),
        pl.BlockSpec((1, H_B, tm, D_B), headmajor), pl.BlockSpec((1, H_B, tm, D_B), headmajor),
        pl.BlockSpec((1, H_B, tm, D_B), headmajor),
        pl.BlockSpec((tm, QKV_B), row), pl.BlockSpec((tm, QKV_B), row),
        pl.BlockSpec((tm, KV_LORA), row), pl.BlockSpec((tm, 2 * LANE), row), pl.BlockSpec((tm, LANE), row),
        pl.BlockSpec((1, H_C, tm, 2 * LANE), lambda i: (i, 0, 0, 0)),
    )
    in_specs = [
        pl.BlockSpec((tm, D_MODEL), row), pl.BlockSpec((1, mod_rows, D_MODEL), seq),
        pl.BlockSpec((1, mod_rows, D_MODEL), seq), pl.BlockSpec((1, D_MODEL), const2),
        pl.BlockSpec((D_MODEL, N_PACK), const2), pl.BlockSpec((LANE, 2 * LANE), const2),
        pl.BlockSpec((1, 2 * LANE), const2), pl.BlockSpec((1, LANE), const2),
        pl.BlockSpec((1, Q_LORA), const2), pl.BlockSpec((1, KV_LORA), const2),
        pl.BlockSpec((Q_LORA, H_C * NOPE_C), const2), pl.BlockSpec((H_C * NOPE_C, H_C * KV_LORA), const2),
        pl.BlockSpec((Q_LORA, H_C * LANE), const2), pl.BlockSpec((Q_LORA, H_C * LANE), const2),
        pl.BlockSpec((tm, LANE), tab), pl.BlockSpec((tm, LANE), tab),
    ]
    return pl.pallas_call(
        functools.partial(_inproj_kernel, precise=precise), grid=(n_tok // tm,), in_specs=in_specs,
        out_specs=out_specs, out_shape=out_shape, compiler_params=_cparams(("arbitrary",)), name="inproj",
    )(x2, shift, scale, g, pw["w_in"], pw["wg2"], pw["bg"], pw["bf"], pw["qng"], pw["kvng"],
      pw["wuqn"], pw["wuk"], pw["wr"], pw["wrs"], cos_tab, sin_tab)


def _cumsum_kernel(x_ref, init_ref, o_ref, *, tb):
    n = x_ref.shape[1]
    upper = (lax.broadcasted_iota(jnp.int32, (tb, tb), 0) <= lax.broadcasted_iota(jnp.int32, (tb, tb), 1)).astype(F32)
    carry = init_ref[...]
    for j in range(n // tb):
        blk = jnp.dot(x_ref[:, j * tb:(j + 1) * tb], upper, precision=HIGHEST, preferred_element_type=F32) + carry
        o_ref[:, j * tb:(j + 1) * tb] = blk
        carry = blk[:, tb - 1:tb]


def _cumsum(x, init, tb):
    return pl.pallas_call(
        functools.partial(_cumsum_kernel, tb=tb),
        out_shape=jax.ShapeDtypeStruct(x.shape, F32), name="cumsum",
    )(x, init)


def _gla_kernel(q_ref, k_ref, v_ref, la_ref, ag_ref, gn_ref, s0_ref, y_ref, sout_ref, s_ref, *, chunk, n_chunks,
                precise):
    mm = functools.partial(_mm, precise=precise)
    c = chunk
    nsub = c // GLA_SUB

    @pl.when(pl.program_id(1) == 0)
    def _():
        s_ref[...] = s0_ref[0]

    lane_qk = lax.broadcasted_iota(jnp.int32, (GLA_SUB, QK_A), 1) // DK_A
    lane_v = lax.broadcasted_iota(jnp.int32, (GLA_SUB, V_A), 1) // DV_A
    bd = (lax.broadcasted_iota(jnp.int32, (V_A, QK_A), 0) // DV_A) == (lax.broadcasted_iota(jnp.int32, (V_A, QK_A), 1) // DK_A)
    tril = (lax.broadcasted_iota(jnp.int32, (c, c), 0) >= lax.broadcasted_iota(jnp.int32, (c, c), 1)).astype(F32)
    hm = (lax.broadcasted_iota(jnp.int32, (V_A, V_A), 0) // DV_A) == (lax.broadcasted_iota(jnp.int32, (V_A, V_A), 1) // DV_A)
    head_mean = jnp.where(hm, 1.0 / DV_A, 0.0).astype(F32)

    def body(ci, carry):
        r = pl.multiple_of(ci * c, c)
        q = q_ref[pl.ds(r, c), :]
        k = k_ref[pl.ds(r, c), :]
        v = v_ref[pl.ds(r, c), :]
        la = la_ref[pl.ds(r, c), :]
        b = jnp.dot(tril, la, precision=HIGHEST, preferred_element_type=F32)
        s_t = s_ref[...]
        vb = v if precise else v.astype(BF16)
        o_inter = mm(q * jnp.exp(b), s_t, _NT)
        outs = []
        for i in range(nsub):
            r0 = i * GLA_SUB
            r1 = r0 + GLA_SUB
            bi = b[r0 - 1:r0] if i > 0 else jnp.zeros((1, QK_A), F32)
            qi = q[r0:r1] * jnp.exp(b[r0:r1] - bi)
            kk = k[:r1] * jnp.exp(bi - b[:r1])
            qst = jnp.concatenate([jnp.where(lane_qk == hh, qi, 0.0) for hh in range(H_A)], axis=0)
            att = mm(qst, kk, _NT)
            t_idx = r0 + lax.broadcasted_iota(jnp.int32, att.shape, 0) % GLA_SUB
            s_idx = lax.broadcasted_iota(jnp.int32, att.shape, 1)
            att = jnp.where(s_idx <= t_idx, att, 0.0)
            oi = mm(att, vb[:r1])
            o = jnp.zeros((GLA_SUB, V_A), F32)
            for hh in range(H_A):
                o = o + jnp.where(lane_v == hh, oi[hh * GLA_SUB:(hh + 1) * GLA_SUB], 0.0)
            outs.append(o)
        o = jnp.concatenate(outs, axis=0) + o_inter if nsub > 1 else outs[0] + o_inter
        b_last = b[c - 1:c]
        kd = k * jnp.exp(b_last - b)
        s_ref[...] = s_t * jnp.exp(b_last) + jnp.where(bd, mm(vb, kd, _TN), 0.0)
        ms = jnp.dot(o * o, head_mean, precision=HIGHEST, preferred_element_type=F32)
        y = o * lax.rsqrt(ms + EPS) * gn_ref[...] * _silu(ag_ref[pl.ds(r, c), :])
        y_ref[pl.ds(r, c), :] = y.astype(y_ref.dtype)
        return carry

    lax.fori_loop(0, n_chunks, body, 0)

    @pl.when(pl.program_id(1) == pl.num_programs(1) - 1)
    def _():
        sout_ref[0] = s_ref[...]


def _gla(gq, gk, gv, gla, ag, gnorm, s0_t, n_seq, t_len, tb, chunk, precise):
    nblk = t_len // tb

    def row(b, j):
        return (b * nblk + j, 0)

    def st(b, j):
        return (b, 0, 0)

    n_tok = n_seq * t_len
    return pl.pallas_call(
        functools.partial(_gla_kernel, chunk=chunk, n_chunks=tb // chunk, precise=precise),
        grid=(n_seq, nblk),
        in_specs=[pl.BlockSpec((tb, QK_A), row), pl.BlockSpec((tb, QK_A), row), pl.BlockSpec((tb, V_A), row),
                  pl.BlockSpec((tb, QK_A), row), pl.BlockSpec((tb, V_A), row),
                  pl.BlockSpec((1, V_A), lambda b, j: (0, 0)), pl.BlockSpec((1, V_A, QK_A), st)],
        out_specs=(pl.BlockSpec((tb, V_A), row), pl.BlockSpec((1, V_A, QK_A), st)),
        out_shape=(jax.ShapeDtypeStruct((n_tok, V_A), F32 if precise else BF16),
                   jax.ShapeDtypeStruct((n_seq, V_A, QK_A), F32)),
        scratch_shapes=[pltpu.VMEM((V_A, QK_A), F32)],
        compiler_params=_cparams(("arbitrary", "arbitrary")), name="gla",
    )(gq, gk, gv, gla, ag, gnorm, s0_t)


def _flash_kernel(*refs, tq, rep, tk, q0, mode, kv_len, dv, bias, v_from_k, precise):
    mm = functools.partial(_mm, precise=precise)
    refs = list(refs)
    q_ref = refs.pop(0)
    k_ref = refs.pop(0)
    v_ref = k_ref if v_from_k else refs.pop(0)
    fq_ref = refs.pop(0) if bias else None
    fk_ref = refs.pop(0) if bias else None
    o_ref, m_ref, l_ref, acc_ref = refs
    i = pl.program_id(1)
    j = pl.program_id(2)
    rows = rep * tq

    @pl.when(j == 0)
    def _():
        m_ref[...] = jnp.full(m_ref.shape, NEG, F32)
        l_ref[...] = jnp.zeros(l_ref.shape, F32)
        acc_ref[...] = jnp.zeros(acc_ref.shape, F32)

    first_q = q0 + i * tq
    last_q = first_q + tq - 1
    if mode == "chunk":
        vis_all = (first_q // CHUNK) * CHUNK + CHUNK - 1
        vis_any = (last_q // CHUNK) * CHUNK + CHUNK - 1
    else:
        vis_all = first_q
        vis_any = last_q
    vis_all = jnp.minimum(vis_all, kv_len - 1)
    vis_any = jnp.minimum(vis_any, kv_len - 1)
    k_start = j * tk
    k_end = k_start + tk - 1

    def step(masked):
        q = q_ref[0, 0]
        k = k_ref[0]
        s = mm(q, k, _NT)
        if bias:
            s = s + fq_ref[0, 0] - fk_ref[0]
        if masked:
            qpos = first_q + lax.broadcasted_iota(jnp.int32, (rows, tk), 0) % tq
            kpos = k_start + lax.broadcasted_iota(jnp.int32, (rows, tk), 1)
            if mode == "chunk":
                ok = (kpos // CHUNK) <= (qpos // CHUNK)
            else:
                ok = kpos <= qpos
            ok = ok & (kpos < kv_len)
            s = jnp.where(ok, s, NEG)
        m_prev = m_ref[...]
        m_new = jnp.maximum(m_prev, jnp.max(s, axis=-1, keepdims=True))
        alpha = jnp.exp(m_prev - m_new)
        p = jnp.exp(s - m_new)
        l_ref[...] = alpha * l_ref[...] + jnp.sum(p, axis=-1, keepdims=True)
        v = v_ref[0][:, :dv] if v_from_k else v_ref[0]
        acc_ref[...] = alpha * acc_ref[...] + mm(p, v)
        m_ref[...] = m_new

    @pl.when(k_end <= vis_all)
    def _():
        step(False)

    @pl.when((k_end > vis_all) & (k_start <= vis_any))
    def _():
        step(True)

    @pl.when(j == pl.num_programs(2) - 1)
    def _():
        o_ref[0, 0] = (acc_ref[...] / l_ref[...]).astype(o_ref.dtype)


def _flash(q, k, v, fq, fk, *, tq, rep, tk, q0, mode, kv_len, dv, precise=False):
    g, nq, rows, dqk = q.shape
    t_k = k.shape[1]
    nk = t_k // tk
    bias = fq is not None
    v_from_k = v is None

    def last_blk(i):
        last_q = q0 + (i + 1) * tq - 1
        vis = (last_q // CHUNK) * CHUNK + CHUNK - 1 if mode == "chunk" else last_q
        return jnp.minimum(vis, kv_len - 1) // tk

    def qmap(b, i, j):
        return (b, i, 0, 0)

    def kmap(b, i, j):
        return (b, jnp.minimum(j, last_blk(i)), 0)

    def fkmap(b, i, j):
        return (b, 0, jnp.minimum(j, last_blk(i)))

    in_specs = [pl.BlockSpec((1, 1, rows, dqk), qmap), pl.BlockSpec((1, tk, dqk), kmap)]
    args = [q, k]
    if not v_from_k:
        in_specs.append(pl.BlockSpec((1, tk, dv), kmap))
        args.append(v)
    if bias:
        in_specs += [pl.BlockSpec((1, 1, rows, 1), qmap), pl.BlockSpec((1, 1, tk), fkmap)]
        args += [fq, fk]
    return pl.pallas_call(
        functools.partial(_flash_kernel, tq=tq, rep=rep, tk=tk, q0=q0, mode=mode, kv_len=kv_len, dv=dv,
                          bias=bias, v_from_k=v_from_k, precise=precise),
        grid=(g, nq, nk), in_specs=in_specs,
        out_specs=pl.BlockSpec((1, 1, rows, dv), qmap),
        out_shape=jax.ShapeDtypeStruct((g, nq, rows, dv), F32 if precise else BF16),
        scratch_shapes=[pltpu.VMEM((rows, 1), F32), pltpu.VMEM((rows, 1), F32), pltpu.VMEM((rows, dv), F32)],
        compiler_params=_cparams(("arbitrary", "arbitrary", "arbitrary")), name="flash_" + mode,
    )(*args)


def _mixout_kernel(x_ref, gate_ref, ya_ref, of_ref, ol_ref, wuv_ref, wo_ref, o_ref, *, precise):
    mm = functools.partial(_mm, precise=precise)
    acc = mm(ya_ref[...], wo_ref[:V_A, :])
    for hh in range(H_B):
        r0 = V_A + hh * D_B
        acc = acc + mm(of_ref[0, hh], wo_ref[r0:r0 + D_B, :])
    yc = mm(ol_ref[0, 0], wuv_ref[0])
    for hh in range(1, H_C):
        yc = yc + mm(ol_ref[0, hh], wuv_ref[hh])
    acc = acc + mm(yc, wo_ref[V_A + QKV_B:, :])
    o_ref[...] = x_ref[...] + gate_ref[0] * acc


def _mixout(x2, gate, ya, ofox, olat, wuv, wo, n_seq, t_len, tm, precise):
    nblk = t_len // tm
    n_tok = n_seq * t_len
    mod_rows = gate.shape[1]

    def row(i):
        return (i, 0)

    return pl.pallas_call(
        functools.partial(_mixout_kernel, precise=precise), grid=(n_tok // tm,),
        in_specs=[pl.BlockSpec((tm, D_MODEL), row), pl.BlockSpec((1, mod_rows, D_MODEL), lambda i: (i // nblk, 0, 0)),
                  pl.BlockSpec((tm, V_A), row),
                  pl.BlockSpec((1, H_B, tm, D_B), lambda i: (i // nblk, 0, i % nblk, 0)),
                  pl.BlockSpec((1, H_C, tm, KV_LORA), lambda i: (i, 0, 0, 0)),
                  pl.BlockSpec((H_C, KV_LORA, V_A), lambda i: (0, 0, 0)),
                  pl.BlockSpec((D_MODEL, D_MODEL), lambda i: (0, 0))],
        out_specs=pl.BlockSpec((tm, D_MODEL), row),
        out_shape=jax.ShapeDtypeStruct((n_tok, D_MODEL), F32),
        compiler_params=_cparams(("arbitrary",)), name="mixout",
    )(x2, gate, ya, ofox, olat, wuv, wo)


def _normmod_kernel(*refs, route):
    if route:
        x_ref, sh_ref, sc_ref, g_ref, wr_ref, h_ref, ids_ref, gates_ref = refs
    else:
        x_ref, sh_ref, sc_ref, g_ref, h_ref = refs
    h = (_rms(x_ref[...]) * g_ref[...]) * (1.0 + sc_ref[0]) + sh_ref[0]
    h_ref[...] = h
    if route:
        logits = jnp.dot(h, wr_ref[...], precision=HIGHEST, preferred_element_type=F32)
        lane = lax.broadcasted_iota(jnp.int32, logits.shape, 1)
        logits = jnp.where(lane < N_EXPERTS, logits, NEG)
        m1 = jnp.max(logits, axis=-1, keepdims=True)
        i1 = jnp.min(jnp.where(logits == m1, lane, LANE), axis=-1, keepdims=True)
        rest = jnp.where(lane == i1, NEG, logits)
        m2 = jnp.max(rest, axis=-1, keepdims=True)
        i2 = jnp.min(jnp.where(rest == m2, lane, LANE), axis=-1, keepdims=True)
        e2 = jnp.exp(m2 - m1)
        g1 = 1.0 / (1.0 + e2)
        g2 = e2 / (1.0 + e2)
        ids_ref[...] = jnp.where(lane == 0, i1, i2)
        gates_ref[...] = jnp.where(lane == 0, g1, g2)


def _normmod(x2, shift, scale, g, router_pad, out, row0, n_seq, t_len, tm):
    nblk = t_len // tm
    n_tok = n_seq * t_len
    route = router_pad is not None
    off = row0 // tm

    def row(i):
        return (i, 0)

    def seq(i):
        return (i // nblk, 0, 0)

    mod_rows = shift.shape[1]
    in_specs = [pl.BlockSpec((tm, D_MODEL), row), pl.BlockSpec((1, mod_rows, D_MODEL), seq),
                pl.BlockSpec((1, mod_rows, D_MODEL), seq), pl.BlockSpec((1, D_MODEL), lambda i: (0, 0))]
    args = [x2, shift, scale, g]
    out_shape = [jax.ShapeDtypeStruct(out.shape, F32)]
    out_specs = [pl.BlockSpec((tm, D_MODEL), lambda i: (off + i, 0))]
    if route:
        in_specs.append(pl.BlockSpec((D_MODEL, LANE), lambda i: (0, 0)))
        args.append(router_pad)
        out_shape += [jax.ShapeDtypeStruct((n_tok, LANE), jnp.int32), jax.ShapeDtypeStruct((n_tok, LANE), F32)]
        out_specs += [pl.BlockSpec((tm, LANE), row), pl.BlockSpec((tm, LANE), row)]
    in_specs.append(pl.BlockSpec(memory_space=pl.ANY))
    args.append(out)
    res = pl.pallas_call(
        functools.partial(_normmod_alias_kernel, route=route), grid=(n_tok // tm,),
        in_specs=in_specs, out_specs=tuple(out_specs), out_shape=tuple(out_shape),
        input_output_aliases={len(args) - 1: 0},
        compiler_params=_cparams(("arbitrary",)), name="normmod",
    )(*args)
    return res


def _normmod_alias_kernel(*refs, route):
    n_in = 5 if route else 4
    _normmod_kernel(*refs[:n_in], *refs[n_in + 1:], route=route)


GATHER_ROWS = 256


def _gather_kernel(idx_ref, src_ref, out_ref, sem):
    base = pl.program_id(0) * GATHER_ROWS

    def row_copy(r, src_row):
        return pltpu.make_async_copy(src_ref.at[src_row], out_ref.at[base + r], sem)

    def start(r, c):
        row_copy(r, idx_ref[0, 0, r]).start()
        return c

    def wait(r, c):
        row_copy(r, 0).wait()
        return c

    lax.fori_loop(0, GATHER_ROWS, start, 0)
    lax.fori_loop(0, GATHER_ROWS, wait, 0)


def _gather_rows(src, idx):
    m = idx.shape[0]
    return pl.pallas_call(
        _gather_kernel, grid=(m // GATHER_ROWS,),
        in_specs=[pl.BlockSpec((1, 1, GATHER_ROWS), lambda i: (i, 0, 0), memory_space=pltpu.SMEM),
                  pl.BlockSpec(memory_space=pl.ANY)],
        out_specs=pl.BlockSpec(memory_space=pl.ANY),
        out_shape=jax.ShapeDtypeStruct((m, src.shape[1]), src.dtype),
        scratch_shapes=[pltpu.SemaphoreType.DMA(())],
        compiler_params=pltpu.CompilerParams(dimension_semantics=("arbitrary",)), name="gather_rows",
    )(idx.reshape(m // GATHER_ROWS, 1, GATHER_ROWS), src)


def _swiglu_kernel(te_ref, nt_ref, x_ref, wg_ref, wu_ref, wd_ref, o_ref, acc_ref, *, precise):
    mm = functools.partial(_mm, precise=precise)
    i = pl.program_id(0)
    j = pl.program_id(1)

    @pl.when(i < nt_ref[0])
    def _():
        @pl.when(j == 0)
        def _():
            acc_ref[...] = jnp.zeros(acc_ref.shape, F32)

        x = x_ref[...] if precise else x_ref[...].astype(BF16)
        a = mm(x, wg_ref[0])
        u = mm(x, wu_ref[0])
        acc_ref[...] += mm(_silu(a) * u, wd_ref[0])

        @pl.when(j == pl.num_programs(1) - 1)
        def _():
            o_ref[...] = acc_ref[...]

    @pl.when((i >= nt_ref[0]) & (j == pl.num_programs(1) - 1))
    def _():
        o_ref[...] = jnp.zeros(o_ref.shape, F32)


def _swiglu_grouped(x, tile_expert, n_tiles_used, wg, wu, wd, tm, tf, precise=False):
    m = x.shape[0]
    f = wg.shape[2]
    grid_spec = pltpu.PrefetchScalarGridSpec(
        num_scalar_prefetch=2, grid=(m // tm, f // tf),
        in_specs=[pl.BlockSpec((tm, D_MODEL), lambda i, j, te, nt: (i, 0)),
                  pl.BlockSpec((1, D_MODEL, tf), lambda i, j, te, nt: (te[i], 0, j)),
                  pl.BlockSpec((1, D_MODEL, tf), lambda i, j, te, nt: (te[i], 0, j)),
                  pl.BlockSpec((1, tf, D_MODEL), lambda i, j, te, nt: (te[i], j, 0))],
        out_specs=pl.BlockSpec((tm, D_MODEL), lambda i, j, te, nt: (i, 0)),
        scratch_shapes=[pltpu.VMEM((tm, D_MODEL), F32)])
    return pl.pallas_call(
        functools.partial(_swiglu_kernel, precise=precise), grid_spec=grid_spec,
        out_shape=jax.ShapeDtypeStruct((m, D_MODEL), F32),
        compiler_params=_cparams(("arbitrary", "arbitrary")), name="swiglu",
    )(tile_expert, n_tiles_used, x, wg, wu, wd)


def _combine_kernel(*refs, moe, final):
    refs = list(refs)
    x_ref = refs.pop(0)
    gate_ref = refs.pop(0)
    y1_ref = refs.pop(0)
    if moe:
        y2_ref = refs.pop(0)
        gates_ref = refs.pop(0)
    fg_ref = refs.pop(0) if final else None
    o_ref = refs.pop(0)
    y = y1_ref[...]
    if moe:
        gts = gates_ref[...]
        y = gts[:, 0:1] * y + gts[:, 1:2] * y2_ref[...]
    out = x_ref[...] + gate_ref[0] * y
    if final:
        out = _rms(out) * fg_ref[...]
    o_ref[...] = out


def _combine(x2, gate, y, y_row0, y2_row0, gates, final_g, n_seq, t_len, tm):
    nblk = t_len // tm
    n_tok = n_seq * t_len
    moe = gates is not None
    final = final_g is not None

    def row(i):
        return (i, 0)

    in_specs = [pl.BlockSpec((tm, D_MODEL), row), pl.BlockSpec((1, gate.shape[1], D_MODEL), lambda i: (i // nblk, 0, 0)),
                pl.BlockSpec((tm, D_MODEL), lambda i: (y_row0 // tm + i, 0))]
    args = [x2, gate, y]
    if moe:
        in_specs += [pl.BlockSpec((tm, D_MODEL), lambda i: (y2_row0 // tm + i, 0)), pl.BlockSpec((tm, LANE), row)]
        args += [y, gates]
    if final:
        in_specs.append(pl.BlockSpec((1, D_MODEL), lambda i: (0, 0)))
        args.append(final_g)
    return pl.pallas_call(
        functools.partial(_combine_kernel, moe=moe, final=final), grid=(n_tok // tm,),
        in_specs=in_specs, out_specs=pl.BlockSpec((tm, D_MODEL), row),
        out_shape=jax.ShapeDtypeStruct((n_tok, D_MODEL), F32),
        compiler_params=_cparams(("arbitrary",)), name="combine",
    )(*args)


def _pack_mixer_weights(w_in, w_gate2, b_gate, fox_b_f, qng, kvng, w_uq, w_uk, w_uv, gla_norm_g):
    offs = np.concatenate([[0], np.cumsum(IN_SPLITS)])
    cols = {n: (int(offs[i]), int(offs[i + 1])) for i, n in enumerate(
        ("gq", "gk", "gv", "ag", "ar", "fq", "fk", "fv", "bf", "cq", "ckv", "kr"))}

    def seg(n):
        return w_in[:, cols[n][0]:cols[n][1]]

    half = ROPE_C // 2
    kr = seg("kr")
    w = jnp.zeros((D_MODEL, N_PACK), F32)
    for n, c0 in (("gq", C_GQ), ("gk", C_GK), ("gv", C_GV), ("ag", C_AG), ("fq", C_FQ), ("fk", C_FK), ("fv", C_FV),
                  ("cq", C_CQ), ("ckv", C_CKV)):
        w = w.at[:, c0:c0 + cols[n][1] - cols[n][0]].set(seg(n))
    w = w.at[:, C_SM + SM_KR:C_SM + SM_KR + ROPE_C].set(kr)
    w = w.at[:, C_SM + SM_AR:C_SM + SM_AR + GATE_RANK].set(seg("ar"))
    w = w.at[:, C_SM + SM_BF:C_SM + SM_BF + H_B].set(seg("bf"))
    w = w.at[:, C_SM + SM_KRS:C_SM + SM_KRS + ROPE_C].set(jnp.concatenate([-kr[:, half:], kr[:, :half]], axis=1))
    wg2 = jnp.zeros((LANE, 2 * LANE), F32).at[SM_AR:SM_AR + GATE_RANK, :QK_A].set(w_gate2)
    bg = jnp.zeros((1, 2 * LANE), F32).at[0, :QK_A].set(b_gate)
    bf = jnp.zeros((1, LANE), F32).at[0, SM_BF:SM_BF + H_B].set(fox_b_f)
    uq = w_uq.reshape(Q_LORA, H_C, NOPE_C + ROPE_C)
    wuqn = uq[:, :, :NOPE_C].reshape(Q_LORA, H_C * NOPE_C)
    x1 = uq[:, :, NOPE_C:NOPE_C + half]
    x2 = uq[:, :, NOPE_C + half:]
    pad = jnp.zeros((Q_LORA, H_C, LANE - ROPE_C), F32)
    wr = jnp.concatenate([x1, x2, pad], axis=2).reshape(Q_LORA, H_C * LANE)
    wrs = jnp.concatenate([-x2, x1, pad], axis=2).reshape(Q_LORA, H_C * LANE)
    wuk = jnp.zeros((H_C * NOPE_C, H_C * KV_LORA), F32)
    wuv = jnp.zeros((H_C, KV_LORA, V_A), F32)
    for hh in range(H_C):
        wuk = wuk.at[hh * NOPE_C:(hh + 1) * NOPE_C, hh * KV_LORA:(hh + 1) * KV_LORA].set(w_uk[:, hh, :].T)
        wuv = wuv.at[hh, :, hh * V_C:(hh + 1) * V_C].set(w_uv[:, hh, :])
    full = dict(w_in=w, wg2=wg2, bg=bg, bf=bf, qng=qng.reshape(1, Q_LORA), kvng=kvng.reshape(1, KV_LORA), wuqn=wuqn,
                wuk=wuk, wr=wr, wrs=wrs, wuv=wuv, gnorm=jnp.tile(gla_norm_g, H_A).reshape(1, V_A))
    half_prec = dict(full)
    for n in ("w_in", "wg2", "wuqn", "wuk", "wr", "wrs", "wuv"):
        half_prec[n] = full[n].astype(BF16)
    return half_prec, full


def _rope_tables(pos):
    half = ROPE_C // 2
    inv_freq = ROPE_BASE ** (-jnp.arange(half, dtype=F32) / half)
    ang = pos.astype(F32)[:, None] * inv_freq[None, :]
    n = pos.shape[0]
    cos = jnp.concatenate([jnp.cos(ang), jnp.cos(ang), jnp.ones((n, LANE - ROPE_C), F32)], axis=1)
    sin = jnp.concatenate([jnp.sin(ang), jnp.sin(ang), jnp.zeros((n, LANE - ROPE_C), F32)], axis=1)
    return cos, sin


def _state_to_t(s):
    b = s.shape[0]
    out = jnp.zeros((b, V_A, QK_A), F32)
    for hh in range(H_A):
        out = out.at[:, hh * DV_A:(hh + 1) * DV_A, hh * DK_A:(hh + 1) * DK_A].set(jnp.swapaxes(s[:, hh], 1, 2))
    return out


def _state_from_t(s_t):
    return jnp.stack([jnp.swapaxes(s_t[:, hh * DV_A:(hh + 1) * DV_A, hh * DK_A:(hh + 1) * DK_A], 1, 2)
                      for hh in range(H_A)], axis=1)


def _round_up(a, b):
    return (a + b - 1) // b * b


def kernel(x_prompt, x_sample, c_prompt, c_sample, cache_fox_k, cache_fox_v, cache_fox_logf, cache_mla_ckv, cache_mla_krope, state_gla, ada_w, ada_b, norm_mix_g, norm_ffn_g, w_in, gla_w_gate2, gla_b_gate, gla_norm_g, fox_b_f, mla_q_norm_g, mla_kv_norm_g, mla_w_uq, mla_w_uk, mla_w_uv, w_out, ffn_w_gate, ffn_w_up, ffn_w_down, moe_router, moe_w_gate, moe_w_up, moe_w_down, final_norm_g):
    bp, tp, _ = x_prompt.shape
    bs, ts, _ = x_sample.shape
    past = cache_fox_k.shape[2]
    np_tok, ns_tok = bp * tp, bs * ts
    n_all = np_tok + ns_tok

    tm_p, tm_s = 256, ns_tok
    tq_fox, tk = 512, 512
    tk_pad = past + tk
    kv_len_s = past + ts

    nc = _round_up(bp + bs, 8)
    c_all = jnp.zeros((nc, D_MODEL), F32).at[:bp].set(c_prompt).at[bp:bp + bs].set(c_sample)
    mod = _ada(c_all, ada_w, ada_b)

    cos_p, sin_p = _rope_tables(jnp.arange(tp))
    cos_s, sin_s = _rope_tables(past + jnp.tile(jnp.arange(ts), bs))

    def seq_major(a):
        hh, d = a.shape[1], a.shape[3]
        return jnp.transpose(a.reshape(hh, bs, ts, d), (1, 0, 2, 3))

    def tok_major(a):
        hh, d = a.shape[1], a.shape[3]
        return jnp.transpose(a, (1, 0, 2, 3)).reshape(1, hh, bs * ts, d)

    cl = jnp.transpose(cache_fox_logf.astype(F32), (0, 1, 3, 2)).reshape(DEPTH * bs * H_B, past)
    f_cache = _cumsum(cl, jnp.zeros((cl.shape[0], 1), F32), 512).reshape(DEPTH, bs * H_B, past)

    xp = x_prompt.reshape(np_tok, D_MODEL)
    xs = x_sample.reshape(ns_tok, D_MODEL)
    p_states = [[] for _ in range(6)]
    s_states = [[] for _ in range(6)]

    for l in range(DEPTH):
        mods = [mod[l, :, i * D_MODEL:(i + 1) * D_MODEL] for i in range(6)]
        mp = [m[:bp].reshape(bp, 1, D_MODEL) for m in mods]
        ms = [jnp.repeat(m[bp:bp + bs], ts, axis=0).reshape(1, ns_tok, D_MODEL) for m in mods]
        pw, pw32 = _pack_mixer_weights(w_in[l], gla_w_gate2[l], gla_b_gate[l], fox_b_f[l], mla_q_norm_g[l],
                                       mla_kv_norm_g[l], mla_w_uq[l], mla_w_uk[l], mla_w_uv[l], gla_norm_g[l])
        wo = w_out[l].astype(BF16)
        g_mix = norm_mix_g[l].reshape(1, D_MODEL)
        g_ffn = norm_ffn_g[l].reshape(1, D_MODEL)

        (gq, gk, gv, ag, gla, fqh, fkh, fvh, fk, fv, ckv, kc, small, qs) = _inproj(
            xp, mp[0], mp[1], g_mix, pw, cos_p, sin_p, bp, tp, tm_p, False)
        logf = small[:, SM_BF:SM_BF + H_B]
        krope = small[:, SM_KR:SM_KR + ROPE_C]
        f_rows = jnp.transpose(logf.reshape(bp, tp, H_B), (0, 2, 1)).reshape(bp * H_B, tp)
        f_cum = _cumsum(f_rows, jnp.zeros((bp * H_B, 1), F32), 512)
        ya, s_t = _gla(gq, gk, gv, gla, ag, pw["gnorm"], jnp.zeros((bp, V_A, QK_A), F32), bp, tp, 512, CHUNK, False)
        g_fox = bp * H_B
        o_fox = _flash(fqh.reshape(g_fox, tp // tq_fox, tq_fox, D_B), fkh.reshape(g_fox, tp, D_B),
                       fvh.reshape(g_fox, tp, D_B), f_cum.reshape(g_fox, tp // tq_fox, tq_fox, 1),
                       f_cum.reshape(g_fox, 1, tp), tq=tq_fox, rep=1, tk=tk, q0=0, mode="causal", kv_len=tp, dv=D_B)
        o_mla = _flash(qs.reshape(bp, tp // tm_p, H_C * tm_p, 2 * LANE), kc.reshape(bp, tp, 2 * LANE), None, None, None,
                       tq=tm_p, rep=H_C, tk=tk, q0=0, mode="chunk", kv_len=tp, dv=KV_LORA)
        xp = _mixout(xp, mp[2], ya, o_fox.reshape(bp, H_B, tp, D_B),
                     o_mla.reshape(np_tok // tm_p, H_C, tm_p, KV_LORA), pw["wuv"], wo, bp, tp, tm_p, False)
        for i, st in enumerate((_state_from_t(s_t), fk.reshape(bp, tp, H_B, D_B), fv.reshape(bp, tp, H_B, D_B),
                                logf.reshape(bp, tp, H_B), ckv.reshape(bp, tp, KV_LORA), krope.reshape(bp, tp, ROPE_C))):
            p_states[i].append(st)

        (gq, gk, gv, ag, gla, fqh, fkh, fvh, fk, fv, ckv, kc, small, qs) = _inproj(
            xs, ms[0], ms[1], g_mix, pw32, cos_s, sin_s, 1, ns_tok, tm_s, True)
        logf = small[:, SM_BF:SM_BF + H_B]
        krope = small[:, SM_KR:SM_KR + ROPE_C]
        g_fox = bs * H_B
        f_rows = jnp.transpose(logf.reshape(bs, ts, H_B), (0, 2, 1)).reshape(g_fox, ts)
        f_new = _cumsum(f_rows, f_cache[l][:, past - 1:past], ts)
        f_keys = jnp.concatenate([f_cache[l], f_new, jnp.zeros((g_fox, tk_pad - kv_len_s), F32)], axis=1)
        ya, s_t = _gla(gq, gk, gv, gla, ag, pw["gnorm"], _state_to_t(state_gla[l].astype(F32)), bs, ts, ts, ts, True)

        def keys(cache, new):
            c = jnp.transpose(cache.astype(F32), (0, 2, 1, 3))
            full = jnp.concatenate([c, new, jnp.zeros((bs, H_B, tk_pad - kv_len_s, D_B), F32)], axis=2)
            return full.reshape(g_fox, tk_pad, D_B)

        o_fox = _flash(seq_major(fqh).reshape(g_fox, 1, ts, D_B), keys(cache_fox_k[l], seq_major(fkh)),
                       keys(cache_fox_v[l], seq_major(fvh)),
                       f_new.reshape(g_fox, 1, ts, 1), f_keys.reshape(g_fox, 1, tk_pad),
                       tq=ts, rep=1, tk=tk, q0=past, mode="causal", kv_len=kv_len_s, dv=D_B, precise=True)
        kc_cache = jnp.concatenate([cache_mla_ckv[l].astype(F32), cache_mla_krope[l].astype(F32),
                                    jnp.zeros((bs, past, LANE - ROPE_C), F32)], axis=2)
        kc_all = jnp.concatenate([kc_cache, kc.reshape(bs, ts, 2 * LANE),
                                  jnp.zeros((bs, tk_pad - kv_len_s, 2 * LANE), F32)], axis=1)
        o_mla = _flash(seq_major(qs).reshape(bs, 1, H_C * ts, 2 * LANE), kc_all, None, None, None,
                       tq=ts, rep=H_C, tk=tk, q0=past, mode="chunk", kv_len=kv_len_s, dv=KV_LORA, precise=True)
        xs = _mixout(xs, ms[2], ya, tok_major(o_fox.reshape(bs, H_B, ts, D_B)),
                     tok_major(o_mla.reshape(bs, H_C, ts, KV_LORA)), pw32["wuv"], w_out[l], 1, ns_tok, tm_s, True)
        for i, st in enumerate((_state_from_t(s_t), fk.reshape(bs, ts, H_B, D_B), fv.reshape(bs, ts, H_B, D_B),
                                logf.reshape(bs, ts, H_B), ckv.reshape(bs, ts, KV_LORA), krope.reshape(bs, ts, ROPE_C))):
            s_states[i].append(st)

        last = l == DEPTH - 1
        fg = final_norm_g.reshape(1, D_MODEL) if last else None
        tm_f = 512
        if l % 2 == 0:
            j = l // 2
            h_p = _normmod(xp, mp[3], mp[4], g_ffn, None, jnp.zeros((np_tok, D_MODEL), F32), 0, bp, tp, tm_p)[0]
            h_s = _normmod(xs, ms[3], ms[4], g_ffn, None, jnp.zeros((ns_tok, D_MODEL), F32), 0, 1, ns_tok, tm_s)[0]
            n_tiles = np_tok // tm_f
            y_p = _swiglu_grouped(h_p, jnp.zeros((n_tiles,), jnp.int32), jnp.full((1,), n_tiles, jnp.int32),
                                  ffn_w_gate[j:j + 1].astype(BF16), ffn_w_up[j:j + 1].astype(BF16),
                                  ffn_w_down[j:j + 1].astype(BF16), tm_f, 1408)
            y_s = _swiglu_grouped(h_s, jnp.zeros((1,), jnp.int32), jnp.ones((1,), jnp.int32),
                                  ffn_w_gate[j:j + 1], ffn_w_up[j:j + 1], ffn_w_down[j:j + 1], tm_s, 1408, precise=True)
            xp = _combine(xp, mp[5], y_p, 0, 0, None, fg, bp, tp, tm_p)
            xs = _combine(xs, ms[5], y_s, 0, 0, None, fg, 1, ns_tok, tm_s)
        else:
            j = l // 2
            router_pad = jnp.zeros((D_MODEL, LANE), F32).at[:, :N_EXPERTS].set(moe_router[j])
            h_all = jnp.zeros((n_all, D_MODEL), F32)
            h_all, ids_p, gates_p = _normmod(xp, mp[3], mp[4], g_ffn, router_pad, h_all, 0, bp, tp, tm_p)
            h_all, ids_s, gates_s = _normmod(xs, ms[3], ms[4], g_ffn, router_pad, h_all, np_tok, 1, ns_tok, tm_s)
            ids = jnp.concatenate([ids_p[:, :2], ids_s[:, :2]], axis=0)
            e = jnp.transpose(ids).reshape(-1)
            onehot = (e[:, None] == jnp.arange(N_EXPERTS)[None, :]).astype(jnp.int32)
            rank = jnp.sum((jnp.cumsum(onehot, axis=0) - onehot) * onehot, axis=1)
            counts = jnp.sum(onehot, axis=0)
            padded = (counts + tm_f - 1) // tm_f * tm_f
            ends = jnp.cumsum(padded)
            starts = ends - padded
            pos = starts[e] + rank
            m_pad = _round_up(2 * n_all + N_EXPERTS * (tm_f - 1), tm_f)
            token = jnp.tile(jnp.arange(n_all, dtype=jnp.int32), 2)
            src = jnp.zeros((m_pad,), jnp.int32).at[pos].set(token)
            n_tiles = m_pad // tm_f
            tile_row0 = jnp.arange(n_tiles, dtype=jnp.int32) * tm_f
            tile_expert = jnp.minimum(jnp.sum((ends[None, :] <= tile_row0[:, None]).astype(jnp.int32), axis=1),
                                      N_EXPERTS - 1)
            n_used = (ends[-1] // tm_f).astype(jnp.int32).reshape(1)
            x_sorted = _gather_rows(h_all, src)
            y = _swiglu_grouped(x_sorted, tile_expert, n_used, moe_w_gate[j].astype(BF16), moe_w_up[j].astype(BF16),
                                moe_w_down[j].astype(BF16), tm_f, 512)
            n_back = _round_up(n_all, GATHER_ROWS)
            back = jnp.zeros((2 * n_back,), jnp.int32).at[:n_all].set(pos[:n_all]).at[n_back:n_back + n_all].set(pos[n_all:])
            yg = _gather_rows(y, back)
            xp = _combine(xp, mp[5], yg, 0, n_back, gates_p, fg, bp, tp, tm_p)
            xs = _combine(xs, ms[5], yg, np_tok, n_back + np_tok, gates_s, fg, 1, ns_tok, tm_s)

    outs_p = [jnp.stack(s, axis=0) for s in p_states]
    outs_s = [jnp.stack(s, axis=0) for s in s_states]
    return (xp.reshape(bp, tp, D_MODEL), xs.reshape(bs, ts, D_MODEL), *outs_p, *outs_s)
```

```python
import functools

import numpy as np
import jax
import jax.numpy as jnp
from jax import lax
from jax.experimental import pallas as pl
from jax.experimental.pallas import tpu as pltpu

F32 = jnp.float32
BF16 = jnp.bfloat16
HIGHEST = lax.Precision.HIGHEST

D_MODEL = 1024
DEPTH = 2
CHUNK = 64
EPS = 1e-6
H_A, DK_A, DV_A = 6, 32, 64
GATE_RANK = 16
GATE_TAU = 16.0
H_B, D_B = 4, 64
H_C, NOPE_C, ROPE_C, V_C = 6, 64, 32, 64
Q_LORA, KV_LORA = 256, 128
ROPE_BASE = 10000.0
N_EXPERTS = 8
IN_SPLITS = (H_A * DK_A, H_A * DK_A, H_A * DV_A, H_A * DV_A, GATE_RANK,
             H_B * D_B, H_B * D_B, H_B * D_B, H_B, Q_LORA, KV_LORA, ROPE_C)

QK_A = H_A * DK_A
V_A = H_A * DV_A
QKV_B = H_B * D_B
LANE = 128
NEG = -1e30

C_GQ, C_GK, C_GV, C_AG = 0, 256, 512, 896
C_FQ, C_FK, C_FV = 1280, 1536, 1792
C_CQ, C_CKV, C_SM = 2048, 2304, 2432
N_PACK = 2560
SM_KR, SM_AR, SM_BF, SM_KRS = 0, 32, 48, 64

GLA_SUB = 16
FLASH_ROW_GROUP = 256
LOG2E = float(np.log2(np.e))
VMEM_LIMIT = 56 * 1024 * 1024


def _cparams(sem):
    return pltpu.CompilerParams(dimension_semantics=sem, vmem_limit_bytes=VMEM_LIMIT)


def _log_sigmoid(z):
    return jnp.minimum(z, 0.0) - jnp.log1p(jnp.exp(-jnp.abs(z)))


def _silu(z):
    return z * (1.0 / (1.0 + jnp.exp(-z)))


def _rms(x):
    return x * lax.rsqrt(jnp.mean(x * x, axis=-1, keepdims=True) + EPS)


def _dot(a, b):
    return jnp.dot(a, b, preferred_element_type=F32)


def _dot_nt(a, b):
    return lax.dot_general(a, b, (((1,), (1,)), ((), ())), preferred_element_type=F32)


def _dot_tn(a, b):
    return lax.dot_general(a, b, (((0,), (0,)), ((), ())), preferred_element_type=F32)


def _mm(a, b, dims=(((1,), (0,)), ((), ())), *, precise):
    if precise:
        return lax.dot_general(a.astype(F32), b.astype(F32), dims, precision=HIGHEST, preferred_element_type=F32)
    return lax.dot_general(a.astype(BF16), b.astype(BF16), dims, preferred_element_type=F32)


_NT = (((1,), (1,)), ((), ()))
_TN = (((0,), (0,)), ((), ()))

SUBLANE = 8
TOK_TILE = (D_MODEL // LANE, LANE)
assert TOK_TILE[0] == SUBLANE


def _block_rows(tm, arr, index_map):
    if arr.ndim == 3:
        return pl.BlockSpec((tm,) + TOK_TILE, lambda *a: index_map(*a) + (0,))
    return pl.BlockSpec((tm, D_MODEL), index_map)


def _load_rows(ref):
    if len(ref.shape) == 3:
        return jnp.concatenate([ref[:, s, :] for s in range(TOK_TILE[0])], axis=-1)
    return ref[...]


def _store_rows(ref, val):
    if len(ref.shape) == 3:
        for s in range(TOK_TILE[0]):
            ref[:, s, :] = val[:, s * LANE:(s + 1) * LANE]
    else:
        ref[...] = val


def _ada_kernel(c_ref, w_ref, b_ref, o_ref):
    s = _silu(c_ref[...])
    o_ref[0] = jnp.dot(s, w_ref[0], precision=HIGHEST, preferred_element_type=F32) + b_ref[0]


def _ada(c_all, ada_w, ada_b):
    nc = c_all.shape[0]
    tn = 1536
    return pl.pallas_call(
        _ada_kernel,
        grid=(DEPTH, 6 * D_MODEL // tn),
        in_specs=[pl.BlockSpec((nc, D_MODEL), lambda l, j: (0, 0)),
                  pl.BlockSpec((1, D_MODEL, tn), lambda l, j: (l, 0, j)),
                  pl.BlockSpec((1, 1, tn), lambda l, j: (l, 0, j))],
        out_specs=pl.BlockSpec((1, nc, tn), lambda l, j: (l, 0, j)),
        out_shape=jax.ShapeDtypeStruct((DEPTH, nc, 6 * D_MODEL), F32),
        compiler_params=_cparams(("arbitrary", "arbitrary")),
        name="ada",
    )(c_all, ada_w, ada_b.reshape(DEPTH, 1, 6 * D_MODEL))


def _inproj_kernel(x_ref, sh_ref, sc_ref, g_ref, w_ref, wg2_ref, bg_ref, bf_ref, qng_ref, kvng_ref,
                   wuqn_ref, wuk_ref, wr_ref, wrs_ref, cos_ref, sin_ref,
                   gq_ref, gk_ref, gv_ref, ag_ref, gla_ref, fqh_ref, fkh_ref, fvh_ref, fk_ref, fv_ref,
                   ckv_ref, kc_ref, small_ref, qs_ref, *, precise):
    mm = functools.partial(_mm, precise=precise)
    act = kc_ref.dtype
    x = x_ref[...]
    h = (_rms(x) * g_ref[...]) * (1.0 + sc_ref[0]) + sh_ref[0]
    p = mm(h, w_ref[...])

    gq_ref[...] = p[:, C_GQ:C_GQ + QK_A] * (DK_A ** -0.5)
    gk_ref[...] = p[:, C_GK:C_GK + QK_A]
    gv_ref[...] = p[:, C_GV:C_GV + V_A]
    ag_ref[...] = p[:, C_AG:C_AG + V_A]
    sm = p[:, C_SM:C_SM + LANE]
    z = mm(sm, wg2_ref[...]) + bg_ref[...]
    gla_ref[...] = _log_sigmoid(z[:, :QK_A]) * (1.0 / GATE_TAU)

    fq = p[:, C_FQ:C_FQ + QKV_B] * (D_B ** -0.5 * LOG2E)
    fk = p[:, C_FK:C_FK + QKV_B]
    fv = p[:, C_FV:C_FV + QKV_B]
    fk_ref[...] = fk
    fv_ref[...] = fv
    for hh in range(H_B):
        sl = slice(hh * D_B, (hh + 1) * D_B)
        fqh_ref[0, hh] = fq[:, sl].astype(act)
        fkh_ref[0, hh] = fk[:, sl].astype(act)
        fvh_ref[0, hh, :, :D_B] = fv[:, sl].astype(act)
        fvh_ref[0, hh, :, D_B:] = jnp.ones((fv.shape[0], D_B), act)

    cos = cos_ref[...]
    sin = sin_ref[...]
    lane = lax.broadcasted_iota(jnp.int32, sm.shape, 1)
    kr = sm * cos + pltpu.roll(sm, LANE - SM_KRS, 1) * sin
    logf = _log_sigmoid(sm + bf_ref[...])
    small_ref[...] = jnp.where((lane >= SM_BF) & (lane < SM_BF + H_B), logf, kr)

    ckv = _rms(p[:, C_CKV:C_CKV + KV_LORA]) * kvng_ref[...]
    ckv_ref[...] = ckv
    kc_ref[:, :KV_LORA] = ckv.astype(act)
    kc_ref[:, KV_LORA:] = jnp.where(lane < ROPE_C, kr, 0.0).astype(act)
    cqn = _rms(p[:, C_CQ:C_CQ + Q_LORA]) * qng_ref[...]
    nope = mm(cqn, wuqn_ref[...])
    qlat = mm(nope, wuk_ref[...])
    qa = mm(cqn, wr_ref[...])
    qb = mm(cqn, wrs_ref[...])
    scale = (NOPE_C + ROPE_C) ** -0.5 * LOG2E
    for hh in range(H_C):
        sl = slice(hh * LANE, (hh + 1) * LANE)
        qs_ref[0, hh, :, :KV_LORA] = (qlat[:, sl] * scale).astype(act)
        qs_ref[0, hh, :, KV_LORA:] = ((qa[:, sl] * cos + qb[:, sl] * sin) * scale).astype(act)


def _inproj(x2, shift, scale, g, pw, cos_tab, sin_tab, n_seq, t_len, tm, precise):
    n_tok = n_seq * t_len
    nblk = t_len // tm
    act = F32 if precise else BF16
    mod_rows = shift.shape[1]

    def row(i):
        return (i, 0)

    def seq(i):
        return (i // nblk, 0, 0)

    def const2(i):
        return (0, 0)

    def tab(i):
        return (i % nblk, 0)

    def headmajor(i):
        return (i // nblk, 0, i % nblk, 0)

    sds = jax.ShapeDtypeStruct
    out_shape = (
        sds((n_tok, QK_A), F32), sds((n_tok, QK_A), F32), sds((n_tok, V_A), F32), sds((n_tok, V_A), F32),
        sds((n_tok, QK_A), F32),
        sds((n_seq, H_B, t_len, D_B), act), sds((n_seq, H_B, t_len, D_B), act), sds((n_seq, H_B, t_len, 2 * D_B), act),
        sds((n_tok, QKV_B), F32), sds((n_tok, QKV_B), F32),
        sds((n_tok, KV_LORA), F32), sds((n_tok, 2 * LANE), act), sds((n_tok, LANE), F32),
        sds((n_tok // tm, H_C, tm, 2 * LANE), act),
    )
    out_specs = (
        pl.BlockSpec((tm, QK_A), row), pl.BlockSpec((tm, QK_A), row), pl.BlockSpec((tm, V_A), row),
        pl.BlockSpec((tm, V_A), row), pl.BlockSpec((tm, QK_A), row),
        pl.BlockSpec((1, H_B, tm, D_B), headmajor), pl.BlockSpec((1, H_B, tm, D_B), headmajor),
        pl.BlockSpec((1, H_B, tm, 2 * D_B), headmajor),
        pl.BlockSpec((tm, QKV_B), row), pl.BlockSpec((tm, QKV_B), row),
        pl.BlockSpec((tm, KV_LORA), row), pl.BlockSpec((tm, 2 * LANE), row), pl.BlockSpec((tm, LANE), row),
        pl.BlockSpec((1, H_C, tm, 2 * LANE), lambda i: (i, 0, 0, 0)),
    )
    in_specs = [
        pl.BlockSpec((tm, D_MODEL), row), pl.BlockSpec((1, mod_rows, D_MODEL), seq),
        pl.BlockSpec((1, mod_rows, D_MODEL), seq), pl.BlockSpec((1, D_MODEL), const2),
        pl.BlockSpec((D_MODEL, N_PACK), const2), pl.BlockSpec((LANE, 2 * LANE), const2),
        pl.BlockSpec((1, 2 * LANE), const2), pl.BlockSpec((1, LANE), const2),
        pl.BlockSpec((1, Q_LORA), const2), pl.BlockSpec((1, KV_LORA), const2),
        pl.BlockSpec((Q_LORA, H_C * NOPE_C), const2), pl.BlockSpec((H_C * NOPE_C, H_C * KV_LORA), const2),
        pl.BlockSpec((Q_LORA, H_C * LANE), const2), pl.BlockSpec((Q_LORA, H_C * LANE), const2),
        pl.BlockSpec((tm, LANE), tab), pl.BlockSpec((tm, LANE), tab),
    ]
    return pl.pallas_call(
        functools.partial(_inproj_kernel, precise=precise), grid=(n_tok // tm,), in_specs=in_specs,
        out_specs=out_specs, out_shape=out_shape, compiler_params=_cparams(("arbitrary",)), name="inproj",
    )(x2, shift, scale, g, pw["w_in"], pw["wg2"], pw["bg"], pw["bf"], pw["qng"], pw["kvng"],
      pw["wuqn"], pw["wuk"], pw["wr"], pw["wrs"], cos_tab, sin_tab)


def _cumsum_kernel(x_ref, init_ref, o_ref, o2_ref, *, tb):
    n = x_ref.shape[1]
    upper = (lax.broadcasted_iota(jnp.int32, (tb, tb), 0) <= lax.broadcasted_iota(jnp.int32, (tb, tb), 1)).astype(F32)
    carry = init_ref[...]
    for j in range(n // tb):
        blk = jnp.dot(x_ref[:, j * tb:(j + 1) * tb], upper, precision=HIGHEST, preferred_element_type=F32) + carry
        o_ref[:, j * tb:(j + 1) * tb] = blk
        o2_ref[:, j * tb:(j + 1) * tb] = blk * LOG2E
        carry = blk[:, tb - 1:tb]


def _cumsum(x, init, tb):
    return pl.pallas_call(
        functools.partial(_cumsum_kernel, tb=tb),
        out_shape=(jax.ShapeDtypeStruct(x.shape, F32), jax.ShapeDtypeStruct(x.shape, F32)), name="cumsum",
    )(x, init)


def _gla_kernel(q_ref, k_ref, v_ref, la_ref, ag_ref, gn_ref, s0_ref, y_ref, sout_ref, s_ref, *, chunk, n_chunks,
                precise):
    mm = functools.partial(_mm, precise=precise)
    c = chunk
    nsub = c // GLA_SUB

    @pl.when(pl.program_id(1) == 0)
    def _():
        s_ref[...] = s0_ref[0]

    lane_qk = lax.broadcasted_iota(jnp.int32, (GLA_SUB, QK_A), 1) // DK_A
    lane_v = lax.broadcasted_iota(jnp.int32, (GLA_SUB, V_A), 1) // DV_A
    bd = (lax.broadcasted_iota(jnp.int32, (V_A, QK_A), 0) // DV_A) == (lax.broadcasted_iota(jnp.int32, (V_A, QK_A), 1) // DK_A)
    tril = (lax.broadcasted_iota(jnp.int32, (c, c), 0) >= lax.broadcasted_iota(jnp.int32, (c, c), 1)).astype(F32)
    hm = (lax.broadcasted_iota(jnp.int32, (V_A, V_A), 0) // DV_A) == (lax.broadcasted_iota(jnp.int32, (V_A, V_A), 1) // DV_A)
    head_mean = jnp.where(hm, 1.0 / DV_A, 0.0).astype(F32)

    def body(ci, carry):
        r = pl.multiple_of(ci * c, c)
        q = q_ref[pl.ds(r, c), :]
        k = k_ref[pl.ds(r, c), :]
        v = v_ref[pl.ds(r, c), :]
        la = la_ref[pl.ds(r, c), :]
        b = jnp.dot(tril, la, precision=HIGHEST, preferred_element_type=F32)
        s_t = s_ref[...]
        vb = v if precise else v.astype(BF16)
        o_inter = mm(q * jnp.exp(b), s_t, _NT)
        outs = []
        for i in range(nsub):
            r0 = i * GLA_SUB
            r1 = r0 + GLA_SUB
            bi = b[r0 - 1:r0] if i > 0 else jnp.zeros((1, QK_A), F32)
            qi = q[r0:r1] * jnp.exp(b[r0:r1] - bi)
            kk = k[:r1] * jnp.exp(bi - b[:r1])
            qst = jnp.concatenate([jnp.where(lane_qk == hh, qi, 0.0) for hh in range(H_A)], axis=0)
            att = mm(qst, kk, _NT)
            t_idx = r0 + lax.broadcasted_iota(jnp.int32, att.shape, 0) % GLA_SUB
            s_idx = lax.broadcasted_iota(jnp.int32, att.shape, 1)
            att = jnp.where(s_idx <= t_idx, att, 0.0)
            oi = mm(att, vb[:r1])
            o = jnp.zeros((GLA_SUB, V_A), F32)
            for hh in range(H_A):
                o = o + jnp.where(lane_v == hh, oi[hh * GLA_SUB:(hh + 1) * GLA_SUB], 0.0)
            outs.append(o)
        o = jnp.concatenate(outs, axis=0) + o_inter if nsub > 1 else outs[0] + o_inter
        b_last = b[c - 1:c]
        kd = k * jnp.exp(b_last - b)
        s_ref[...] = s_t * jnp.exp(b_last) + jnp.where(bd, mm(vb, kd, _TN), 0.0)
        ms = jnp.dot(o * o, head_mean, precision=HIGHEST, preferred_element_type=F32)
        y = o * lax.rsqrt(ms + EPS) * gn_ref[...] * _silu(ag_ref[pl.ds(r, c), :])
        y_ref[pl.ds(r, c), :] = y.astype(y_ref.dtype)
        return carry

    lax.fori_loop(0, n_chunks, body, 0)

    @pl.when(pl.program_id(1) == pl.num_programs(1) - 1)
    def _():
        sout_ref[0] = s_ref[...]


def _gla(gq, gk, gv, gla, ag, gnorm, s0_t, n_seq, t_len, tb, chunk, precise):
    nblk = t_len // tb

    def row(b, j):
        return (b * nblk + j, 0)

    def st(b, j):
        return (b, 0, 0)

    n_tok = n_seq * t_len
    return pl.pallas_call(
        functools.partial(_gla_kernel, chunk=chunk, n_chunks=tb // chunk, precise=precise),
        grid=(n_seq, nblk),
        in_specs=[pl.BlockSpec((tb, QK_A), row), pl.BlockSpec((tb, QK_A), row), pl.BlockSpec((tb, V_A), row),
                  pl.BlockSpec((tb, QK_A), row), pl.BlockSpec((tb, V_A), row),
                  pl.BlockSpec((1, V_A), lambda b, j: (0, 0)), pl.BlockSpec((1, V_A, QK_A), st)],
        out_specs=(pl.BlockSpec((tb, V_A), row), pl.BlockSpec((1, V_A, QK_A), st)),
        out_shape=(jax.ShapeDtypeStruct((n_tok, V_A), F32 if precise else BF16),
                   jax.ShapeDtypeStruct((n_seq, V_A, QK_A), F32)),
        scratch_shapes=[pltpu.VMEM((V_A, QK_A), F32)],
        compiler_params=_cparams(("arbitrary", "arbitrary")), name="gla",
    )(gq, gk, gv, gla, ag, gnorm, s0_t)


def _flash_kernel(*refs, tq, rep, tk, q0, mode, kv_len, dv, bias, v_from_k, precise):
    mm = functools.partial(_mm, precise=precise)
    refs = list(refs)
    q_ref = refs.pop(0)
    k_ref = refs.pop(0)
    v_ref = k_ref if v_from_k else refs.pop(0)
    fq_ref = refs.pop(0) if bias else None
    fk_ref = refs.pop(0) if bias else None
    o_ref, m_ref, acc_ref = refs
    i = pl.program_id(1)
    j = pl.program_id(2)
    rows = rep * tq

    @pl.when(j == 0)
    def _():
        m_ref[...] = jnp.full(m_ref.shape, NEG, F32)
        acc_ref[...] = jnp.zeros(acc_ref.shape, F32)

    first_q = q0 + i * tq
    last_q = first_q + tq - 1
    if mode == "chunk":
        vis_all = (first_q // CHUNK) * CHUNK + CHUNK - 1
        vis_any = (last_q // CHUNK) * CHUNK + CHUNK - 1
    else:
        vis_all = first_q
        vis_any = last_q
    vis_all = jnp.minimum(vis_all, kv_len - 1)
    vis_any = jnp.minimum(vis_any, kv_len - 1)
    k_start = j * tk
    k_end = k_start + tk - 1

    grp = min(rows, FLASH_ROW_GROUP)

    def step(masked):
        k = k_ref[0]
        if v_from_k:
            v = jnp.where(lax.broadcasted_iota(jnp.int32, k.shape, 1) < dv, k, jnp.ones_like(k))
        else:
            v = v_ref[0]
        for g in range(rows // grp):
            rs = slice(g * grp, (g + 1) * grp)
            s = mm(q_ref[0, 0, rs, :], k, _NT)
            if bias:
                s = s + fq_ref[0, 0, rs, :] - fk_ref[0]
            if masked:
                qpos = first_q + (g * grp + lax.broadcasted_iota(jnp.int32, (grp, tk), 0)) % tq
                kpos = k_start + lax.broadcasted_iota(jnp.int32, (grp, tk), 1)
                if mode == "chunk":
                    ok = (kpos // CHUNK) <= (qpos // CHUNK)
                else:
                    ok = kpos <= qpos
                ok = ok & (kpos < kv_len)
                s = jnp.where(ok, s, NEG)
            chunks = [s[:, c * LANE:(c + 1) * LANE] for c in range(tk // LANE)]
            smax = chunks[0]
            for ch in chunks[1:]:
                smax = jnp.maximum(smax, ch)
            m_prev = m_ref[rs, :]
            m_new = jnp.maximum(m_prev, jnp.max(smax, axis=-1, keepdims=True))
            alpha = jnp.exp2(m_prev - m_new)
            p = jnp.concatenate([jnp.exp2(ch - m_new) for ch in chunks], axis=1)
            acc = acc_ref[rs, :]
            alpha_w = alpha if acc.shape[1] == LANE else jnp.concatenate([alpha] * (acc.shape[1] // LANE), axis=1)
            acc_ref[rs, :] = alpha_w * acc + mm(p, v)
            m_ref[rs, :] = m_new

    @pl.when(k_end <= vis_all)
    def _():
        step(False)

    @pl.when((k_end > vis_all) & (k_start <= vis_any))
    def _():
        step(True)

    @pl.when(j == pl.num_programs(2) - 1)
    def _():
        acc = acc_ref[...]
        if dv == LANE:
            out = acc[:, :dv] / acc[:, dv:]
        else:
            out = (acc / pltpu.roll(acc, LANE - dv, 1))[:, :dv]
        o_ref[0, 0] = out.astype(o_ref.dtype)


def _flash(q, k, v, fq, fk, *, tq, rep, tk, q0, mode, kv_len, dv, precise=False):
    g, nq, rows, dqk = q.shape
    t_k = k.shape[1]
    nk = t_k // tk
    bias = fq is not None
    v_from_k = v is None

    def last_blk(i):
        last_q = q0 + (i + 1) * tq - 1
        vis = (last_q // CHUNK) * CHUNK + CHUNK - 1 if mode == "chunk" else last_q
        return jnp.minimum(vis, kv_len - 1) // tk

    def qmap(b, i, j):
        return (b, i, 0, 0)

    def kmap(b, i, j):
        return (b, jnp.minimum(j, last_blk(i)), 0)

    def fkmap(b, i, j):
        return (b, 0, jnp.minimum(j, last_blk(i)))

    in_specs = [pl.BlockSpec((1, 1, rows, dqk), qmap), pl.BlockSpec((1, tk, dqk), kmap)]
    args = [q, k]
    dva = dqk if v_from_k else v.shape[2]
    assert dva % LANE == 0 and dva > dv
    if not v_from_k:
        in_specs.append(pl.BlockSpec((1, tk, dva), kmap))
        args.append(v)
    if bias:
        in_specs += [pl.BlockSpec((1, 1, rows, 1), qmap), pl.BlockSpec((1, 1, tk), fkmap)]
        args += [fq, fk]
    return pl.pallas_call(
        functools.partial(_flash_kernel, tq=tq, rep=rep, tk=tk, q0=q0, mode=mode, kv_len=kv_len, dv=dv,
                          bias=bias, v_from_k=v_from_k, precise=precise),
        grid=(g, nq, nk), in_specs=in_specs,
        out_specs=pl.BlockSpec((1, 1, rows, dv), qmap),
        out_shape=jax.ShapeDtypeStruct((g, nq, rows, dv), F32 if precise else BF16),
        scratch_shapes=[pltpu.VMEM((rows, LANE), F32), pltpu.VMEM((rows, dva), F32)],
        compiler_params=_cparams(("arbitrary", "arbitrary", "arbitrary")), name="flash_" + mode,
    )(*args)


def _mixout_kernel(x_ref, gate_ref, ya_ref, of_ref, ol_ref, wuv_ref, wo_ref, o_ref, *, precise):
    mm = functools.partial(_mm, precise=precise)
    acc = mm(ya_ref[...], wo_ref[:V_A, :])
    for hh in range(H_B):
        r0 = V_A + hh * D_B
        acc = acc + mm(of_ref[0, hh], wo_ref[r0:r0 + D_B, :])
    yc = mm(ol_ref[0, 0], wuv_ref[0])
    for hh in range(1, H_C):
        yc = yc + mm(ol_ref[0, hh], wuv_ref[hh])
    acc = acc + mm(yc, wo_ref[V_A + QKV_B:, :])
    o_ref[...] = x_ref[...] + gate_ref[0] * acc


def _mixout(x2, gate, ya, ofox, olat, wuv, wo, n_seq, t_len, tm, precise):
    nblk = t_len // tm
    n_tok = n_seq * t_len
    mod_rows = gate.shape[1]

    def row(i):
        return (i, 0)

    return pl.pallas_call(
        functools.partial(_mixout_kernel, precise=precise), grid=(n_tok // tm,),
        in_specs=[pl.BlockSpec((tm, D_MODEL), row), pl.BlockSpec((1, mod_rows, D_MODEL), lambda i: (i // nblk, 0, 0)),
                  pl.BlockSpec((tm, V_A), row),
                  pl.BlockSpec((1, H_B, tm, D_B), lambda i: (i // nblk, 0, i % nblk, 0)),
                  pl.BlockSpec((1, H_C, tm, KV_LORA), lambda i: (i, 0, 0, 0)),
                  pl.BlockSpec((H_C, KV_LORA, V_A), lambda i: (0, 0, 0)),
                  pl.BlockSpec((D_MODEL, D_MODEL), lambda i: (0, 0))],
        out_specs=pl.BlockSpec((tm, D_MODEL), row),
        out_shape=jax.ShapeDtypeStruct((n_tok, D_MODEL), F32),
        compiler_params=_cparams(("arbitrary",)), name="mixout",
    )(x2, gate, ya, ofox, olat, wuv, wo)


def _normmod_kernel(*refs, route):
    if route:
        x_ref, sh_ref, sc_ref, g_ref, wr_ref, h_ref, ids_ref, gates_ref = refs
    else:
        x_ref, sh_ref, sc_ref, g_ref, h_ref = refs
    h = (_rms(x_ref[...]) * g_ref[...]) * (1.0 + sc_ref[0]) + sh_ref[0]
    _store_rows(h_ref, h)
    if route:
        logits = jnp.dot(h, wr_ref[...], precision=HIGHEST, preferred_element_type=F32)
        lane = lax.broadcasted_iota(jnp.int32, logits.shape, 1)
        logits = jnp.where(lane < N_EXPERTS, logits, NEG)
        m1 = jnp.max(logits, axis=-1, keepdims=True)
        i1 = jnp.min(jnp.where(logits == m1, lane, LANE), axis=-1, keepdims=True)
        rest = jnp.where(lane == i1, NEG, logits)
        m2 = jnp.max(rest, axis=-1, keepdims=True)
        i2 = jnp.min(jnp.where(rest == m2, lane, LANE), axis=-1, keepdims=True)
        e2 = jnp.exp(m2 - m1)
        g1 = 1.0 / (1.0 + e2)
        g2 = e2 / (1.0 + e2)
        ids_ref[...] = jnp.where(lane == 0, i1, i2)
        gates_ref[...] = jnp.where(lane == 0, g1, g2)


def _normmod(x2, shift, scale, g, router_pad, out, row0, n_seq, t_len, tm):
    nblk = t_len // tm
    n_tok = n_seq * t_len
    route = router_pad is not None
    off = row0 // tm

    def row(i):
        return (i, 0)

    def seq(i):
        return (i // nblk, 0, 0)

    mod_rows = shift.shape[1]
    in_specs = [pl.BlockSpec((tm, D_MODEL), row), pl.BlockSpec((1, mod_rows, D_MODEL), seq),
                pl.BlockSpec((1, mod_rows, D_MODEL), seq), pl.BlockSpec((1, D_MODEL), lambda i: (0, 0))]
    args = [x2, shift, scale, g]
    out_shape = [jax.ShapeDtypeStruct(out.shape, F32)]
    out_specs = [_block_rows(tm, out, lambda i: (off + i, 0))]
    if route:
        in_specs.append(pl.BlockSpec((D_MODEL, LANE), lambda i: (0, 0)))
        args.append(router_pad)
        out_shape += [jax.ShapeDtypeStruct((n_tok, LANE), jnp.int32), jax.ShapeDtypeStruct((n_tok, LANE), F32)]
        out_specs += [pl.BlockSpec((tm, LANE), row), pl.BlockSpec((tm, LANE), row)]
    in_specs.append(pl.BlockSpec(memory_space=pl.ANY))
    args.append(out)
    res = pl.pallas_call(
        functools.partial(_normmod_alias_kernel, route=route), grid=(n_tok // tm,),
        in_specs=in_specs, out_specs=tuple(out_specs), out_shape=tuple(out_shape),
        input_output_aliases={len(args) - 1: 0},
        compiler_params=_cparams(("arbitrary",)), name="normmod",
    )(*args)
    return res


def _normmod_alias_kernel(*refs, route):
    n_in = 5 if route else 4
    _normmod_kernel(*refs[:n_in], *refs[n_in + 1:], route=route)


GATHER_ROWS = 256


def _gather_kernel(idx_ref, src_ref, out_ref, sem):
    base = pl.program_id(0) * GATHER_ROWS

    def row_copy(r, src_row):
        return pltpu.make_async_copy(src_ref.at[src_row], out_ref.at[base + r], sem)

    def start(r, c):
        row_copy(r, idx_ref[0, 0, r]).start()
        return c

    def wait(r, c):
        row_copy(r, 0).wait()
        return c

    lax.fori_loop(0, GATHER_ROWS, start, 0)
    lax.fori_loop(0, GATHER_ROWS, wait, 0)


def _gather_rows(src, idx):
    m = idx.shape[0]
    return pl.pallas_call(
        _gather_kernel, grid=(m // GATHER_ROWS,),
        in_specs=[pl.BlockSpec((1, 1, GATHER_ROWS), lambda i: (i, 0, 0), memory_space=pltpu.SMEM),
                  pl.BlockSpec(memory_space=pl.ANY)],
        out_specs=pl.BlockSpec(memory_space=pl.ANY),
        out_shape=jax.ShapeDtypeStruct((m,) + src.shape[1:], src.dtype),
        scratch_shapes=[pltpu.SemaphoreType.DMA(())],
        compiler_params=pltpu.CompilerParams(dimension_semantics=("arbitrary",)), name="gather_rows",
    )(idx.reshape(m // GATHER_ROWS, 1, GATHER_ROWS), src)


def _swiglu_kernel(te_ref, nt_ref, x_ref, wg_ref, wu_ref, wd_ref, o_ref, acc_ref, *, precise):
    mm = functools.partial(_mm, precise=precise)
    i = pl.program_id(0)
    j = pl.program_id(1)

    @pl.when(i < nt_ref[0])
    def _():
        @pl.when(j == 0)
        def _():
            acc_ref[...] = jnp.zeros(acc_ref.shape, F32)

        x = _load_rows(x_ref) if precise else _load_rows(x_ref).astype(BF16)
        a = mm(x, wg_ref[0])
        u = mm(x, wu_ref[0])
        acc_ref[...] += mm(_silu(a) * u, wd_ref[0])

        @pl.when(j == pl.num_programs(1) - 1)
        def _():
            _store_rows(o_ref, acc_ref[...])

    @pl.when((i >= nt_ref[0]) & (j == pl.num_programs(1) - 1))
    def _():
        o_ref[...] = jnp.zeros(o_ref.shape, F32)


def _swiglu_grouped(x, tile_expert, n_tiles_used, wg, wu, wd, tm, tf, precise=False):
    m = x.shape[0]
    f = wg.shape[2]
    grid_spec = pltpu.PrefetchScalarGridSpec(
        num_scalar_prefetch=2, grid=(m // tm, f // tf),
        in_specs=[_block_rows(tm, x, lambda i, j, te, nt: (i, 0)),
                  pl.BlockSpec((1, D_MODEL, tf), lambda i, j, te, nt: (te[i], 0, j)),
                  pl.BlockSpec((1, D_MODEL, tf), lambda i, j, te, nt: (te[i], 0, j)),
                  pl.BlockSpec((1, tf, D_MODEL), lambda i, j, te, nt: (te[i], j, 0))],
        out_specs=_block_rows(tm, x, lambda i, j, te, nt: (i, 0)),
        scratch_shapes=[pltpu.VMEM((tm, D_MODEL), F32)])
    return pl.pallas_call(
        functools.partial(_swiglu_kernel, precise=precise), grid_spec=grid_spec,
        out_shape=jax.ShapeDtypeStruct(x.shape, F32),
        compiler_params=_cparams(("arbitrary", "arbitrary")), name="swiglu",
    )(tile_expert, n_tiles_used, x, wg, wu, wd)


def _combine_kernel(*refs, moe, final):
    refs = list(refs)
    x_ref = refs.pop(0)
    gate_ref = refs.pop(0)
    y1_ref = refs.pop(0)
    if moe:
        y2_ref = refs.pop(0)
        gates_ref = refs.pop(0)
    fg_ref = refs.pop(0) if final else None
    o_ref = refs.pop(0)
    y = _load_rows(y1_ref)
    if moe:
        gts = gates_ref[...]
        y = gts[:, 0:1] * y + gts[:, 1:2] * _load_rows(y2_ref)
    out = x_ref[...] + gate_ref[0] * y
    if final:
        out = _rms(out) * fg_ref[...]
    o_ref[...] = out


def _combine(x2, gate, y, y_row0, y2_row0, gates, final_g, n_seq, t_len, tm):
    nblk = t_len // tm
    n_tok = n_seq * t_len
    moe = gates is not None
    final = final_g is not None

    def row(i):
        return (i, 0)

    in_specs = [pl.BlockSpec((tm, D_MODEL), row), pl.BlockSpec((1, gate.shape[1], D_MODEL), lambda i: (i // nblk, 0, 0)),
                _block_rows(tm, y, lambda i: (y_row0 // tm + i, 0))]
    args = [x2, gate, y]
    if moe:
        in_specs += [_block_rows(tm, y, lambda i: (y2_row0 // tm + i, 0)), pl.BlockSpec((tm, LANE), row)]
        args += [y, gates]
    if final:
        in_specs.append(pl.BlockSpec((1, D_MODEL), lambda i: (0, 0)))
        args.append(final_g)
    return pl.pallas_call(
        functools.partial(_combine_kernel, moe=moe, final=final), grid=(n_tok // tm,),
        in_specs=in_specs, out_specs=pl.BlockSpec((tm, D_MODEL), row),
        out_shape=jax.ShapeDtypeStruct((n_tok, D_MODEL), F32),
        compiler_params=_cparams(("arbitrary",)), name="combine",
    )(*args)


def _pack_mixer_weights(w_in, w_gate2, b_gate, fox_b_f, qng, kvng, w_uq, w_uk, w_uv, gla_norm_g):
    offs = np.concatenate([[0], np.cumsum(IN_SPLITS)])
    cols = {n: (int(offs[i]), int(offs[i + 1])) for i, n in enumerate(
        ("gq", "gk", "gv", "ag", "ar", "fq", "fk", "fv", "bf", "cq", "ckv", "kr"))}

    def seg(n):
        return w_in[:, cols[n][0]:cols[n][1]]

    half = ROPE_C // 2
    kr = seg("kr")
    w = jnp.zeros((D_MODEL, N_PACK), F32)
    for n, c0 in (("gq", C_GQ), ("gk", C_GK), ("gv", C_GV), ("ag", C_AG), ("fq", C_FQ), ("fk", C_FK), ("fv", C_FV),
                  ("cq", C_CQ), ("ckv", C_CKV)):
        w = w.at[:, c0:c0 + cols[n][1] - cols[n][0]].set(seg(n))
    w = w.at[:, C_SM + SM_KR:C_SM + SM_KR + ROPE_C].set(kr)
    w = w.at[:, C_SM + SM_AR:C_SM + SM_AR + GATE_RANK].set(seg("ar"))
    w = w.at[:, C_SM + SM_BF:C_SM + SM_BF + H_B].set(seg("bf"))
    w = w.at[:, C_SM + SM_KRS:C_SM + SM_KRS + ROPE_C].set(jnp.concatenate([-kr[:, half:], kr[:, :half]], axis=1))
    wg2 = jnp.zeros((LANE, 2 * LANE), F32).at[SM_AR:SM_AR + GATE_RANK, :QK_A].set(w_gate2)
    bg = jnp.zeros((1, 2 * LANE), F32).at[0, :QK_A].set(b_gate)
    bf = jnp.zeros((1, LANE), F32).at[0, SM_BF:SM_BF + H_B].set(fox_b_f)
    uq = w_uq.reshape(Q_LORA, H_C, NOPE_C + ROPE_C)
    wuqn = uq[:, :, :NOPE_C].reshape(Q_LORA, H_C * NOPE_C)
    x1 = uq[:, :, NOPE_C:NOPE_C + half]
    x2 = uq[:, :, NOPE_C + half:]
    pad = jnp.zeros((Q_LORA, H_C, LANE - ROPE_C), F32)
    wr = jnp.concatenate([x1, x2, pad], axis=2).reshape(Q_LORA, H_C * LANE)
    wrs = jnp.concatenate([-x2, x1, pad], axis=2).reshape(Q_LORA, H_C * LANE)
    wuk = jnp.zeros((H_C * NOPE_C, H_C * KV_LORA), F32)
    wuv = jnp.zeros((H_C, KV_LORA, V_A), F32)
    for hh in range(H_C):
        wuk = wuk.at[hh * NOPE_C:(hh + 1) * NOPE_C, hh * KV_LORA:(hh + 1) * KV_LORA].set(w_uk[:, hh, :].T)
        wuv = wuv.at[hh, :, hh * V_C:(hh + 1) * V_C].set(w_uv[:, hh, :])
    full = dict(w_in=w, wg2=wg2, bg=bg, bf=bf, qng=qng.reshape(1, Q_LORA), kvng=kvng.reshape(1, KV_LORA), wuqn=wuqn,
                wuk=wuk, wr=wr, wrs=wrs, wuv=wuv, gnorm=jnp.tile(gla_norm_g, H_A).reshape(1, V_A))
    half_prec = dict(full)
    for n in ("w_in", "wg2", "wuqn", "wuk", "wr", "wrs", "wuv"):
        half_prec[n] = full[n].astype(BF16)
    return half_prec, full


def _rope_tables(pos):
    half = ROPE_C // 2
    inv_freq = ROPE_BASE ** (-jnp.arange(half, dtype=F32) / half)
    ang = pos.astype(F32)[:, None] * inv_freq[None, :]
    n = pos.shape[0]
    cos = jnp.concatenate([jnp.cos(ang), jnp.cos(ang), jnp.ones((n, LANE - ROPE_C), F32)], axis=1)
    sin = jnp.concatenate([jnp.sin(ang), jnp.sin(ang), jnp.zeros((n, LANE - ROPE_C), F32)], axis=1)
    return cos, sin


def _state_to_t(s):
    b = s.shape[0]
    out = jnp.zeros((b, V_A, QK_A), F32)
    for hh in range(H_A):
        out = out.at[:, hh * DV_A:(hh + 1) * DV_A, hh * DK_A:(hh + 1) * DK_A].set(jnp.swapaxes(s[:, hh], 1, 2))
    return out


def _state_from_t(s_t):
    return jnp.stack([jnp.swapaxes(s_t[:, hh * DV_A:(hh + 1) * DV_A, hh * DK_A:(hh + 1) * DK_A], 1, 2)
                      for hh in range(H_A)], axis=1)


def _round_up(a, b):
    return (a + b - 1) // b * b


def kernel(x_prompt, x_sample, c_prompt, c_sample, cache_fox_k, cache_fox_v, cache_fox_logf, cache_mla_ckv, cache_mla_krope, state_gla, ada_w, ada_b, norm_mix_g, norm_ffn_g, w_in, gla_w_gate2, gla_b_gate, gla_norm_g, fox_b_f, mla_q_norm_g, mla_kv_norm_g, mla_w_uq, mla_w_uk, mla_w_uv, w_out, ffn_w_gate, ffn_w_up, ffn_w_down, moe_router, moe_w_gate, moe_w_up, moe_w_down, final_norm_g):
    bp, tp, _ = x_prompt.shape
    bs, ts, _ = x_sample.shape
    past = cache_fox_k.shape[2]
    np_tok, ns_tok = bp * tp, bs * ts
    n_all = np_tok + ns_tok

    tm_p, tm_s = 256, ns_tok
    tq_fox, tk = 512, 512
    tk_pad = past + tk
    kv_len_s = past + ts

    nc = _round_up(bp + bs, 8)
    c_all = jnp.zeros((nc, D_MODEL), F32).at[:bp].set(c_prompt).at[bp:bp + bs].set(c_sample)
    mod = _ada(c_all, ada_w, ada_b)

    cos_p, sin_p = _rope_tables(jnp.arange(tp))
    cos_s, sin_s = _rope_tables(past + jnp.tile(jnp.arange(ts), bs))

    def seq_major(a):
        hh, d = a.shape[1], a.shape[3]
        return jnp.transpose(a.reshape(hh, bs, ts, d), (1, 0, 2, 3))

    def tok_major(a):
        hh, d = a.shape[1], a.shape[3]
        return jnp.transpose(a, (1, 0, 2, 3)).reshape(1, hh, bs * ts, d)

    cl = jnp.transpose(cache_fox_logf.astype(F32), (0, 1, 3, 2)).reshape(DEPTH * bs * H_B, past)
    f_cache, f_cache2 = [a.reshape(DEPTH, bs * H_B, past) for a in _cumsum(cl, jnp.zeros((cl.shape[0], 1), F32), 512)]

    xp = x_prompt.reshape(np_tok, D_MODEL)
    xs = x_sample.reshape(ns_tok, D_MODEL)
    p_states = [[] for _ in range(6)]
    s_states = [[] for _ in range(6)]

    for l in range(DEPTH):
        mods = [mod[l, :, i * D_MODEL:(i + 1) * D_MODEL] for i in range(6)]
        mp = [m[:bp].reshape(bp, 1, D_MODEL) for m in mods]
        ms = [jnp.repeat(m[bp:bp + bs], ts, axis=0).reshape(1, ns_tok, D_MODEL) for m in mods]
        pw, pw32 = _pack_mixer_weights(w_in[l], gla_w_gate2[l], gla_b_gate[l], fox_b_f[l], mla_q_norm_g[l],
                                       mla_kv_norm_g[l], mla_w_uq[l], mla_w_uk[l], mla_w_uv[l], gla_norm_g[l])
        wo = w_out[l].astype(BF16)
        g_mix = norm_mix_g[l].reshape(1, D_MODEL)
        g_ffn = norm_ffn_g[l].reshape(1, D_MODEL)

        (gq, gk, gv, ag, gla, fqh, fkh, fvh, fk, fv, ckv, kc, small, qs) = _inproj(
            xp, mp[0], mp[1], g_mix, pw, cos_p, sin_p, bp, tp, tm_p, False)
        logf = small[:, SM_BF:SM_BF + H_B]
        krope = small[:, SM_KR:SM_KR + ROPE_C]
        f_rows = jnp.transpose(logf.reshape(bp, tp, H_B), (0, 2, 1)).reshape(bp * H_B, tp)
        f_cum = _cumsum(f_rows, jnp.zeros((bp * H_B, 1), F32), 512)[1]
        ya, s_t = _gla(gq, gk, gv, gla, ag, pw["gnorm"], jnp.zeros((bp, V_A, QK_A), F32), bp, tp, 512, CHUNK, False)
        g_fox = bp * H_B
        o_fox = _flash(fqh.reshape(g_fox, tp // tq_fox, tq_fox, D_B), fkh.reshape(g_fox, tp, D_B),
                       fvh.reshape(g_fox, tp, 2 * D_B), f_cum.reshape(g_fox, tp // tq_fox, tq_fox, 1),
                       f_cum.reshape(g_fox, 1, tp), tq=tq_fox, rep=1, tk=tk, q0=0, mode="causal", kv_len=tp, dv=D_B)
        o_mla = _flash(qs.reshape(bp, tp // tm_p, H_C * tm_p, 2 * LANE), kc.reshape(bp, tp, 2 * LANE), None, None, None,
                       tq=tm_p, rep=H_C, tk=tk, q0=0, mode="chunk", kv_len=tp, dv=KV_LORA)
        xp = _mixout(xp, mp[2], ya, o_fox.reshape(bp, H_B, tp, D_B),
                     o_mla.reshape(np_tok // tm_p, H_C, tm_p, KV_LORA), pw["wuv"], wo, bp, tp, tm_p, False)
        for i, st in enumerate((_state_from_t(s_t), fk.reshape(bp, tp, H_B, D_B), fv.reshape(bp, tp, H_B, D_B),
                                logf.reshape(bp, tp, H_B), ckv.reshape(bp, tp, KV_LORA), krope.reshape(bp, tp, ROPE_C))):
            p_states[i].append(st)

        (gq, gk, gv, ag, gla, fqh, fkh, fvh, fk, fv, ckv, kc, small, qs) = _inproj(
            xs, ms[0], ms[1], g_mix, pw32, cos_s, sin_s, 1, ns_tok, tm_s, True)
        logf = small[:, SM_BF:SM_BF + H_B]
        krope = small[:, SM_KR:SM_KR + ROPE_C]
        g_fox = bs * H_B
        f_rows = jnp.transpose(logf.reshape(bs, ts, H_B), (0, 2, 1)).reshape(g_fox, ts)
        f_new = _cumsum(f_rows, f_cache[l][:, past - 1:past], ts)[1]
        f_keys = jnp.concatenate([f_cache2[l], f_new, jnp.zeros((g_fox, tk_pad - kv_len_s), F32)], axis=1)
        ya, s_t = _gla(gq, gk, gv, gla, ag, pw["gnorm"], _state_to_t(state_gla[l].astype(F32)), bs, ts, ts, ts, True)

        def keys(cache, new):
            w = new.shape[3]
            c = jnp.transpose(cache.astype(F32), (0, 2, 1, 3))
            if w > D_B:
                c = jnp.concatenate([c, jnp.ones((bs, H_B, past, w - D_B), F32)], axis=3)
            full = jnp.concatenate([c, new, jnp.zeros((bs, H_B, tk_pad - kv_len_s, w), F32)], axis=2)
            return full.reshape(g_fox, tk_pad, w)

        o_fox = _flash(seq_major(fqh).reshape(g_fox, 1, ts, D_B), keys(cache_fox_k[l], seq_major(fkh)),
                       keys(cache_fox_v[l], seq_major(fvh)),
                       f_new.reshape(g_fox, 1, ts, 1), f_keys.reshape(g_fox, 1, tk_pad),
                       tq=ts, rep=1, tk=tk, q0=past, mode="causal", kv_len=kv_len_s, dv=D_B, precise=True)
        kc_cache = jnp.concatenate([cache_mla_ckv[l].astype(F32), cache_mla_krope[l].astype(F32),
                                    jnp.zeros((bs, past, LANE - ROPE_C), F32)], axis=2)
        kc_all = jnp.concatenate([kc_cache, kc.reshape(bs, ts, 2 * LANE),
                                  jnp.zeros((bs, tk_pad - kv_len_s, 2 * LANE), F32)], axis=1)
        o_mla = _flash(seq_major(qs).reshape(bs, 1, H_C * ts, 2 * LANE), kc_all, None, None, None,
                       tq=ts, rep=H_C, tk=tk, q0=past, mode="chunk", kv_len=kv_len_s, dv=KV_LORA, precise=True)
        xs = _mixout(xs, ms[2], ya, tok_major(o_fox.reshape(bs, H_B, ts, D_B)),
                     tok_major(o_mla.reshape(bs, H_C, ts, KV_LORA)), pw32["wuv"], w_out[l], 1, ns_tok, tm_s, True)
        for i, st in enumerate((_state_from_t(s_t), fk.reshape(bs, ts, H_B, D_B), fv.reshape(bs, ts, H_B, D_B),
                                logf.reshape(bs, ts, H_B), ckv.reshape(bs, ts, KV_LORA), krope.reshape(bs, ts, ROPE_C))):
            s_states[i].append(st)

        last = l == DEPTH - 1
        fg = final_norm_g.reshape(1, D_MODEL) if last else None
        tm_f = 512
        if l % 2 == 0:
            j = l // 2
            h_p = _normmod(xp, mp[3], mp[4], g_ffn, None, jnp.zeros((np_tok, D_MODEL), F32), 0, bp, tp, tm_p)[0]
            h_s = _normmod(xs, ms[3], ms[4], g_ffn, None, jnp.zeros((ns_tok, D_MODEL), F32), 0, 1, ns_tok, tm_s)[0]
            n_tiles = np_tok // tm_f
            y_p = _swiglu_grouped(h_p, jnp.zeros((n_tiles,), jnp.int32), jnp.full((1,), n_tiles, jnp.int32),
                                  ffn_w_gate[j:j + 1].astype(BF16), ffn_w_up[j:j + 1].astype(BF16),
                                  ffn_w_down[j:j + 1].astype(BF16), tm_f, 1408)
            y_s = _swiglu_grouped(h_s, jnp.zeros((1,), jnp.int32), jnp.ones((1,), jnp.int32),
                                  ffn_w_gate[j:j + 1], ffn_w_up[j:j + 1], ffn_w_down[j:j + 1], tm_s, 1408, precise=True)
            xp = _combine(xp, mp[5], y_p, 0, 0, None, fg, bp, tp, tm_p)
            xs = _combine(xs, ms[5], y_s, 0, 0, None, fg, 1, ns_tok, tm_s)
        else:
            j = l // 2
            router_pad = jnp.zeros((D_MODEL, LANE), F32).at[:, :N_EXPERTS].set(moe_router[j])
            h_all = jnp.zeros((n_all,) + TOK_TILE, F32)
            h_all, ids_p, gates_p = _normmod(xp, mp[3], mp[4], g_ffn, router_pad, h_all, 0, bp, tp, tm_p)
            h_all, ids_s, gates_s = _normmod(xs, ms[3], ms[4], g_ffn, router_pad, h_all, np_tok, 1, ns_tok, tm_s)
            ids = jnp.concatenate([ids_p[:, :2], ids_s[:, :2]], axis=0)
            e = jnp.transpose(ids).reshape(-1)
            onehot = (e[:, None] == jnp.arange(N_EXPERTS)[None, :]).astype(jnp.int32)
            rank = jnp.sum((jnp.cumsum(onehot, axis=0) - onehot) * onehot, axis=1)
            counts = jnp.sum(onehot, axis=0)
            padded = (counts + tm_f - 1) // tm_f * tm_f
            ends = jnp.cumsum(padded)
            starts = ends - padded
            pos = starts[e] + rank
            m_pad = _round_up(2 * n_all + N_EXPERTS * (tm_f - 1), tm_f)
            token = jnp.tile(jnp.arange(n_all, dtype=jnp.int32), 2)
            src = jnp.zeros((m_pad,), jnp.int32).at[pos].set(token)
            n_tiles = m_pad // tm_f
            tile_row0 = jnp.arange(n_tiles, dtype=jnp.int32) * tm_f
            tile_expert = jnp.minimum(jnp.sum((ends[None, :] <= tile_row0[:, None]).astype(jnp.int32), axis=1),
                                      N_EXPERTS - 1)
            n_used = (ends[-1] // tm_f).astype(jnp.int32).reshape(1)
            x_sorted = _gather_rows(h_all, src)
            y = _swiglu_grouped(x_sorted, tile_expert, n_used, moe_w_gate[j].astype(BF16), moe_w_up[j].astype(BF16),
                                moe_w_down[j].astype(BF16), tm_f, 512)
            n_back = _round_up(n_all, GATHER_ROWS)
            back = jnp.zeros((2 * n_back,), jnp.int32).at[:n_all].set(pos[:n_all]).at[n_back:n_back + n_all].set(pos[n_all:])
            yg = _gather_rows(y, back)
            xp = _combine(xp, mp[5], yg, 0, n_back, gates_p, fg, bp, tp, tm_p)
            xs = _combine(xs, ms[5], yg, np_tok, n_back + np_tok, gates_s, fg, 1, ns_tok, tm_s)

    outs_p = [jnp.stack(s, axis=0) for s in p_states]
    outs_s = [jnp.stack(s, axis=0) for s in s_states]
    return (xp.reshape(bp, tp, D_MODEL), xs.reshape(bs, ts, D_MODEL), *outs_p, *outs_s)
```

```python
import functools

import numpy as np
import jax
import jax.numpy as jnp
from jax import lax
from jax.experimental import pallas as pl
from jax.experimental.pallas import tpu as pltpu

F32 = jnp.float32
BF16 = jnp.bfloat16
HIGHEST = lax.Precision.HIGHEST

D_MODEL = 1024
DEPTH = 2
CHUNK = 64
EPS = 1e-6
H_A, DK_A, DV_A = 6, 32, 64
GATE_RANK = 16
GATE_TAU = 16.0
H_B, D_B = 4, 64
H_C, NOPE_C, ROPE_C, V_C = 6, 64, 32, 64
Q_LORA, KV_LORA = 256, 128
ROPE_BASE = 10000.0
N_EXPERTS = 8
IN_SPLITS = (H_A * DK_A, H_A * DK_A, H_A * DV_A, H_A * DV_A, GATE_RANK,
             H_B * D_B, H_B * D_B, H_B * D_B, H_B, Q_LORA, KV_LORA, ROPE_C)

QK_A = H_A * DK_A
V_A = H_A * DV_A
QKV_B = H_B * D_B
LANE = 128
NEG = -1e30

C_GQ, C_GK, C_GV, C_AG = 0, 256, 512, 896
C_FQ, C_FK, C_FV = 1280, 1536, 1792
C_CQ, C_CKV, C_SM = 2048, 2304, 2432
N_PACK = 2560
SM_KR, SM_AR, SM_BF, SM_KRS = 0, 32, 48, 64

GLA_SUB = 16
FLASH_ROW_GROUP = 256
LOG2E = float(np.log2(np.e))
VMEM_LIMIT = 56 * 1024 * 1024


def _cparams(sem):
    return pltpu.CompilerParams(dimension_semantics=sem, vmem_limit_bytes=VMEM_LIMIT)


def _log_sigmoid(z):
    return jnp.minimum(z, 0.0) - jnp.log1p(jnp.exp(-jnp.abs(z)))


def _silu(z):
    return z * (1.0 / (1.0 + jnp.exp(-z)))


def _rms(x):
    return x * lax.rsqrt(jnp.mean(x * x, axis=-1, keepdims=True) + EPS)


def _dot(a, b):
    return jnp.dot(a, b, preferred_element_type=F32)


def _dot_nt(a, b):
    return lax.dot_general(a, b, (((1,), (1,)), ((), ())), preferred_element_type=F32)


def _dot_tn(a, b):
    return lax.dot_general(a, b, (((0,), (0,)), ((), ())), preferred_element_type=F32)


def _mm(a, b, dims=(((1,), (0,)), ((), ())), *, precise):
    if precise:
        return lax.dot_general(a.astype(F32), b.astype(F32), dims, precision=HIGHEST, preferred_element_type=F32)
    return lax.dot_general(a.astype(BF16), b.astype(BF16), dims, preferred_element_type=F32)


_NT = (((1,), (1,)), ((), ()))
_TN = (((0,), (0,)), ((), ()))

SUBLANE = 8
TOK_TILE = (D_MODEL // LANE, LANE)
assert TOK_TILE[0] == SUBLANE


def _block_rows(tm, arr, index_map):
    if arr.ndim == 3:
        return pl.BlockSpec((tm,) + TOK_TILE, lambda *a: index_map(*a) + (0,))
    return pl.BlockSpec((tm, D_MODEL), index_map)


def _load_rows(ref):
    if len(ref.shape) == 3:
        return jnp.concatenate([ref[:, s, :] for s in range(TOK_TILE[0])], axis=-1)
    return ref[...]


def _store_rows(ref, val):
    if len(ref.shape) == 3:
        for s in range(TOK_TILE[0]):
            ref[:, s, :] = val[:, s * LANE:(s + 1) * LANE]
    else:
        ref[...] = val


def _ada_kernel(c_ref, w_ref, b_ref, o_ref):
    s = _silu(c_ref[...])
    o_ref[0] = jnp.dot(s, w_ref[0], precision=HIGHEST, preferred_element_type=F32) + b_ref[0]


def _ada(c_all, ada_w, ada_b):
    nc = c_all.shape[0]
    tn = 1536
    return pl.pallas_call(
        _ada_kernel,
        grid=(DEPTH, 6 * D_MODEL // tn),
        in_specs=[pl.BlockSpec((nc, D_MODEL), lambda l, j: (0, 0)),
                  pl.BlockSpec((1, D_MODEL, tn), lambda l, j: (l, 0, j)),
                  pl.BlockSpec((1, 1, tn), lambda l, j: (l, 0, j))],
        out_specs=pl.BlockSpec((1, nc, tn), lambda l, j: (l, 0, j)),
        out_shape=jax.ShapeDtypeStruct((DEPTH, nc, 6 * D_MODEL), F32),
        compiler_params=_cparams(("arbitrary", "arbitrary")),
        name="ada",
    )(c_all, ada_w, ada_b.reshape(DEPTH, 1, 6 * D_MODEL))


def _inproj_kernel(x_ref, sh_ref, sc_ref, g_ref, w_ref, wg2_ref, bg_ref, bf_ref, qng_ref, kvng_ref,
                   wuqn_ref, wuk_ref, wr_ref, wrs_ref, cos_ref, sin_ref,
                   gq_ref, gk_ref, gv_ref, ag_ref, gla_ref, fqh_ref, fkh_ref, fvh_ref, fk_ref, fv_ref,
                   ckv_ref, kc_ref, small_ref, qs_ref, *, precise):
    mm = functools.partial(_mm, precise=precise)
    act = kc_ref.dtype
    x = x_ref[...]
    h = (_rms(x) * g_ref[...]) * (1.0 + sc_ref[0]) + sh_ref[0]
    p = mm(h, w_ref[...])

    gq_ref[...] = p[:, C_GQ:C_GQ + QK_A] * (DK_A ** -0.5)
    gk_ref[...] = p[:, C_GK:C_GK + QK_A]
    gv_ref[...] = p[:, C_GV:C_GV + V_A]
    ag_ref[...] = p[:, C_AG:C_AG + V_A]
    sm = p[:, C_SM:C_SM + LANE]
    z = mm(sm, wg2_ref[...]) + bg_ref[...]
    gla_ref[...] = _log_sigmoid(z[:, :QK_A]) * (1.0 / GATE_TAU)

    fq = p[:, C_FQ:C_FQ + QKV_B] * (D_B ** -0.5 * LOG2E)
    fk = p[:, C_FK:C_FK + QKV_B]
    fv = p[:, C_FV:C_FV + QKV_B]
    fk_ref[...] = fk
    fv_ref[...] = fv
    for hh in range(H_B):
        sl = slice(hh * D_B, (hh + 1) * D_B)
        fqh_ref[0, hh] = fq[:, sl].astype(act)
        fkh_ref[0, hh] = fk[:, sl].astype(act)
        fvh_ref[0, hh, :, :D_B] = fv[:, sl].astype(act)
        fvh_ref[0, hh, :, D_B:] = jnp.ones((fv.shape[0], D_B), act)

    cos = cos_ref[...]
    sin = sin_ref[...]
    lane = lax.broadcasted_iota(jnp.int32, sm.shape, 1)
    kr = sm * cos + pltpu.roll(sm, LANE - SM_KRS, 1) * sin
    logf = _log_sigmoid(sm + bf_ref[...])
    small_ref[...] = jnp.where((lane >= SM_BF) & (lane < SM_BF + H_B), logf, kr)

    ckv = _rms(p[:, C_CKV:C_CKV + KV_LORA]) * kvng_ref[...]
    ckv_ref[...] = ckv
    kc_ref[:, :KV_LORA] = ckv.astype(act)
    kc_ref[:, KV_LORA:] = jnp.where(lane < ROPE_C, kr, 0.0).astype(act)
    cqn = _rms(p[:, C_CQ:C_CQ + Q_LORA]) * qng_ref[...]
    nope = mm(cqn, wuqn_ref[...])
    qlat = mm(nope, wuk_ref[...])
    qa = mm(cqn, wr_ref[...])
    qb = mm(cqn, wrs_ref[...])
    scale = (NOPE_C + ROPE_C) ** -0.5 * LOG2E
    for hh in range(H_C):
        sl = slice(hh * LANE, (hh + 1) * LANE)
        qs_ref[0, hh, :, :KV_LORA] = (qlat[:, sl] * scale).astype(act)
        qs_ref[0, hh, :, KV_LORA:] = ((qa[:, sl] * cos + qb[:, sl] * sin) * scale).astype(act)


def _inproj(x2, shift, scale, g, pw, cos_tab, sin_tab, n_seq, t_len, tm, precise):
    n_tok = n_seq * t_len
    nblk = t_len // tm
    act = F32 if precise else BF16
    mod_rows = shift.shape[1]

    def row(i):
        return (i, 0)

    def seq(i):
        return (i // nblk, 0, 0)

    def const2(i):
        return (0, 0)

    def tab(i):
        return (i % nblk, 0)

    def headmajor(i):
        return (i // nblk, 0, i % nblk, 0)

    sds = jax.ShapeDtypeStruct
    out_shape = (
        sds((n_tok, QK_A), F32), sds((n_tok, QK_A), F32), sds((n_tok, V_A), F32), sds((n_tok, V_A), F32),
        sds((n_tok, QK_A), F32),
        sds((n_seq, H_B, t_len, D_B), act), sds((n_seq, H_B, t_len, D_B), act), sds((n_seq, H_B, t_len, 2 * D_B), act),
        sds((n_tok, QKV_B), F32), sds((n_tok, QKV_B), F32),
        sds((n_tok, KV_LORA), F32), sds((n_tok, 2 * LANE), act), sds((n_tok, LANE), F32),
        sds((n_tok // tm, H_C, tm, 2 * LANE), act),
    )
    out_specs = (
        pl.BlockSpec((tm, QK_A), row), pl.BlockSpec((tm, QK_A), row), pl.BlockSpec((tm, V_A), row),
        pl.BlockSpec((tm, V_A), row), pl.BlockSpec((tm, QK_A), row),
        pl.BlockSpec((1, H_B, tm, D_B), headmajor), pl.BlockSpec((1, H_B, tm, D_B), headmajor),
        pl.BlockSpec((1, H_B, tm, 2 * D_B), headmajor),
        pl.BlockSpec((tm, QKV_B), row), pl.BlockSpec((tm, QKV_B), row),
        pl.BlockSpec((tm, KV_LORA), row), pl.BlockSpec((tm, 2 * LANE), row), pl.BlockSpec((tm, LANE), row),
        pl.BlockSpec((1, H_C, tm, 2 * LANE), lambda i: (i, 0, 0, 0)),
    )
    in_specs = [
        pl.BlockSpec((tm, D_MODEL), row), pl.BlockSpec((1, mod_rows, D_MODEL), seq),
        pl.BlockSpec((1, mod_rows, D_MODEL), seq), pl.BlockSpec((1, D_MODEL), const2),
        pl.BlockSpec((D_MODEL, N_PACK), const2), pl.BlockSpec((LANE, 2 * LANE), const2),
        pl.BlockSpec((1, 2 * LANE), const2), pl.BlockSpec((1, LANE), const2),
        pl.BlockSpec((1, Q_LORA), const2), pl.BlockSpec((1, KV_LORA), const2),
        pl.BlockSpec((Q_LORA, H_C * NOPE_C), const2), pl.BlockSpec((H_C * NOPE_C, H_C * KV_LORA), const2),
        pl.BlockSpec((Q_LORA, H_C * LANE), const2), pl.BlockSpec((Q_LORA, H_C * LANE), const2),
        pl.BlockSpec((tm, LANE), tab), pl.BlockSpec((tm, LANE), tab),
    ]
    return pl.pallas_call(
        functools.partial(_inproj_kernel, precise=precise), grid=(n_tok // tm,), in_specs=in_specs,
        out_specs=out_specs, out_shape=out_shape, compiler_params=_cparams(("arbitrary",)), name="inproj",
    )(x2, shift, scale, g, pw["w_in"], pw["wg2"], pw["bg"], pw["bf"], pw["qng"], pw["kvng"],
      pw["wuqn"], pw["wuk"], pw["wr"], pw["wrs"], cos_tab, sin_tab)


def _cumsum_kernel(x_ref, init_ref, o_ref, o2_ref, *, tb):
    n = x_ref.shape[1]
    upper = (lax.broadcasted_iota(jnp.int32, (tb, tb), 0) <= lax.broadcasted_iota(jnp.int32, (tb, tb), 1)).astype(F32)
    carry = init_ref[...]
    for j in range(n // tb):
        blk = jnp.dot(x_ref[:, j * tb:(j + 1) * tb], upper, precision=HIGHEST, preferred_element_type=F32) + carry
        o_ref[:, j * tb:(j + 1) * tb] = blk
        o2_ref[:, j * tb:(j + 1) * tb] = blk * LOG2E
        carry = blk[:, tb - 1:tb]


def _cumsum(x, init, tb):
    return pl.pallas_call(
        functools.partial(_cumsum_kernel, tb=tb),
        out_shape=(jax.ShapeDtypeStruct(x.shape, F32), jax.ShapeDtypeStruct(x.shape, F32)), name="cumsum",
    )(x, init)


def _gla_kernel(q_ref, k_ref, v_ref, la_ref, ag_ref, gn_ref, s0_ref, y_ref, sout_ref, s_ref, *, chunk, n_chunks,
                precise):
    mm = functools.partial(_mm, precise=precise)
    c = chunk
    nsub = c // GLA_SUB

    @pl.when(pl.program_id(1) == 0)
    def _():
        s_ref[...] = s0_ref[0]

    lane_qk = lax.broadcasted_iota(jnp.int32, (GLA_SUB, QK_A), 1) // DK_A
    lane_v = lax.broadcasted_iota(jnp.int32, (GLA_SUB, V_A), 1) // DV_A
    bd = (lax.broadcasted_iota(jnp.int32, (V_A, QK_A), 0) // DV_A) == (lax.broadcasted_iota(jnp.int32, (V_A, QK_A), 1) // DK_A)
    tril = (lax.broadcasted_iota(jnp.int32, (c, c), 0) >= lax.broadcasted_iota(jnp.int32, (c, c), 1)).astype(F32)
    hm = (lax.broadcasted_iota(jnp.int32, (V_A, V_A), 0) // DV_A) == (lax.broadcasted_iota(jnp.int32, (V_A, V_A), 1) // DV_A)
    head_mean = jnp.where(hm, 1.0 / DV_A, 0.0).astype(F32)

    def body(ci, carry):
        r = pl.multiple_of(ci * c, c)
        q = q_ref[pl.ds(r, c), :]
        k = k_ref[pl.ds(r, c), :]
        v = v_ref[pl.ds(r, c), :]
        la = la_ref[pl.ds(r, c), :]
        b = jnp.dot(tril, la, precision=HIGHEST, preferred_element_type=F32)
        s_t = s_ref[...]
        vb = v if precise else v.astype(BF16)
        o_inter = mm(q * jnp.exp(b), s_t, _NT)
        outs = []
        for i in range(nsub):
            r0 = i * GLA_SUB
            r1 = r0 + GLA_SUB
            bi = b[r0 - 1:r0] if i > 0 else jnp.zeros((1, QK_A), F32)
            qi = q[r0:r1] * jnp.exp(b[r0:r1] - bi)
            kk = k[:r1] * jnp.exp(bi - b[:r1])
            qst = jnp.concatenate([jnp.where(lane_qk == hh, qi, 0.0) for hh in range(H_A)], axis=0)
            att = mm(qst, kk, _NT)
            t_idx = r0 + lax.broadcasted_iota(jnp.int32, att.shape, 0) % GLA_SUB
            s_idx = lax.broadcasted_iota(jnp.int32, att.shape, 1)
            att = jnp.where(s_idx <= t_idx, att, 0.0)
            oi = mm(att, vb[:r1])
            o = jnp.zeros((GLA_SUB, V_A), F32)
            for hh in range(H_A):
                o = o + jnp.where(lane_v == hh, oi[hh * GLA_SUB:(hh + 1) * GLA_SUB], 0.0)
            outs.append(o)
        o = jnp.concatenate(outs, axis=0) + o_inter if nsub > 1 else outs[0] + o_inter
        b_last = b[c - 1:c]
        kd = k * jnp.exp(b_last - b)
        s_ref[...] = s_t * jnp.exp(b_last) + jnp.where(bd, mm(vb, kd, _TN), 0.0)
        ms = jnp.dot(o * o, head_mean, precision=HIGHEST, preferred_element_type=F32)
        y = o * lax.rsqrt(ms + EPS) * gn_ref[...] * _silu(ag_ref[pl.ds(r, c), :])
        y_ref[pl.ds(r, c), :] = y.astype(y_ref.dtype)
        return carry

    lax.fori_loop(0, n_chunks, body, 0)

    @pl.when(pl.program_id(1) == pl.num_programs(1) - 1)
    def _():
        sout_ref[0] = s_ref[...]


def _gla(gq, gk, gv, gla, ag, gnorm, s0_t, n_seq, t_len, tb, chunk, precise):
    nblk = t_len // tb

    def row(b, j):
        return (b * nblk + j, 0)

    def st(b, j):
        return (b, 0, 0)

    n_tok = n_seq * t_len
    return pl.pallas_call(
        functools.partial(_gla_kernel, chunk=chunk, n_chunks=tb // chunk, precise=precise),
        grid=(n_seq, nblk),
        in_specs=[pl.BlockSpec((tb, QK_A), row), pl.BlockSpec((tb, QK_A), row), pl.BlockSpec((tb, V_A), row),
                  pl.BlockSpec((tb, QK_A), row), pl.BlockSpec((tb, V_A), row),
                  pl.BlockSpec((1, V_A), lambda b, j: (0, 0)), pl.BlockSpec((1, V_A, QK_A), st)],
        out_specs=(pl.BlockSpec((tb, V_A), row), pl.BlockSpec((1, V_A, QK_A), st)),
        out_shape=(jax.ShapeDtypeStruct((n_tok, V_A), F32 if precise else BF16),
                   jax.ShapeDtypeStruct((n_seq, V_A, QK_A), F32)),
        scratch_shapes=[pltpu.VMEM((V_A, QK_A), F32)],
        compiler_params=_cparams(("arbitrary", "arbitrary")), name="gla",
    )(gq, gk, gv, gla, ag, gnorm, s0_t)


def _flash_kernel(*refs, tq, rep, tk, q0, mode, kv_len, dv, bias, v_from_k, precise):
    mm = functools.partial(_mm, precise=precise)
    refs = list(refs)
    q_ref = refs.pop(0)
    k_ref = refs.pop(0)
    v_ref = k_ref if v_from_k else refs.pop(0)
    fq_ref = refs.pop(0) if bias else None
    fk_ref = refs.pop(0) if bias else None
    o_ref, m_ref, acc_ref = refs
    i = pl.program_id(1)
    rows = rep * tq
    m_ref[...] = jnp.full(m_ref.shape, NEG, F32)
    acc_ref[...] = jnp.zeros(acc_ref.shape, F32)

    first_q = q0 + i * tq
    last_q = first_q + tq - 1
    if mode == "chunk":
        vis_all = (first_q // CHUNK) * CHUNK + CHUNK - 1
        vis_any = (last_q // CHUNK) * CHUNK + CHUNK - 1
    else:
        vis_all = first_q
        vis_any = last_q
    vis_all = jnp.minimum(vis_all, kv_len - 1)
    vis_any = jnp.minimum(vis_any, kv_len - 1)
    n_full = (vis_all + 1) // tk
    n_any = vis_any // tk + 1

    grp = min(rows, FLASH_ROW_GROUP)

    def step(jb, masked):
        k_start = pl.multiple_of(jb * tk, tk)
        k = k_ref[0, pl.ds(k_start, tk), :]
        if v_from_k:
            v = jnp.where(lax.broadcasted_iota(jnp.int32, k.shape, 1) < dv, k, jnp.ones_like(k))
        else:
            v = v_ref[0, pl.ds(k_start, tk), :]
        for g in range(rows // grp):
            rs = slice(g * grp, (g + 1) * grp)
            s = mm(q_ref[0, 0, rs, :], k, _NT)
            if bias:
                s = s + fq_ref[0, 0, rs, :] - fk_ref[0, jb]
            if masked:
                qpos = first_q + (g * grp + lax.broadcasted_iota(jnp.int32, (grp, tk), 0)) % tq
                kpos = k_start + lax.broadcasted_iota(jnp.int32, (grp, tk), 1)
                if mode == "chunk":
                    ok = (kpos // CHUNK) <= (qpos // CHUNK)
                else:
                    ok = kpos <= qpos
                ok = ok & (kpos < kv_len)
                s = jnp.where(ok, s, NEG)
            chunks = [s[:, c * LANE:(c + 1) * LANE] for c in range(tk // LANE)]
            smax = chunks[0]
            for ch in chunks[1:]:
                smax = jnp.maximum(smax, ch)
            m_prev = m_ref[rs, :]
            m_new = jnp.maximum(m_prev, jnp.max(smax, axis=-1, keepdims=True))
            alpha = jnp.exp2(m_prev - m_new)
            p = jnp.concatenate([jnp.exp2(ch - m_new) for ch in chunks], axis=1)
            acc = acc_ref[rs, :]
            alpha_w = alpha if acc.shape[1] == LANE else jnp.concatenate([alpha] * (acc.shape[1] // LANE), axis=1)
            acc_ref[rs, :] = alpha_w * acc + mm(p, v)
            m_ref[rs, :] = m_new

    def loop(lo, hi, masked):
        def body(jb, carry):
            step(jb, masked)
            return carry
        lax.fori_loop(lo, hi, body, 0)

    loop(0, n_full, False)
    loop(n_full, n_any, True)

    acc = acc_ref[...]
    if dv == LANE:
        out = acc[:, :dv] / acc[:, dv:]
    else:
        out = (acc / pltpu.roll(acc, LANE - dv, 1))[:, :dv]
    o_ref[0, 0] = out.astype(o_ref.dtype)


def _flash(q, k, v, fq, fk, *, tq, rep, tk, q0, mode, kv_len, dv, precise=False):
    g, nq, rows, dqk = q.shape
    t_k = k.shape[1]
    nk = t_k // tk
    bias = fq is not None
    v_from_k = v is None

    def qmap(b, i):
        return (b, i, 0, 0)

    def kmap(b, i):
        return (b, 0, 0)

    in_specs = [pl.BlockSpec((1, 1, rows, dqk), qmap), pl.BlockSpec((1, t_k, dqk), kmap)]
    args = [q, k]
    dva = dqk if v_from_k else v.shape[2]
    assert dva % LANE == 0 and dva > dv
    if not v_from_k:
        in_specs.append(pl.BlockSpec((1, t_k, dva), kmap))
        args.append(v)
    if bias:
        in_specs += [pl.BlockSpec((1, 1, rows, 1), qmap), pl.BlockSpec((1, nk, 1, tk), lambda b, i: (b, 0, 0, 0))]
        args += [fq, fk.reshape(g, nk, 1, tk)]
    return pl.pallas_call(
        functools.partial(_flash_kernel, tq=tq, rep=rep, tk=tk, q0=q0, mode=mode, kv_len=kv_len, dv=dv,
                          bias=bias, v_from_k=v_from_k, precise=precise),
        grid=(g, nq), in_specs=in_specs,
        out_specs=pl.BlockSpec((1, 1, rows, dv), qmap),
        out_shape=jax.ShapeDtypeStruct((g, nq, rows, dv), F32 if precise else BF16),
        scratch_shapes=[pltpu.VMEM((rows, LANE), F32), pltpu.VMEM((rows, dva), F32)],
        compiler_params=_cparams(("arbitrary", "arbitrary")), name="flash_" + mode,
    )(*args)


def _mixout_kernel(x_ref, gate_ref, ya_ref, of_ref, ol_ref, wuv_ref, wo_ref, o_ref, *, precise):
    mm = functools.partial(_mm, precise=precise)
    acc = mm(ya_ref[...], wo_ref[:V_A, :])
    for hh in range(H_B):
        r0 = V_A + hh * D_B
        acc = acc + mm(of_ref[0, hh], wo_ref[r0:r0 + D_B, :])
    yc = mm(ol_ref[0, 0], wuv_ref[0])
    for hh in range(1, H_C):
        yc = yc + mm(ol_ref[0, hh], wuv_ref[hh])
    acc = acc + mm(yc, wo_ref[V_A + QKV_B:, :])
    o_ref[...] = x_ref[...] + gate_ref[0] * acc


def _mixout(x2, gate, ya, ofox, olat, wuv, wo, n_seq, t_len, tm, precise):
    nblk = t_len // tm
    n_tok = n_seq * t_len
    mod_rows = gate.shape[1]

    def row(i):
        return (i, 0)

    return pl.pallas_call(
        functools.partial(_mixout_kernel, precise=precise), grid=(n_tok // tm,),
        in_specs=[pl.BlockSpec((tm, D_MODEL), row), pl.BlockSpec((1, mod_rows, D_MODEL), lambda i: (i // nblk, 0, 0)),
                  pl.BlockSpec((tm, V_A), row),
                  pl.BlockSpec((1, H_B, tm, D_B), lambda i: (i // nblk, 0, i % nblk, 0)),
                  pl.BlockSpec((1, H_C, tm, KV_LORA), lambda i: (i, 0, 0, 0)),
                  pl.BlockSpec((H_C, KV_LORA, V_A), lambda i: (0, 0, 0)),
                  pl.BlockSpec((D_MODEL, D_MODEL), lambda i: (0, 0))],
        out_specs=pl.BlockSpec((tm, D_MODEL), row),
        out_shape=jax.ShapeDtypeStruct((n_tok, D_MODEL), F32),
        compiler_params=_cparams(("arbitrary",)), name="mixout",
    )(x2, gate, ya, ofox, olat, wuv, wo)


def _normmod_kernel(*refs, route):
    if route:
        x_ref, sh_ref, sc_ref, g_ref, wr_ref, h_ref, ids_ref, gates_ref = refs
    else:
        x_ref, sh_ref, sc_ref, g_ref, h_ref = refs
    h = (_rms(x_ref[...]) * g_ref[...]) * (1.0 + sc_ref[0]) + sh_ref[0]
    _store_rows(h_ref, h)
    if route:
        logits = jnp.dot(h, wr_ref[...], precision=HIGHEST, preferred_element_type=F32)
        lane = lax.broadcasted_iota(jnp.int32, logits.shape, 1)
        logits = jnp.where(lane < N_EXPERTS, logits, NEG)
        m1 = jnp.max(logits, axis=-1, keepdims=True)
        i1 = jnp.min(jnp.where(logits == m1, lane, LANE), axis=-1, keepdims=True)
        rest = jnp.where(lane == i1, NEG, logits)
        m2 = jnp.max(rest, axis=-1, keepdims=True)
        i2 = jnp.min(jnp.where(rest == m2, lane, LANE), axis=-1, keepdims=True)
        e2 = jnp.exp(m2 - m1)
        g1 = 1.0 / (1.0 + e2)
        g2 = e2 / (1.0 + e2)
        ids_ref[...] = jnp.where(lane == 0, i1, i2)
        gates_ref[...] = jnp.where(lane == 0, g1, g2)


def _normmod(x2, shift, scale, g, router_pad, out, row0, n_seq, t_len, tm):
    nblk = t_len // tm
    n_tok = n_seq * t_len
    route = router_pad is not None
    off = row0 // tm

    def row(i):
        return (i, 0)

    def seq(i):
        return (i // nblk, 0, 0)

    mod_rows = shift.shape[1]
    in_specs = [pl.BlockSpec((tm, D_MODEL), row), pl.BlockSpec((1, mod_rows, D_MODEL), seq),
                pl.BlockSpec((1, mod_rows, D_MODEL), seq), pl.BlockSpec((1, D_MODEL), lambda i: (0, 0))]
    args = [x2, shift, scale, g]
    out_shape = [jax.ShapeDtypeStruct(out.shape, F32)]
    out_specs = [_block_rows(tm, out, lambda i: (off + i, 0))]
    if route:
        in_specs.append(pl.BlockSpec((D_MODEL, LANE), lambda i: (0, 0)))
        args.append(router_pad)
        out_shape += [jax.ShapeDtypeStruct((n_tok, LANE), jnp.int32), jax.ShapeDtypeStruct((n_tok, LANE), F32)]
        out_specs += [pl.BlockSpec((tm, LANE), row), pl.BlockSpec((tm, LANE), row)]
    in_specs.append(pl.BlockSpec(memory_space=pl.ANY))
    args.append(out)
    res = pl.pallas_call(
        functools.partial(_normmod_alias_kernel, route=route), grid=(n_tok // tm,),
        in_specs=in_specs, out_specs=tuple(out_specs), out_shape=tuple(out_shape),
        input_output_aliases={len(args) - 1: 0},
        compiler_params=_cparams(("arbitrary",)), name="normmod",
    )(*args)
    return res


def _normmod_alias_kernel(*refs, route):
    n_in = 5 if route else 4
    _normmod_kernel(*refs[:n_in], *refs[n_in + 1:], route=route)


GATHER_ROWS = 256
GATHER_UNROLL = 8


def _gather_kernel(idx_ref, src_ref, out_ref, sem):
    def row_copy(r, src_row):
        return pltpu.make_async_copy(src_ref.at[src_row], out_ref.at[r], sem)

    def start(r, c):
        row_copy(r, idx_ref[0, 0, r]).start()
        return c

    def wait(r, c):
        row_copy(r, 0).wait()
        return c

    lax.fori_loop(0, GATHER_ROWS, start, 0, unroll=GATHER_UNROLL)
    lax.fori_loop(0, GATHER_ROWS, wait, 0, unroll=GATHER_UNROLL)


def _gather_rows(src, idx):
    m = idx.shape[0]
    return pl.pallas_call(
        _gather_kernel, grid=(m // GATHER_ROWS,),
        in_specs=[pl.BlockSpec((1, 1, GATHER_ROWS), lambda i: (i, 0, 0), memory_space=pltpu.SMEM),
                  pl.BlockSpec(memory_space=pl.ANY)],
        out_specs=pl.BlockSpec((GATHER_ROWS,) + src.shape[1:], lambda i: (i, 0, 0)),
        out_shape=jax.ShapeDtypeStruct((m,) + src.shape[1:], src.dtype),
        scratch_shapes=[pltpu.SemaphoreType.DMA(())],
        compiler_params=pltpu.CompilerParams(dimension_semantics=("arbitrary",)), name="gather_rows",
    )(idx.reshape(m // GATHER_ROWS, 1, GATHER_ROWS), src)


def _swiglu_kernel(te_ref, nt_ref, x_ref, wg_ref, wu_ref, wd_ref, o_ref, acc_ref, xs_ref, *, precise):
    mm = functools.partial(_mm, precise=precise)
    i = pl.program_id(0)
    j = pl.program_id(1)

    @pl.when(i < nt_ref[0])
    def _():
        @pl.when(j == 0)
        def _():
            acc_ref[...] = jnp.zeros(acc_ref.shape, F32)
            xs_ref[...] = _load_rows(x_ref).astype(xs_ref.dtype)

        x = xs_ref[...]
        a = mm(x, wg_ref[0])
        u = mm(x, wu_ref[0])
        acc_ref[...] += mm(_silu(a) * u, wd_ref[0])

        @pl.when(j == pl.num_programs(1) - 1)
        def _():
            _store_rows(o_ref, acc_ref[...])

    @pl.when((i >= nt_ref[0]) & (j == pl.num_programs(1) - 1))
    def _():
        o_ref[...] = jnp.zeros(o_ref.shape, F32)


def _swiglu_grouped(x, tile_expert, n_tiles_used, wg, wu, wd, tm, tf, precise=False):
    m = x.shape[0]
    f = wg.shape[2]
    grid_spec = pltpu.PrefetchScalarGridSpec(
        num_scalar_prefetch=2, grid=(m // tm, f // tf),
        in_specs=[_block_rows(tm, x, lambda i, j, te, nt: (i, 0)),
                  pl.BlockSpec((1, D_MODEL, tf), lambda i, j, te, nt: (te[i], 0, j)),
                  pl.BlockSpec((1, D_MODEL, tf), lambda i, j, te, nt: (te[i], 0, j)),
                  pl.BlockSpec((1, tf, D_MODEL), lambda i, j, te, nt: (te[i], j, 0))],
        out_specs=_block_rows(tm, x, lambda i, j, te, nt: (i, 0)),
        scratch_shapes=[pltpu.VMEM((tm, D_MODEL), F32), pltpu.VMEM((tm, D_MODEL), F32 if precise else BF16)])
    return pl.pallas_call(
        functools.partial(_swiglu_kernel, precise=precise), grid_spec=grid_spec,
        out_shape=jax.ShapeDtypeStruct(x.shape, F32),
        compiler_params=_cparams(("arbitrary", "arbitrary")), name="swiglu",
    )(tile_expert, n_tiles_used, x, wg, wu, wd)


def _combine_kernel(*refs, moe, final):
    refs = list(refs)
    x_ref = refs.pop(0)
    gate_ref = refs.pop(0)
    y1_ref = refs.pop(0)
    if moe:
        y2_ref = refs.pop(0)
        gates_ref = refs.pop(0)
    fg_ref = refs.pop(0) if final else None
    o_ref = refs.pop(0)
    y = _load_rows(y1_ref)
    if moe:
        gts = gates_ref[...]
        y = gts[:, 0:1] * y + gts[:, 1:2] * _load_rows(y2_ref)
    out = x_ref[...] + gate_ref[0] * y
    if final:
        out = _rms(out) * fg_ref[...]
    o_ref[...] = out


def _combine(x2, gate, y, y_row0, y2_row0, gates, final_g, n_seq, t_len, tm):
    nblk = t_len // tm
    n_tok = n_seq * t_len
    moe = gates is not None
    final = final_g is not None

    def row(i):
        return (i, 0)

    in_specs = [pl.BlockSpec((tm, D_MODEL), row), pl.BlockSpec((1, gate.shape[1], D_MODEL), lambda i: (i // nblk, 0, 0)),
                _block_rows(tm, y, lambda i: (y_row0 // tm + i, 0))]
    args = [x2, gate, y]
    if moe:
        in_specs += [_block_rows(tm, y, lambda i: (y2_row0 // tm + i, 0)), pl.BlockSpec((tm, LANE), row)]
        args += [y, gates]
    if final:
        in_specs.append(pl.BlockSpec((1, D_MODEL), lambda i: (0, 0)))
        args.append(final_g)
    return pl.pallas_call(
        functools.partial(_combine_kernel, moe=moe, final=final), grid=(n_tok // tm,),
        in_specs=in_specs, out_specs=pl.BlockSpec((tm, D_MODEL), row),
        out_shape=jax.ShapeDtypeStruct((n_tok, D_MODEL), F32),
        compiler_params=_cparams(("arbitrary",)), name="combine",
    )(*args)


def _pack_mixer_weights(w_in, w_gate2, b_gate, fox_b_f, qng, kvng, w_uq, w_uk, w_uv, gla_norm_g):
    offs = np.concatenate([[0], np.cumsum(IN_SPLITS)])
    cols = {n: (int(offs[i]), int(offs[i + 1])) for i, n in enumerate(
        ("gq", "gk", "gv", "ag", "ar", "fq", "fk", "fv", "bf", "cq", "ckv", "kr"))}

    def seg(n):
        return w_in[:, cols[n][0]:cols[n][1]]

    half = ROPE_C // 2
    kr = seg("kr")
    w = jnp.zeros((D_MODEL, N_PACK), F32)
    for n, c0 in (("gq", C_GQ), ("gk", C_GK), ("gv", C_GV), ("ag", C_AG), ("fq", C_FQ), ("fk", C_FK), ("fv", C_FV),
                  ("cq", C_CQ), ("ckv", C_CKV)):
        w = w.at[:, c0:c0 + cols[n][1] - cols[n][0]].set(seg(n))
    w = w.at[:, C_SM + SM_KR:C_SM + SM_KR + ROPE_C].set(kr)
    w = w.at[:, C_SM + SM_AR:C_SM + SM_AR + GATE_RANK].set(seg("ar"))
    w = w.at[:, C_SM + SM_BF:C_SM + SM_BF + H_B].set(seg("bf"))
    w = w.at[:, C_SM + SM_KRS:C_SM + SM_KRS + ROPE_C].set(jnp.concatenate([-kr[:, half:], kr[:, :half]], axis=1))
    wg2 = jnp.zeros((LANE, 2 * LANE), F32).at[SM_AR:SM_AR + GATE_RANK, :QK_A].set(w_gate2)
    bg = jnp.zeros((1, 2 * LANE), F32).at[0, :QK_A].set(b_gate)
    bf = jnp.zeros((1, LANE), F32).at[0, SM_BF:SM_BF + H_B].set(fox_b_f)
    uq = w_uq.reshape(Q_LORA, H_C, NOPE_C + ROPE_C)
    wuqn = uq[:, :, :NOPE_C].reshape(Q_LORA, H_C * NOPE_C)
    x1 = uq[:, :, NOPE_C:NOPE_C + half]
    x2 = uq[:, :, NOPE_C + half:]
    pad = jnp.zeros((Q_LORA, H_C, LANE - ROPE_C), F32)
    wr = jnp.concatenate([x1, x2, pad], axis=2).reshape(Q_LORA, H_C * LANE)
    wrs = jnp.concatenate([-x2, x1, pad], axis=2).reshape(Q_LORA, H_C * LANE)
    wuk = jnp.zeros((H_C * NOPE_C, H_C * KV_LORA), F32)
    wuv = jnp.zeros((H_C, KV_LORA, V_A), F32)
    for hh in range(H_C):
        wuk = wuk.at[hh * NOPE_C:(hh + 1) * NOPE_C, hh * KV_LORA:(hh + 1) * KV_LORA].set(w_uk[:, hh, :].T)
        wuv = wuv.at[hh, :, hh * V_C:(hh + 1) * V_C].set(w_uv[:, hh, :])
    full = dict(w_in=w, wg2=wg2, bg=bg, bf=bf, qng=qng.reshape(1, Q_LORA), kvng=kvng.reshape(1, KV_LORA), wuqn=wuqn,
                wuk=wuk, wr=wr, wrs=wrs, wuv=wuv, gnorm=jnp.tile(gla_norm_g, H_A).reshape(1, V_A))
    half_prec = dict(full)
    for n in ("w_in", "wg2", "wuqn", "wuk", "wr", "wrs", "wuv"):
        half_prec[n] = full[n].astype(BF16)
    return half_prec, full


def _rope_tables(pos):
    half = ROPE_C // 2
    inv_freq = ROPE_BASE ** (-jnp.arange(half, dtype=F32) / half)
    ang = pos.astype(F32)[:, None] * inv_freq[None, :]
    n = pos.shape[0]
    cos = jnp.concatenate([jnp.cos(ang), jnp.cos(ang), jnp.ones((n, LANE - ROPE_C), F32)], axis=1)
    sin = jnp.concatenate([jnp.sin(ang), jnp.sin(ang), jnp.zeros((n, LANE - ROPE_C), F32)], axis=1)
    return cos, sin


def _state_to_t(s):
    b = s.shape[0]
    out = jnp.zeros((b, V_A, QK_A), F32)
    for hh in range(H_A):
        out = out.at[:, hh * DV_A:(hh + 1) * DV_A, hh * DK_A:(hh + 1) * DK_A].set(jnp.swapaxes(s[:, hh], 1, 2))
    return out


def _state_from_t(s_t):
    return jnp.stack([jnp.swapaxes(s_t[:, hh * DV_A:(hh + 1) * DV_A, hh * DK_A:(hh + 1) * DK_A], 1, 2)
                      for hh in range(H_A)], axis=1)


def _round_up(a, b):
    return (a + b - 1) // b * b


def kernel(x_prompt, x_sample, c_prompt, c_sample, cache_fox_k, cache_fox_v, cache_fox_logf, cache_mla_ckv, cache_mla_krope, state_gla, ada_w, ada_b, norm_mix_g, norm_ffn_g, w_in, gla_w_gate2, gla_b_gate, gla_norm_g, fox_b_f, mla_q_norm_g, mla_kv_norm_g, mla_w_uq, mla_w_uk, mla_w_uv, w_out, ffn_w_gate, ffn_w_up, ffn_w_down, moe_router, moe_w_gate, moe_w_up, moe_w_down, final_norm_g):
    bp, tp, _ = x_prompt.shape
    bs, ts, _ = x_sample.shape
    past = cache_fox_k.shape[2]
    np_tok, ns_tok = bp * tp, bs * ts
    n_all = np_tok + ns_tok

    tm_p, tm_s = 256, ns_tok
    tq_fox, tk = 512, 512
    tk_pad = past + tk
    kv_len_s = past + ts

    nc = _round_up(bp + bs, 8)
    c_all = jnp.zeros((nc, D_MODEL), F32).at[:bp].set(c_prompt).at[bp:bp + bs].set(c_sample)
    mod = _ada(c_all, ada_w, ada_b)

    cos_p, sin_p = _rope_tables(jnp.arange(tp))
    cos_s, sin_s = _rope_tables(past + jnp.tile(jnp.arange(ts), bs))

    def seq_major(a):
        hh, d = a.shape[1], a.shape[3]
        return jnp.transpose(a.reshape(hh, bs, ts, d), (1, 0, 2, 3))

    def tok_major(a):
        hh, d = a.shape[1], a.shape[3]
        return jnp.transpose(a, (1, 0, 2, 3)).reshape(1, hh, bs * ts, d)

    cl = jnp.transpose(cache_fox_logf.astype(F32), (0, 1, 3, 2)).reshape(DEPTH * bs * H_B, past)
    f_cache, f_cache2 = [a.reshape(DEPTH, bs * H_B, past) for a in _cumsum(cl, jnp.zeros((cl.shape[0], 1), F32), 512)]

    xp = x_prompt.reshape(np_tok, D_MODEL)
    xs = x_sample.reshape(ns_tok, D_MODEL)
    p_states = [[] for _ in range(6)]
    s_states = [[] for _ in range(6)]

    for l in range(DEPTH):
        mods = [mod[l, :, i * D_MODEL:(i + 1) * D_MODEL] for i in range(6)]
        mp = [m[:bp].reshape(bp, 1, D_MODEL) for m in mods]
        ms = [jnp.repeat(m[bp:bp + bs], ts, axis=0).reshape(1, ns_tok, D_MODEL) for m in mods]
        pw, pw32 = _pack_mixer_weights(w_in[l], gla_w_gate2[l], gla_b_gate[l], fox_b_f[l], mla_q_norm_g[l],
                                       mla_kv_norm_g[l], mla_w_uq[l], mla_w_uk[l], mla_w_uv[l], gla_norm_g[l])
        wo = w_out[l].astype(BF16)
        g_mix = norm_mix_g[l].reshape(1, D_MODEL)
        g_ffn = norm_ffn_g[l].reshape(1, D_MODEL)

        (gq, gk, gv, ag, gla, fqh, fkh, fvh, fk, fv, ckv, kc, small, qs) = _inproj(
            xp, mp[0], mp[1], g_mix, pw, cos_p, sin_p, bp, tp, tm_p, False)
        logf = small[:, SM_BF:SM_BF + H_B]
        krope = small[:, SM_KR:SM_KR + ROPE_C]
        f_rows = jnp.transpose(logf.reshape(bp, tp, H_B), (0, 2, 1)).reshape(bp * H_B, tp)
        f_cum = _cumsum(f_rows, jnp.zeros((bp * H_B, 1), F32), 512)[1]
        ya, s_t = _gla(gq, gk, gv, gla, ag, pw["gnorm"], jnp.zeros((bp, V_A, QK_A), F32), bp, tp, 512, CHUNK, False)
        g_fox = bp * H_B
        o_fox = _flash(fqh.reshape(g_fox, tp // tq_fox, tq_fox, D_B), fkh.reshape(g_fox, tp, D_B),
                       fvh.reshape(g_fox, tp, 2 * D_B), f_cum.reshape(g_fox, tp // tq_fox, tq_fox, 1),
                       f_cum.reshape(g_fox, 1, tp), tq=tq_fox, rep=1, tk=tk, q0=0, mode="causal", kv_len=tp, dv=D_B)
        o_mla = _flash(qs.reshape(bp, tp // tm_p, H_C * tm_p, 2 * LANE), kc.reshape(bp, tp, 2 * LANE), None, None, None,
                       tq=tm_p, rep=H_C, tk=tk, q0=0, mode="chunk", kv_len=tp, dv=KV_LORA)
        xp = _mixout(xp, mp[2], ya, o_fox.reshape(bp, H_B, tp, D_B),
                     o_mla.reshape(np_tok // tm_p, H_C, tm_p, KV_LORA), pw["wuv"], wo, bp, tp, tm_p, False)
        for i, st in enumerate((_state_from_t(s_t), fk.reshape(bp, tp, H_B, D_B), fv.reshape(bp, tp, H_B, D_B),
                                logf.reshape(bp, tp, H_B), ckv.reshape(bp, tp, KV_LORA), krope.reshape(bp, tp, ROPE_C))):
            p_states[i].append(st)

        (gq, gk, gv, ag, gla, fqh, fkh, fvh, fk, fv, ckv, kc, small, qs) = _inproj(
            xs, ms[0], ms[1], g_mix, pw32, cos_s, sin_s, 1, ns_tok, tm_s, True)
        logf = small[:, SM_BF:SM_BF + H_B]
        krope = small[:, SM_KR:SM_KR + ROPE_C]
        g_fox = bs * H_B
        f_rows = jnp.transpose(logf.reshape(bs, ts, H_B), (0, 2, 1)).reshape(g_fox, ts)
        f_new = _cumsum(f_rows, f_cache[l][:, past - 1:past], ts)[1]
        f_keys = jnp.concatenate([f_cache2[l], f_new, jnp.zeros((g_fox, tk_pad - kv_len_s), F32)], axis=1)
        ya, s_t = _gla(gq, gk, gv, gla, ag, pw["gnorm"], _state_to_t(state_gla[l].astype(F32)), bs, ts, ts, ts, True)

        def keys(cache, new):
            w = new.shape[3]
            c = jnp.transpose(cache.astype(F32), (0, 2, 1, 3))
            if w > D_B:
                c = jnp.concatenate([c, jnp.ones((bs, H_B, past, w - D_B), F32)], axis=3)
            full = jnp.concatenate([c, new, jnp.zeros((bs, H_B, tk_pad - kv_len_s, w), F32)], axis=2)
            return full.reshape(g_fox, tk_pad, w)

        o_fox = _flash(seq_major(fqh).reshape(g_fox, 1, ts, D_B), keys(cache_fox_k[l], seq_major(fkh)),
                       keys(cache_fox_v[l], seq_major(fvh)),
                       f_new.reshape(g_fox, 1, ts, 1), f_keys.reshape(g_fox, 1, tk_pad),
                       tq=ts, rep=1, tk=tk, q0=past, mode="causal", kv_len=kv_len_s, dv=D_B, precise=True)
        kc_cache = jnp.concatenate([cache_mla_ckv[l].astype(F32), cache_mla_krope[l].astype(F32),
                                    jnp.zeros((bs, past, LANE - ROPE_C), F32)], axis=2)
        kc_all = jnp.concatenate([kc_cache, kc.reshape(bs, ts, 2 * LANE),
                                  jnp.zeros((bs, tk_pad - kv_len_s, 2 * LANE), F32)], axis=1)
        o_mla = _flash(seq_major(qs).reshape(bs, 1, H_C * ts, 2 * LANE), kc_all, None, None, None,
                       tq=ts, rep=H_C, tk=tk, q0=past, mode="chunk", kv_len=kv_len_s, dv=KV_LORA, precise=True)
        xs = _mixout(xs, ms[2], ya, tok_major(o_fox.reshape(bs, H_B, ts, D_B)),
                     tok_major(o_mla.reshape(bs, H_C, ts, KV_LORA)), pw32["wuv"], w_out[l], 1, ns_tok, tm_s, True)
        for i, st in enumerate((_state_from_t(s_t), fk.reshape(bs, ts, H_B, D_B), fv.reshape(bs, ts, H_B, D_B),
                                logf.reshape(bs, ts, H_B), ckv.reshape(bs, ts, KV_LORA), krope.reshape(bs, ts, ROPE_C))):
            s_states[i].append(st)

        last = l == DEPTH - 1
        fg = final_norm_g.reshape(1, D_MODEL) if last else None
        tm_f = 512
        if l % 2 == 0:
            j = l // 2
            h_p = _normmod(xp, mp[3], mp[4], g_ffn, None, jnp.zeros((np_tok, D_MODEL), F32), 0, bp, tp, tm_p)[0]
            h_s = _normmod(xs, ms[3], ms[4], g_ffn, None, jnp.zeros((ns_tok, D_MODEL), F32), 0, 1, ns_tok, tm_s)[0]
            n_tiles = np_tok // tm_f
            y_p = _swiglu_grouped(h_p, jnp.zeros((n_tiles,), jnp.int32), jnp.full((1,), n_tiles, jnp.int32),
                                  ffn_w_gate[j:j + 1].astype(BF16), ffn_w_up[j:j + 1].astype(BF16),
                                  ffn_w_down[j:j + 1].astype(BF16), tm_f, 1408)
            y_s = _swiglu_grouped(h_s, jnp.zeros((1,), jnp.int32), jnp.ones((1,), jnp.int32),
                                  ffn_w_gate[j:j + 1], ffn_w_up[j:j + 1], ffn_w_down[j:j + 1], tm_s, 1408, precise=True)
            xp = _combine(xp, mp[5], y_p, 0, 0, None, fg, bp, tp, tm_p)
            xs = _combine(xs, ms[5], y_s, 0, 0, None, fg, 1, ns_tok, tm_s)
        else:
            j = l // 2
            router_pad = jnp.zeros((D_MODEL, LANE), F32).at[:, :N_EXPERTS].set(moe_router[j])
            h_all = jnp.zeros((n_all,) + TOK_TILE, F32)
            h_all, ids_p, gates_p = _normmod(xp, mp[3], mp[4], g_ffn, router_pad, h_all, 0, bp, tp, tm_p)
            h_all, ids_s, gates_s = _normmod(xs, ms[3], ms[4], g_ffn, router_pad, h_all, np_tok, 1, ns_tok, tm_s)
            ids = jnp.concatenate([ids_p[:, :2], ids_s[:, :2]], axis=0)
            e = jnp.transpose(ids).reshape(-1)
            onehot = (e[:, None] == jnp.arange(N_EXPERTS)[None, :]).astype(jnp.int32)
            rank = jnp.sum((jnp.cumsum(onehot, axis=0) - onehot) * onehot, axis=1)
            counts = jnp.sum(onehot, axis=0)
            padded = (counts + tm_f - 1) // tm_f * tm_f
            ends = jnp.cumsum(padded)
            starts = ends - padded
            pos = starts[e] + rank
            m_pad = _round_up(2 * n_all + N_EXPERTS * (tm_f - 1), tm_f)
            token = jnp.tile(jnp.arange(n_all, dtype=jnp.int32), 2)
            src = jnp.zeros((m_pad,), jnp.int32).at[pos].set(token)
            n_tiles = m_pad // tm_f
            tile_row0 = jnp.arange(n_tiles, dtype=jnp.int32) * tm_f
            tile_expert = jnp.minimum(jnp.sum((ends[None, :] <= tile_row0[:, None]).astype(jnp.int32), axis=1),
                                      N_EXPERTS - 1)
            n_used = (ends[-1] // tm_f).astype(jnp.int32).reshape(1)
            x_sorted = _gather_rows(h_all, src)
            y = _swiglu_grouped(x_sorted, tile_expert, n_used, moe_w_gate[j].astype(BF16), moe_w_up[j].astype(BF16),
                                moe_w_down[j].astype(BF16), tm_f, 1792)
            n_back = _round_up(n_all, GATHER_ROWS)
            back = jnp.zeros((2 * n_back,), jnp.int32).at[:n_all].set(pos[:n_all]).at[n_back:n_back + n_all].set(pos[n_all:])
            yg = _gather_rows(y, back)
            xp = _combine(xp, mp[5], yg, 0, n_back, gates_p, fg, bp, tp, tm_p)
            xs = _combine(xs, ms[5], yg, np_tok, n_back + np_tok, gates_s, fg, 1, ns_tok, tm_s)

    outs_p = [jnp.stack(s, axis=0) for s in p_states]
    outs_s = [jnp.stack(s, axis=0) for s in s_states]
    return (xp.reshape(bp, tp, D_MODEL), xs.reshape(bs, ts, D_MODEL), *outs_p, *outs_s)
```

```python
import functools

import numpy as np
import jax
import jax.numpy as jnp
from jax import lax
from jax.experimental import pallas as pl
from jax.experimental.pallas import tpu as pltpu

F32 = jnp.float32
BF16 = jnp.bfloat16
HIGHEST = lax.Precision.HIGHEST

D_MODEL = 1024
DEPTH = 2
CHUNK = 64
EPS = 1e-6
H_A, DK_A, DV_A = 6, 32, 64
GATE_RANK = 16
GATE_TAU = 16.0
H_B, D_B = 4, 64
H_C, NOPE_C, ROPE_C, V_C = 6, 64, 32, 64
Q_LORA, KV_LORA = 256, 128
ROPE_BASE = 10000.0
N_EXPERTS = 8
IN_SPLITS = (H_A * DK_A, H_A * DK_A, H_A * DV_A, H_A * DV_A, GATE_RANK,
             H_B * D_B, H_B * D_B, H_B * D_B, H_B, Q_LORA, KV_LORA, ROPE_C)

QK_A = H_A * DK_A
V_A = H_A * DV_A
QKV_B = H_B * D_B
LANE = 128
NEG = -1e30

C_GQ, C_GK, C_GV, C_AG = 0, 256, 512, 896
C_FQ, C_FK, C_FV = 1280, 1536, 1792
C_CQ, C_CKV, C_SM = 2048, 2304, 2432
N_PACK = 2560
SM_KR, SM_AR, SM_BF, SM_KRS = 0, 32, 48, 64

GLA_SUB = 16
FLASH_ROW_GROUP = 256
LOG2E = float(np.log2(np.e))
VMEM_LIMIT = 56 * 1024 * 1024


def _cparams(sem):
    return pltpu.CompilerParams(dimension_semantics=sem, vmem_limit_bytes=VMEM_LIMIT)


def _log_sigmoid(z):
    return jnp.minimum(z, 0.0) - jnp.log1p(jnp.exp(-jnp.abs(z)))


def _silu(z):
    return z * (1.0 / (1.0 + jnp.exp(-z)))


def _rms(x):
    return x * lax.rsqrt(jnp.mean(x * x, axis=-1, keepdims=True) + EPS)


def _dot(a, b):
    return jnp.dot(a, b, preferred_element_type=F32)


def _dot_nt(a, b):
    return lax.dot_general(a, b, (((1,), (1,)), ((), ())), preferred_element_type=F32)


def _dot_tn(a, b):
    return lax.dot_general(a, b, (((0,), (0,)), ((), ())), preferred_element_type=F32)


def _mm(a, b, dims=(((1,), (0,)), ((), ())), *, precise):
    if precise:
        return lax.dot_general(a.astype(F32), b.astype(F32), dims, precision=HIGHEST, preferred_element_type=F32)
    return lax.dot_general(a.astype(BF16), b.astype(BF16), dims, preferred_element_type=F32)


_NT = (((1,), (1,)), ((), ()))
_TN = (((0,), (0,)), ((), ()))

SUBLANE = 8
TOK_TILE = (D_MODEL // LANE, LANE)
assert TOK_TILE[0] == SUBLANE


def _block_rows(tm, arr, index_map):
    if arr.ndim == 3:
        return pl.BlockSpec((tm,) + TOK_TILE, lambda *a: index_map(*a) + (0,))
    return pl.BlockSpec((tm, D_MODEL), index_map)


def _load_rows(ref):
    if len(ref.shape) == 3:
        return jnp.concatenate([ref[:, s, :] for s in range(TOK_TILE[0])], axis=-1)
    return ref[...]


def _store_rows(ref, val):
    if len(ref.shape) == 3:
        for s in range(TOK_TILE[0]):
            ref[:, s, :] = val[:, s * LANE:(s + 1) * LANE]
    else:
        ref[...] = val


def _ada_kernel(c_ref, w_ref, b_ref, o_ref):
    s = _silu(c_ref[...])
    o_ref[0] = jnp.dot(s, w_ref[0], precision=HIGHEST, preferred_element_type=F32) + b_ref[0]


def _ada(c_all, ada_w, ada_b):
    nc = c_all.shape[0]
    tn = 1536
    return pl.pallas_call(
        _ada_kernel,
        grid=(DEPTH, 6 * D_MODEL // tn),
        in_specs=[pl.BlockSpec((nc, D_MODEL), lambda l, j: (0, 0)),
                  pl.BlockSpec((1, D_MODEL, tn), lambda l, j: (l, 0, j)),
                  pl.BlockSpec((1, 1, tn), lambda l, j: (l, 0, j))],
        out_specs=pl.BlockSpec((1, nc, tn), lambda l, j: (l, 0, j)),
        out_shape=jax.ShapeDtypeStruct((DEPTH, nc, 6 * D_MODEL), F32),
        compiler_params=_cparams(("arbitrary", "arbitrary")),
        name="ada",
    )(c_all, ada_w, ada_b.reshape(DEPTH, 1, 6 * D_MODEL))


def _inproj_kernel(x_ref, sh_ref, sc_ref, g_ref, w_ref, wg2_ref, bg_ref, bf_ref, qng_ref, kvng_ref,
                   wuqn_ref, wuk_ref, wr_ref, wrs_ref, cos_ref, sin_ref,
                   gq_ref, gk_ref, gv_ref, ag_ref, gla_ref, fk_ref, fv_ref,
                   ckv_ref, kc_ref, small_ref, qs_ref, *fox_refs, precise):
    mm = functools.partial(_mm, precise=precise)
    act = kc_ref.dtype
    x = x_ref[...]
    h = (_rms(x) * g_ref[...]) * (1.0 + sc_ref[0]) + sh_ref[0]
    p = mm(h, w_ref[...])

    gq_ref[...] = p[:, C_GQ:C_GQ + QK_A] * (DK_A ** -0.5)
    gk_ref[...] = p[:, C_GK:C_GK + QK_A]
    gv_ref[...] = p[:, C_GV:C_GV + V_A]
    ag_ref[...] = p[:, C_AG:C_AG + V_A]
    sm = p[:, C_SM:C_SM + LANE]
    z = mm(sm, wg2_ref[...]) + bg_ref[...]
    gla_ref[...] = _log_sigmoid(z[:, :QK_A]) * (1.0 / GATE_TAU)

    fq = p[:, C_FQ:C_FQ + QKV_B] * (D_B ** -0.5 * LOG2E)
    fk = p[:, C_FK:C_FK + QKV_B]
    fv = p[:, C_FV:C_FV + QKV_B]
    fk_ref[...] = fk
    fv_ref[...] = fv
    if len(fox_refs) == 1:
        fox_refs[0][...] = fq
    else:
        fqh_ref, fkh_ref, fvh_ref = fox_refs
        for hh in range(H_B):
            sl = slice(hh * D_B, (hh + 1) * D_B)
            fqh_ref[0, hh] = fq[:, sl].astype(act)
            fkh_ref[0, hh] = fk[:, sl].astype(act)
            fvh_ref[0, hh, :, :D_B] = fv[:, sl].astype(act)
            fvh_ref[0, hh, :, D_B:] = jnp.ones((fv.shape[0], D_B), act)

    cos = cos_ref[...]
    sin = sin_ref[...]
    lane = lax.broadcasted_iota(jnp.int32, sm.shape, 1)
    kr = sm * cos + pltpu.roll(sm, LANE - SM_KRS, 1) * sin
    logf = _log_sigmoid(sm + bf_ref[...])
    small_ref[...] = jnp.where((lane >= SM_BF) & (lane < SM_BF + H_B), logf, kr)

    ckv = _rms(p[:, C_CKV:C_CKV + KV_LORA]) * kvng_ref[...]
    ckv_ref[...] = ckv
    kc_ref[:, :KV_LORA] = ckv.astype(act)
    kc_ref[:, KV_LORA:] = jnp.where(lane < ROPE_C, kr, 0.0).astype(act)
    cqn = _rms(p[:, C_CQ:C_CQ + Q_LORA]) * qng_ref[...]
    nope = mm(cqn, wuqn_ref[...])
    qlat = mm(nope, wuk_ref[...])
    qa = mm(cqn, wr_ref[...])
    qb = mm(cqn, wrs_ref[...])
    scale = (NOPE_C + ROPE_C) ** -0.5 * LOG2E
    for hh in range(H_C):
        sl = slice(hh * LANE, (hh + 1) * LANE)
        qs_ref[0, hh, :, :KV_LORA] = (qlat[:, sl] * scale).astype(act)
        qs_ref[0, hh, :, KV_LORA:] = ((qa[:, sl] * cos + qb[:, sl] * sin) * scale).astype(act)


def _inproj(x2, shift, scale, g, pw, cos_tab, sin_tab, n_seq, t_len, tm, precise, fox_head_major):
    n_tok = n_seq * t_len
    nblk = t_len // tm
    act = F32 if precise else BF16
    mod_rows = shift.shape[1]

    def row(i):
        return (i, 0)

    def seq(i):
        return (i // nblk, 0, 0)

    def const2(i):
        return (0, 0)

    def tab(i):
        return (i % nblk, 0)

    def headmajor(i):
        return (i // nblk, 0, i % nblk, 0)

    sds = jax.ShapeDtypeStruct
    out_shape = (
        sds((n_tok, QK_A), F32), sds((n_tok, QK_A), F32), sds((n_tok, V_A), F32), sds((n_tok, V_A), F32),
        sds((n_tok, QK_A), F32),
        sds((n_tok, QKV_B), F32), sds((n_tok, QKV_B), F32),
        sds((n_tok, KV_LORA), F32), sds((n_tok, 2 * LANE), act), sds((n_tok, LANE), F32),
        sds((n_tok // tm, H_C, tm, 2 * LANE), act),
    )
    out_specs = (
        pl.BlockSpec((tm, QK_A), row), pl.BlockSpec((tm, QK_A), row), pl.BlockSpec((tm, V_A), row),
        pl.BlockSpec((tm, V_A), row), pl.BlockSpec((tm, QK_A), row),
        pl.BlockSpec((tm, QKV_B), row), pl.BlockSpec((tm, QKV_B), row),
        pl.BlockSpec((tm, KV_LORA), row), pl.BlockSpec((tm, 2 * LANE), row), pl.BlockSpec((tm, LANE), row),
        pl.BlockSpec((1, H_C, tm, 2 * LANE), lambda i: (i, 0, 0, 0)),
    )
    if fox_head_major:
        out_shape += (sds((n_seq, H_B, t_len, D_B), act), sds((n_seq, H_B, t_len, D_B), act),
                      sds((n_seq, H_B, t_len, 2 * D_B), act))
        out_specs += (pl.BlockSpec((1, H_B, tm, D_B), headmajor), pl.BlockSpec((1, H_B, tm, D_B), headmajor),
                      pl.BlockSpec((1, H_B, tm, 2 * D_B), headmajor))
    else:
        out_shape += (sds((n_tok, QKV_B), F32),)
        out_specs += (pl.BlockSpec((tm, QKV_B), row),)
    in_specs = [
        pl.BlockSpec((tm, D_MODEL), row), pl.BlockSpec((1, mod_rows, D_MODEL), seq),
        pl.BlockSpec((1, mod_rows, D_MODEL), seq), pl.BlockSpec((1, D_MODEL), const2),
        pl.BlockSpec((D_MODEL, N_PACK), const2), pl.BlockSpec((LANE, 2 * LANE), const2),
        pl.BlockSpec((1, 2 * LANE), const2), pl.BlockSpec((1, LANE), const2),
        pl.BlockSpec((1, Q_LORA), const2), pl.BlockSpec((1, KV_LORA), const2),
        pl.BlockSpec((Q_LORA, H_C * NOPE_C), const2), pl.BlockSpec((H_C * NOPE_C, H_C * KV_LORA), const2),
        pl.BlockSpec((Q_LORA, H_C * LANE), const2), pl.BlockSpec((Q_LORA, H_C * LANE), const2),
        pl.BlockSpec((tm, LANE), tab), pl.BlockSpec((tm, LANE), tab),
    ]
    return pl.pallas_call(
        functools.partial(_inproj_kernel, precise=precise), grid=(n_tok // tm,), in_specs=in_specs,
        out_specs=out_specs, out_shape=out_shape, compiler_params=_cparams(("arbitrary",)), name="inproj",
    )(x2, shift, scale, g, pw["w_in"], pw["wg2"], pw["bg"], pw["bf"], pw["qng"], pw["kvng"],
      pw["wuqn"], pw["wuk"], pw["wr"], pw["wrs"], cos_tab, sin_tab)


def _cumsum_kernel(x_ref, init_ref, o_ref, o2_ref, *, tb):
    n = x_ref.shape[1]
    upper = (lax.broadcasted_iota(jnp.int32, (tb, tb), 0) <= lax.broadcasted_iota(jnp.int32, (tb, tb), 1)).astype(F32)
    carry = init_ref[...]
    for j in range(n // tb):
        blk = jnp.dot(x_ref[:, j * tb:(j + 1) * tb], upper, precision=HIGHEST, preferred_element_type=F32) + carry
        o_ref[:, j * tb:(j + 1) * tb] = blk
        o2_ref[:, j * tb:(j + 1) * tb] = blk * LOG2E
        carry = blk[:, tb - 1:tb]


def _cumsum(x, init, tb):
    return pl.pallas_call(
        functools.partial(_cumsum_kernel, tb=tb),
        out_shape=(jax.ShapeDtypeStruct(x.shape, F32), jax.ShapeDtypeStruct(x.shape, F32)), name="cumsum",
    )(x, init)


def _gla_kernel(q_ref, k_ref, v_ref, la_ref, ag_ref, gn_ref, s0_ref, y_ref, sout_ref, s_ref, *, chunk, n_chunks,
                precise):
    mm = functools.partial(_mm, precise=precise)
    c = chunk
    nsub = c // GLA_SUB

    nb = q_ref.shape[0]

    @pl.when(pl.program_id(1) == 0)
    def _():
        s_ref[...] = s0_ref[...]

    lane_qk = lax.broadcasted_iota(jnp.int32, (GLA_SUB, QK_A), 1) // DK_A
    lane_v = lax.broadcasted_iota(jnp.int32, (GLA_SUB, V_A), 1) // DV_A
    bd = (lax.broadcasted_iota(jnp.int32, (V_A, QK_A), 0) // DV_A) == (lax.broadcasted_iota(jnp.int32, (V_A, QK_A), 1) // DK_A)
    tril = (lax.broadcasted_iota(jnp.int32, (c, c), 0) >= lax.broadcasted_iota(jnp.int32, (c, c), 1)).astype(F32)
    hm = (lax.broadcasted_iota(jnp.int32, (V_A, V_A), 0) // DV_A) == (lax.broadcasted_iota(jnp.int32, (V_A, V_A), 1) // DV_A)
    head_mean = jnp.where(hm, 1.0 / DV_A, 0.0).astype(F32)

    def cumsum_rows(la):
        if precise:
            return jnp.dot(tril, la, precision=HIGHEST, preferred_element_type=F32)
        hi = la.astype(BF16)
        lo = (la - hi.astype(F32)).astype(BF16)
        tb16 = tril.astype(BF16)
        return _dot(tb16, hi) + _dot(tb16, lo)

    def chunk_step(bb, r):
        q = q_ref[bb, pl.ds(r, c), :]
        k = k_ref[bb, pl.ds(r, c), :]
        v = v_ref[bb, pl.ds(r, c), :]
        la = la_ref[bb, pl.ds(r, c), :]
        b = cumsum_rows(la)
        s_t = s_ref[bb]
        vb = v if precise else v.astype(BF16)
        o_inter = mm(q * jnp.exp(b), s_t, _NT)
        outs = []
        for i in range(nsub):
            r0 = i * GLA_SUB
            r1 = r0 + GLA_SUB
            bi = b[r0 - 1:r0] if i > 0 else jnp.zeros((1, QK_A), F32)
            qi = q[r0:r1] * jnp.exp(b[r0:r1] - bi)
            kk = k[:r1] * jnp.exp(bi - b[:r1])
            qst = jnp.concatenate([jnp.where(lane_qk == hh, qi, 0.0) for hh in range(H_A)], axis=0)
            att = mm(qst, kk, _NT)
            t_idx = r0 + lax.broadcasted_iota(jnp.int32, att.shape, 0) % GLA_SUB
            s_idx = lax.broadcasted_iota(jnp.int32, att.shape, 1)
            att = jnp.where(s_idx <= t_idx, att, 0.0)
            oi = mm(att, vb[:r1])
            o = jnp.zeros((GLA_SUB, V_A), F32)
            for hh in range(H_A):
                o = o + jnp.where(lane_v == hh, oi[hh * GLA_SUB:(hh + 1) * GLA_SUB], 0.0)
            outs.append(o)
        o = jnp.concatenate(outs, axis=0) + o_inter if nsub > 1 else outs[0] + o_inter
        b_last = b[c - 1:c]
        kd = k * jnp.exp(b_last - b)
        s_ref[bb] = s_t * jnp.exp(b_last) + jnp.where(bd, mm(vb, kd, _TN), 0.0)
        ms = mm(o * o, head_mean)
        y = o * lax.rsqrt(ms + EPS) * gn_ref[...] * _silu(ag_ref[bb, pl.ds(r, c), :])
        y_ref[bb, pl.ds(r, c), :] = y.astype(y_ref.dtype)

    def body(ci, carry):
        r = pl.multiple_of(ci * c, c)
        for bb in range(nb):
            chunk_step(bb, r)
        return carry

    lax.fori_loop(0, n_chunks, body, 0)

    @pl.when(pl.program_id(1) == pl.num_programs(1) - 1)
    def _():
        sout_ref[...] = s_ref[...]


def _gla(gq, gk, gv, gla, ag, gnorm, s0_t, n_seq, t_len, tb, chunk, nb, precise):
    nblk = t_len // tb

    def row(b, j):
        return (b, j, 0)

    def st(b, j):
        return (b, 0, 0)

    def seq3(a):
        return a.reshape(n_seq, t_len, a.shape[-1])

    ya, s_t = pl.pallas_call(
        functools.partial(_gla_kernel, chunk=chunk, n_chunks=tb // chunk, precise=precise),
        grid=(n_seq // nb, nblk),
        in_specs=[pl.BlockSpec((nb, tb, QK_A), row), pl.BlockSpec((nb, tb, QK_A), row), pl.BlockSpec((nb, tb, V_A), row),
                  pl.BlockSpec((nb, tb, QK_A), row), pl.BlockSpec((nb, tb, V_A), row),
                  pl.BlockSpec((1, V_A), lambda b, j: (0, 0)), pl.BlockSpec((nb, V_A, QK_A), st)],
        out_specs=(pl.BlockSpec((nb, tb, V_A), row), pl.BlockSpec((nb, V_A, QK_A), st)),
        out_shape=(jax.ShapeDtypeStruct((n_seq, t_len, V_A), F32 if precise else BF16),
                   jax.ShapeDtypeStruct((n_seq, V_A, QK_A), F32)),
        scratch_shapes=[pltpu.VMEM((nb, V_A, QK_A), F32)],
        compiler_params=_cparams(("arbitrary", "arbitrary")), name="gla",
    )(seq3(gq), seq3(gk), seq3(gv), seq3(gla), seq3(ag), gnorm, s0_t)
    return ya.reshape(n_seq * t_len, V_A), s_t


def _flash_kernel(*refs, tq, rep, hp, tk, q0, mode, kv_len, dv, bias, v_from_k, precise):
    mm = functools.partial(_mm, precise=precise)
    refs = list(refs)
    q_ref = refs.pop(0)
    k_ref = refs.pop(0)
    v_ref = k_ref if v_from_k else refs.pop(0)
    fq_ref = refs.pop(0) if bias else None
    fk_ref = refs.pop(0) if bias else None
    o_ref, m_ref, acc_ref = refs
    i = pl.program_id(1)
    rows = rep * tq
    m_ref[...] = jnp.full(m_ref.shape, NEG, F32)
    acc_ref[...] = jnp.zeros(acc_ref.shape, F32)

    first_q = q0 + i * tq
    last_q = first_q + tq - 1
    if mode == "chunk":
        vis_all = (first_q // CHUNK) * CHUNK + CHUNK - 1
        vis_any = (last_q // CHUNK) * CHUNK + CHUNK - 1
    else:
        vis_all = first_q
        vis_any = last_q
    vis_all = jnp.minimum(vis_all, kv_len - 1)
    vis_any = jnp.minimum(vis_any, kv_len - 1)
    n_full = (vis_all + 1) // tk
    n_any = vis_any // tk + 1

    grp = min(rows, FLASH_ROW_GROUP)

    def step(jb, masked):
        for hh in range(hp):
            head_step(hh, jb, masked)

    def head_step(hh, jb, masked):
        k_start = pl.multiple_of(jb * tk, tk)
        k = k_ref[hh, pl.ds(k_start, tk), :]
        if v_from_k:
            v = jnp.where(lax.broadcasted_iota(jnp.int32, k.shape, 1) < dv, k, jnp.ones_like(k))
        else:
            v = v_ref[hh, pl.ds(k_start, tk), :]
        for g in range(rows // grp):
            rs = slice(g * grp, (g + 1) * grp)
            s = mm(q_ref[hh, 0, rs, :], k, _NT)
            if bias:
                s = s + fq_ref[hh, 0, rs, :] - fk_ref[hh, jb]
            rs = slice(hh * rows + g * grp, hh * rows + (g + 1) * grp)
            if masked:
                qpos = first_q + (g * grp + lax.broadcasted_iota(jnp.int32, (grp, tk), 0)) % tq
                kpos = k_start + lax.broadcasted_iota(jnp.int32, (grp, tk), 1)
                if mode == "chunk":
                    ok = (kpos // CHUNK) <= (qpos // CHUNK)
                else:
                    ok = kpos <= qpos
                ok = ok & (kpos < kv_len)
                s = jnp.where(ok, s, NEG)
            chunks = [s[:, c * LANE:(c + 1) * LANE] for c in range(tk // LANE)]
            smax = chunks[0]
            for ch in chunks[1:]:
                smax = jnp.maximum(smax, ch)
            m_prev = m_ref[rs, :]
            m_new = jnp.maximum(m_prev, jnp.max(smax, axis=-1, keepdims=True))
            alpha = jnp.exp2(m_prev - m_new)
            p = jnp.concatenate([jnp.exp2(ch - m_new) for ch in chunks], axis=1)
            acc = acc_ref[rs, :]
            alpha_w = alpha if acc.shape[1] == LANE else jnp.concatenate([alpha] * (acc.shape[1] // LANE), axis=1)
            acc_ref[rs, :] = alpha_w * acc + mm(p, v)
            m_ref[rs, :] = m_new

    def loop(lo, hi, masked):
        def body(jb, carry):
            step(jb, masked)
            return carry
        lax.fori_loop(lo, hi, body, 0)

    loop(0, n_full, False)
    loop(n_full, n_any, True)

    acc = acc_ref[...]
    if dv == LANE:
        out = acc[:, :dv] / acc[:, dv:]
    else:
        out = (acc / pltpu.roll(acc, LANE - dv, 1))[:, :dv]
    for hh in range(hp):
        o_ref[hh, 0] = out[hh * rows:(hh + 1) * rows].astype(o_ref.dtype)


def _flash(q, k, v, fq, fk, *, tq, rep, tk, q0, mode, kv_len, dv, hp=1, precise=False):
    g, nq, rows, dqk = q.shape
    t_k = k.shape[1]
    nk = t_k // tk
    bias = fq is not None
    v_from_k = v is None

    def qmap(b, i):
        return (b, i, 0, 0)

    def kmap(b, i):
        return (b, 0, 0)

    in_specs = [pl.BlockSpec((hp, 1, rows, dqk), qmap), pl.BlockSpec((hp, t_k, dqk), kmap)]
    args = [q, k]
    dva = dqk if v_from_k else v.shape[2]
    assert dva % LANE == 0 and dva > dv
    if not v_from_k:
        in_specs.append(pl.BlockSpec((hp, t_k, dva), kmap))
        args.append(v)
    if bias:
        in_specs += [pl.BlockSpec((hp, 1, rows, 1), qmap), pl.BlockSpec((hp, nk, 1, tk), lambda b, i: (b, 0, 0, 0))]
        args += [fq, fk.reshape(g, nk, 1, tk)]
    return pl.pallas_call(
        functools.partial(_flash_kernel, tq=tq, rep=rep, hp=hp, tk=tk, q0=q0, mode=mode, kv_len=kv_len, dv=dv,
                          bias=bias, v_from_k=v_from_k, precise=precise),
        grid=(g // hp, nq), in_specs=in_specs,
        out_specs=pl.BlockSpec((hp, 1, rows, dv), qmap),
        out_shape=jax.ShapeDtypeStruct((g, nq, rows, dv), F32 if precise else BF16),
        scratch_shapes=[pltpu.VMEM((hp * rows, LANE), F32), pltpu.VMEM((hp * rows, dva), F32)],
        compiler_params=_cparams(("arbitrary", "arbitrary")), name="flash_" + mode,
    )(*args)


def _hdot(a, b, dims=(((1,), (0,)), ((), ()))):
    return lax.dot_general(a, b, dims, precision=HIGHEST, preferred_element_type=F32)


def _online_softmax_step(s, v, m_ref, l_ref, acc_ref):
    m_prev = m_ref[...]
    m_new = jnp.maximum(m_prev, jnp.max(s, axis=-1, keepdims=True))
    alpha = jnp.exp2(m_prev - m_new)
    p = jnp.exp2(s - m_new)
    l_ref[...] = alpha * l_ref[...] + jnp.sum(p, axis=-1, keepdims=True)
    acc_ref[...] = alpha * acc_ref[...] + _hdot(p, v)
    m_ref[...] = m_new


def _fox_decode_kernel(q_ref, kc_ref, vc_ref, kn_ref, vn_ref, fq_ref, fkc_ref, fkn_ref, o_ref, m_ref, l_ref, acc_ref,
                       *, tk, past, ts):
    rows = H_B * ts
    q = q_ref[...]
    lane_head = lax.broadcasted_iota(jnp.int32, q.shape, 1) // D_B
    q_st = jnp.concatenate([jnp.where(lane_head == hh, q, 0.0) for hh in range(H_B)], axis=0)
    fq = fq_ref[0]
    fq_st = jnp.concatenate([fq[:, hh:hh + 1] for hh in range(H_B)], axis=0)
    m_ref[...] = jnp.full(m_ref.shape, NEG, F32)
    l_ref[...] = jnp.zeros(l_ref.shape, F32)
    acc_ref[...] = jnp.zeros(acc_ref.shape, F32)

    def cached(jb, carry):
        k0 = pl.multiple_of(jb * tk, tk)
        fk_st = jnp.concatenate([jnp.broadcast_to(fkc_ref[0, 0, hh, pl.ds(jb, 1), :], (ts, tk)) for hh in range(H_B)],
                                axis=0)
        s = _hdot(q_st, kc_ref[0, 0, pl.ds(k0, tk), :], _NT) + fq_st - fk_st
        _online_softmax_step(s, vc_ref[0, 0, pl.ds(k0, tk), :], m_ref, l_ref, acc_ref)
        return carry

    lax.fori_loop(0, past // tk, cached, 0)
    fkn = fkn_ref[0]
    fk_st = jnp.concatenate([jnp.broadcast_to(fkn[hh:hh + 1, :], (ts, ts)) for hh in range(H_B)], axis=0)
    s = _hdot(q_st, kn_ref[...], _NT) + fq_st - fk_st
    t_idx = lax.broadcasted_iota(jnp.int32, s.shape, 0) % ts
    s_idx = lax.broadcasted_iota(jnp.int32, s.shape, 1)
    _online_softmax_step(jnp.where(s_idx <= t_idx, s, NEG), vn_ref[...], m_ref, l_ref, acc_ref)
    res = acc_ref[...] / l_ref[...]
    out = jnp.zeros((ts, H_B * D_B), F32)
    for hh in range(H_B):
        out = out + jnp.where(lane_head == hh, res[hh * ts:(hh + 1) * ts], 0.0)
    o_ref[...] = out


def _fox_decode(q, cache_k, cache_v, layer, k_new, v_new, fq_col, f_cache, f_new, n_seq, ts, tk):
    past = cache_k.shape[2]
    width = H_B * D_B
    rows = H_B * ts

    def tok(b):
        return (b, 0)

    def cache(b):
        return (layer, b, 0, 0)

    return pl.pallas_call(
        functools.partial(_fox_decode_kernel, tk=tk, past=past, ts=ts), grid=(n_seq,),
        in_specs=[pl.BlockSpec((ts, width), tok), pl.BlockSpec((1, 1, past, width), cache),
                  pl.BlockSpec((1, 1, past, width), cache), pl.BlockSpec((ts, width), tok), pl.BlockSpec((ts, width), tok),
                  pl.BlockSpec((1, ts, H_B), lambda b: (b, 0, 0)),
                  pl.BlockSpec((1, 1, H_B, past // tk, tk), lambda b: (layer, b, 0, 0, 0)),
                  pl.BlockSpec((1, H_B, ts), lambda b: (b, 0, 0))],
        out_specs=pl.BlockSpec((ts, width), tok),
        out_shape=jax.ShapeDtypeStruct((n_seq * ts, width), F32),
        scratch_shapes=[pltpu.VMEM((rows, 1), F32), pltpu.VMEM((rows, 1), F32), pltpu.VMEM((rows, width), F32)],
        compiler_params=_cparams(("arbitrary",)), name="fox_decode",
    )(q, cache_k, cache_v, k_new, v_new, fq_col, f_cache, f_new.reshape(n_seq, H_B, ts))


def _mla_decode_kernel(q_ref, cc_ref, cr_ref, cn_ref, kn_ref, o_ref, m_ref, l_ref, acc_ref, *, tk, past, ts):
    q = q_ref[0]
    q_lat = q[:, :KV_LORA]
    q_rope = q[:, KV_LORA:KV_LORA + ROPE_C]
    m_ref[...] = jnp.full(m_ref.shape, NEG, F32)
    l_ref[...] = jnp.zeros(l_ref.shape, F32)
    acc_ref[...] = jnp.zeros(acc_ref.shape, F32)

    def cached(jb, carry):
        k0 = pl.multiple_of(jb * tk, tk)
        ck = cc_ref[0, 0, pl.ds(k0, tk), :]
        s = _hdot(q_lat, ck, _NT) + _hdot(q_rope, cr_ref[0, 0, pl.ds(k0, tk), :], _NT)
        _online_softmax_step(s, ck, m_ref, l_ref, acc_ref)
        return carry

    lax.fori_loop(0, past // tk, cached, 0)
    cn = cn_ref[...]
    s = _hdot(q_lat, cn, _NT) + _hdot(q_rope, kn_ref[:, KV_LORA:KV_LORA + ROPE_C], _NT)
    q_chunk = (past + lax.broadcasted_iota(jnp.int32, s.shape, 0) % ts) // CHUNK
    k_chunk = (past + lax.broadcasted_iota(jnp.int32, s.shape, 1)) // CHUNK
    _online_softmax_step(jnp.where(k_chunk <= q_chunk, s, NEG), cn, m_ref, l_ref, acc_ref)
    o_ref[0] = acc_ref[...] / l_ref[...]


def _mla_decode(q, cache_ckv, cache_krope, layer, ckv_new, kc_new, n_seq, ts, tk):
    past = cache_ckv.shape[2]
    rows = H_C * ts

    def tok(b):
        return (b, 0)

    def cache(b):
        return (layer, b, 0, 0)

    return pl.pallas_call(
        functools.partial(_mla_decode_kernel, tk=tk, past=past, ts=ts), grid=(n_seq,),
        in_specs=[pl.BlockSpec((1, rows, 2 * LANE), lambda b: (b, 0, 0)),
                  pl.BlockSpec((1, 1, past, KV_LORA), cache), pl.BlockSpec((1, 1, past, ROPE_C), cache),
                  pl.BlockSpec((ts, KV_LORA), tok), pl.BlockSpec((ts, 2 * LANE), tok)],
        out_specs=pl.BlockSpec((1, rows, KV_LORA), lambda b: (b, 0, 0)),
        out_shape=jax.ShapeDtypeStruct((n_seq, rows, KV_LORA), F32),
        scratch_shapes=[pltpu.VMEM((rows, 1), F32), pltpu.VMEM((rows, 1), F32), pltpu.VMEM((rows, KV_LORA), F32)],
        compiler_params=_cparams(("arbitrary",)), name="mla_decode",
    )(q, cache_ckv, cache_krope, ckv_new, kc_new)


def _mixout_kernel(x_ref, gate_ref, ya_ref, of_ref, ol_ref, wuv_ref, wo_ref, o_ref, *, precise):
    mm = functools.partial(_mm, precise=precise)
    acc = mm(ya_ref[...], wo_ref[:V_A, :])
    if len(of_ref.shape) == 2:
        acc = acc + mm(of_ref[...], wo_ref[V_A:V_A + QKV_B, :])
    else:
        for hh in range(H_B):
            r0 = V_A + hh * D_B
            acc = acc + mm(of_ref[0, hh], wo_ref[r0:r0 + D_B, :])
    yc = mm(ol_ref[0, 0], wuv_ref[0])
    for hh in range(1, H_C):
        yc = yc + mm(ol_ref[0, hh], wuv_ref[hh])
    acc = acc + mm(yc, wo_ref[V_A + QKV_B:, :])
    o_ref[...] = x_ref[...] + gate_ref[0] * acc


def _mixout(x2, gate, ya, ofox, olat, wuv, wo, n_seq, t_len, tm, precise):
    nblk = t_len // tm
    n_tok = n_seq * t_len
    mod_rows = gate.shape[1]

    def row(i):
        return (i, 0)

    return pl.pallas_call(
        functools.partial(_mixout_kernel, precise=precise), grid=(n_tok // tm,),
        in_specs=[pl.BlockSpec((tm, D_MODEL), row), pl.BlockSpec((1, mod_rows, D_MODEL), lambda i: (i // nblk, 0, 0)),
                  pl.BlockSpec((tm, V_A), row),
                  pl.BlockSpec((tm, QKV_B), row) if ofox.ndim == 2 else
                  pl.BlockSpec((1, H_B, tm, D_B), lambda i: (i // nblk, 0, i % nblk, 0)),
                  pl.BlockSpec((1, H_C, tm, KV_LORA), lambda i: (i, 0, 0, 0)),
                  pl.BlockSpec((H_C, KV_LORA, V_A), lambda i: (0, 0, 0)),
                  pl.BlockSpec((D_MODEL, D_MODEL), lambda i: (0, 0))],
        out_specs=pl.BlockSpec((tm, D_MODEL), row),
        out_shape=jax.ShapeDtypeStruct((n_tok, D_MODEL), F32),
        compiler_params=_cparams(("arbitrary",)), name="mixout",
    )(x2, gate, ya, ofox, olat, wuv, wo)


def _normmod_kernel(*refs, route):
    if route:
        x_ref, sh_ref, sc_ref, g_ref, wr_ref, h_ref, ids_ref, gates_ref = refs
    else:
        x_ref, sh_ref, sc_ref, g_ref, h_ref = refs
    h = (_rms(x_ref[...]) * g_ref[...]) * (1.0 + sc_ref[0]) + sh_ref[0]
    _store_rows(h_ref, h)
    if route:
        logits = jnp.dot(h, wr_ref[...], precision=HIGHEST, preferred_element_type=F32)
        lane = lax.broadcasted_iota(jnp.int32, logits.shape, 1)
        logits = jnp.where(lane < N_EXPERTS, logits, NEG)
        m1 = jnp.max(logits, axis=-1, keepdims=True)
        i1 = jnp.min(jnp.where(logits == m1, lane, LANE), axis=-1, keepdims=True)
        rest = jnp.where(lane == i1, NEG, logits)
        m2 = jnp.max(rest, axis=-1, keepdims=True)
        i2 = jnp.min(jnp.where(rest == m2, lane, LANE), axis=-1, keepdims=True)
        e2 = jnp.exp(m2 - m1)
        g1 = 1.0 / (1.0 + e2)
        g2 = e2 / (1.0 + e2)
        ids_ref[...] = jnp.where(lane == 0, i1, i2)
        gates_ref[...] = jnp.where(lane == 0, g1, g2)


def _normmod(x2, shift, scale, g, router_pad, out, row0, n_seq, t_len, tm):
    nblk = t_len // tm
    n_tok = n_seq * t_len
    route = router_pad is not None
    off = row0 // tm

    def row(i):
        return (i, 0)

    def seq(i):
        return (i // nblk, 0, 0)

    mod_rows = shift.shape[1]
    in_specs = [pl.BlockSpec((tm, D_MODEL), row), pl.BlockSpec((1, mod_rows, D_MODEL), seq),
                pl.BlockSpec((1, mod_rows, D_MODEL), seq), pl.BlockSpec((1, D_MODEL), lambda i: (0, 0))]
    args = [x2, shift, scale, g]
    out_shape = [jax.ShapeDtypeStruct(out.shape, F32)]
    out_specs = [_block_rows(tm, out, lambda i: (off + i, 0))]
    if route:
        in_specs.append(pl.BlockSpec((D_MODEL, LANE), lambda i: (0, 0)))
        args.append(router_pad)
        out_shape += [jax.ShapeDtypeStruct((n_tok, LANE), jnp.int32), jax.ShapeDtypeStruct((n_tok, LANE), F32)]
        out_specs += [pl.BlockSpec((tm, LANE), row), pl.BlockSpec((tm, LANE), row)]
    in_specs.append(pl.BlockSpec(memory_space=pl.ANY))
    args.append(out)
    res = pl.pallas_call(
        functools.partial(_normmod_alias_kernel, route=route), grid=(n_tok // tm,),
        in_specs=in_specs, out_specs=tuple(out_specs), out_shape=tuple(out_shape),
        input_output_aliases={len(args) - 1: 0},
        compiler_params=_cparams(("arbitrary",)), name="normmod",
    )(*args)
    return res


def _normmod_alias_kernel(*refs, route):
    n_in = 5 if route else 4
    _normmod_kernel(*refs[:n_in], *refs[n_in + 1:], route=route)


GATHER_ROWS = 256
GATHER_UNROLL = 8


def _gather_kernel(idx_ref, src_ref, out_ref, sem):
    def row_copy(r, src_row):
        return pltpu.make_async_copy(src_ref.at[src_row], out_ref.at[r], sem)

    def start(r, c):
        row_copy(r, idx_ref[0, 0, r]).start()
        return c

    def wait(r, c):
        row_copy(r, 0).wait()
        return c

    lax.fori_loop(0, GATHER_ROWS, start, 0, unroll=GATHER_UNROLL)
    lax.fori_loop(0, GATHER_ROWS, wait, 0, unroll=GATHER_UNROLL)


def _gather_rows(src, idx):
    m = idx.shape[0]
    return pl.pallas_call(
        _gather_kernel, grid=(m // GATHER_ROWS,),
        in_specs=[pl.BlockSpec((1, 1, GATHER_ROWS), lambda i: (i, 0, 0), memory_space=pltpu.SMEM),
                  pl.BlockSpec(memory_space=pl.ANY)],
        out_specs=pl.BlockSpec((GATHER_ROWS,) + src.shape[1:], lambda i: (i, 0, 0)),
        out_shape=jax.ShapeDtypeStruct((m,) + src.shape[1:], src.dtype),
        scratch_shapes=[pltpu.SemaphoreType.DMA(())],
        compiler_params=pltpu.CompilerParams(dimension_semantics=("arbitrary",)), name="gather_rows",
    )(idx.reshape(m // GATHER_ROWS, 1, GATHER_ROWS), src)


def _swiglu_kernel(te_ref, nt_ref, x_ref, wg_ref, wu_ref, wd_ref, o_ref, acc_ref, xs_ref, *, precise):
    mm = functools.partial(_mm, precise=precise)
    i = pl.program_id(0)
    j = pl.program_id(1)

    @pl.when(i < nt_ref[0])
    def _():
        @pl.when(j == 0)
        def _():
            acc_ref[...] = jnp.zeros(acc_ref.shape, F32)
            xs_ref[...] = _load_rows(x_ref).astype(xs_ref.dtype)

        x = xs_ref[...]
        a = mm(x, wg_ref[0])
        u = mm(x, wu_ref[0])
        acc_ref[...] += mm(_silu(a) * u, wd_ref[0])

        @pl.when(j == pl.num_programs(1) - 1)
        def _():
            _store_rows(o_ref, acc_ref[...])

    @pl.when((i >= nt_ref[0]) & (j == pl.num_programs(1) - 1))
    def _():
        o_ref[...] = jnp.zeros(o_ref.shape, F32)


def _swiglu_grouped(x, tile_expert, n_tiles_used, wg, wu, wd, tm, tf, precise=False):
    m = x.shape[0]
    f = wg.shape[2]
    grid_spec = pltpu.PrefetchScalarGridSpec(
        num_scalar_prefetch=2, grid=(m // tm, f // tf),
        in_specs=[_block_rows(tm, x, lambda i, j, te, nt: (i, 0)),
                  pl.BlockSpec((1, D_MODEL, tf), lambda i, j, te, nt: (te[i], 0, j)),
                  pl.BlockSpec((1, D_MODEL, tf), lambda i, j, te, nt: (te[i], 0, j)),
                  pl.BlockSpec((1, tf, D_MODEL), lambda i, j, te, nt: (te[i], j, 0))],
        out_specs=_block_rows(tm, x, lambda i, j, te, nt: (i, 0)),
        scratch_shapes=[pltpu.VMEM((tm, D_MODEL), F32), pltpu.VMEM((tm, D_MODEL), F32 if precise else BF16)])
    return pl.pallas_call(
        functools.partial(_swiglu_kernel, precise=precise), grid_spec=grid_spec,
        out_shape=jax.ShapeDtypeStruct(x.shape, F32),
        compiler_params=_cparams(("arbitrary", "arbitrary")), name="swiglu",
    )(tile_expert, n_tiles_used, x, wg, wu, wd)


def _combine_kernel(*refs, moe, final):
    refs = list(refs)
    x_ref = refs.pop(0)
    gate_ref = refs.pop(0)
    y1_ref = refs.pop(0)
    if moe:
        y2_ref = refs.pop(0)
        gates_ref = refs.pop(0)
    fg_ref = refs.pop(0) if final else None
    o_ref = refs.pop(0)
    y = _load_rows(y1_ref)
    if moe:
        gts = gates_ref[...]
        y = gts[:, 0:1] * y + gts[:, 1:2] * _load_rows(y2_ref)
    out = x_ref[...] + gate_ref[0] * y
    if final:
        out = _rms(out) * fg_ref[...]
    o_ref[...] = out


def _combine(x2, gate, y, y_row0, y2_row0, gates, final_g, n_seq, t_len, tm):
    nblk = t_len // tm
    n_tok = n_seq * t_len
    moe = gates is not None
    final = final_g is not None

    def row(i):
        return (i, 0)

    in_specs = [pl.BlockSpec((tm, D_MODEL), row), pl.BlockSpec((1, gate.shape[1], D_MODEL), lambda i: (i // nblk, 0, 0)),
                _block_rows(tm, y, lambda i: (y_row0 // tm + i, 0))]
    args = [x2, gate, y]
    if moe:
        in_specs += [_block_rows(tm, y, lambda i: (y2_row0 // tm + i, 0)), pl.BlockSpec((tm, LANE), row)]
        args += [y, gates]
    if final:
        in_specs.append(pl.BlockSpec((1, D_MODEL), lambda i: (0, 0)))
        args.append(final_g)
    return pl.pallas_call(
        functools.partial(_combine_kernel, moe=moe, final=final), grid=(n_tok // tm,),
        in_specs=in_specs, out_specs=pl.BlockSpec((tm, D_MODEL), row),
        out_shape=jax.ShapeDtypeStruct((n_tok, D_MODEL), F32),
        compiler_params=_cparams(("arbitrary",)), name="combine",
    )(*args)


def _pack_mixer_weights(w_in, w_gate2, b_gate, fox_b_f, qng, kvng, w_uq, w_uk, w_uv, gla_norm_g):
    offs = np.concatenate([[0], np.cumsum(IN_SPLITS)])
    cols = {n: (int(offs[i]), int(offs[i + 1])) for i, n in enumerate(
        ("gq", "gk", "gv", "ag", "ar", "fq", "fk", "fv", "bf", "cq", "ckv", "kr"))}

    def seg(n):
        return w_in[:, cols[n][0]:cols[n][1]]

    half = ROPE_C // 2
    kr = seg("kr")

    def zcols(n):
        return jnp.zeros((D_MODEL, n), F32)

    small = jnp.concatenate([kr, seg("ar"), seg("bf"), zcols(SM_KRS - SM_BF - H_B), -kr[:, half:], kr[:, :half],
                             zcols(LANE - SM_KRS - ROPE_C)], axis=1)
    w = jnp.concatenate([seg("gq"), zcols(C_GK - QK_A), seg("gk"), zcols(C_GV - C_GK - QK_A), seg("gv"), seg("ag"),
                         seg("fq"), seg("fk"), seg("fv"), seg("cq"), seg("ckv"), small], axis=1)
    assert w.shape == (D_MODEL, N_PACK) and small.shape[1] == LANE
    wg2 = jnp.pad(w_gate2, ((SM_AR, LANE - SM_AR - GATE_RANK), (0, 2 * LANE - QK_A)))
    bg = jnp.pad(b_gate, (0, 2 * LANE - QK_A)).reshape(1, 2 * LANE)
    bf = jnp.pad(fox_b_f, (SM_BF, LANE - SM_BF - H_B)).reshape(1, LANE)
    uq = w_uq.reshape(Q_LORA, H_C, NOPE_C + ROPE_C)
    wuqn = uq[:, :, :NOPE_C].reshape(Q_LORA, H_C * NOPE_C)
    x1 = uq[:, :, NOPE_C:NOPE_C + half]
    x2 = uq[:, :, NOPE_C + half:]
    pad = jnp.zeros((Q_LORA, H_C, LANE - ROPE_C), F32)
    wr = jnp.concatenate([x1, x2, pad], axis=2).reshape(Q_LORA, H_C * LANE)
    wrs = jnp.concatenate([-x2, x1, pad], axis=2).reshape(Q_LORA, H_C * LANE)
    eye = jnp.eye(H_C, dtype=F32)
    wuk = (jnp.transpose(w_uk, (1, 2, 0))[:, :, None, :] * eye[:, None, :, None]).reshape(H_C * NOPE_C, H_C * KV_LORA)
    wuv = (jnp.transpose(w_uv, (1, 0, 2))[:, :, None, :] * eye[:, None, :, None]).reshape(H_C, KV_LORA, V_A)
    full = dict(w_in=w, wg2=wg2, bg=bg, bf=bf, qng=qng.reshape(1, Q_LORA), kvng=kvng.reshape(1, KV_LORA), wuqn=wuqn,
                wuk=wuk, wr=wr, wrs=wrs, wuv=wuv, gnorm=jnp.tile(gla_norm_g, H_A).reshape(1, V_A))
    half_prec = dict(full)
    for n in ("w_in", "wg2", "wuqn", "wuk", "wr", "wrs", "wuv"):
        half_prec[n] = full[n].astype(BF16)
    return half_prec, full


def _rope_tables(pos):
    half = ROPE_C // 2
    inv_freq = ROPE_BASE ** (-jnp.arange(half, dtype=F32) / half)
    ang = pos.astype(F32)[:, None] * inv_freq[None, :]
    n = pos.shape[0]
    cos = jnp.concatenate([jnp.cos(ang), jnp.cos(ang), jnp.ones((n, LANE - ROPE_C), F32)], axis=1)
    sin = jnp.concatenate([jnp.sin(ang), jnp.sin(ang), jnp.zeros((n, LANE - ROPE_C), F32)], axis=1)
    return cos, sin


def _state_to_t(s):
    b = s.shape[0]
    eye = jnp.eye(H_A, dtype=F32)
    s_vk = jnp.swapaxes(s, 2, 3)
    return (s_vk[:, :, :, None, :] * eye[None, :, None, :, None]).reshape(b, V_A, QK_A)


def _state_from_t(s_t):
    b = s_t.shape[0]
    blocks = s_t.reshape(b, H_A, DV_A, H_A, DK_A)
    diag = jnp.stack([blocks[:, hh, :, hh, :] for hh in range(H_A)], axis=1)
    return jnp.swapaxes(diag, 2, 3)


def _round_up(a, b):
    return (a + b - 1) // b * b


def kernel(x_prompt, x_sample, c_prompt, c_sample, cache_fox_k, cache_fox_v, cache_fox_logf, cache_mla_ckv, cache_mla_krope, state_gla, ada_w, ada_b, norm_mix_g, norm_ffn_g, w_in, gla_w_gate2, gla_b_gate, gla_norm_g, fox_b_f, mla_q_norm_g, mla_kv_norm_g, mla_w_uq, mla_w_uk, mla_w_uv, w_out, ffn_w_gate, ffn_w_up, ffn_w_down, moe_router, moe_w_gate, moe_w_up, moe_w_down, final_norm_g):
    bp, tp, _ = x_prompt.shape
    bs, ts, _ = x_sample.shape
    past = cache_fox_k.shape[2]
    np_tok, ns_tok = bp * tp, bs * ts
    n_all = np_tok + ns_tok

    tm_p, tm_s = 256, ns_tok
    tq_fox, tk = 512, 512

    nc = _round_up(bp + bs, 8)
    c_all = jnp.zeros((nc, D_MODEL), F32).at[:bp].set(c_prompt).at[bp:bp + bs].set(c_sample)
    mod = _ada(c_all, ada_w, ada_b)

    cos_p, sin_p = _rope_tables(jnp.arange(tp))
    cos_s, sin_s = _rope_tables(past + jnp.tile(jnp.arange(ts), bs))

    def seq_major(a):
        hh, d = a.shape[1], a.shape[3]
        return jnp.transpose(a.reshape(hh, bs, ts, d), (1, 0, 2, 3))

    def tok_major(a):
        hh, d = a.shape[1], a.shape[3]
        return jnp.transpose(a, (1, 0, 2, 3)).reshape(1, hh, bs * ts, d)

    cl = jnp.transpose(cache_fox_logf.astype(F32), (0, 1, 3, 2)).reshape(DEPTH * bs * H_B, past)
    f_cache, f_cache2 = _cumsum(cl, jnp.zeros((cl.shape[0], 1), F32), 512)
    f_cache = f_cache.reshape(DEPTH, bs * H_B, past)
    f_cache2 = f_cache2.reshape(DEPTH, bs, H_B, past // tk, tk)
    cache_k2d = cache_fox_k.astype(F32).reshape(DEPTH, bs, past, H_B * D_B)
    cache_v2d = cache_fox_v.astype(F32).reshape(DEPTH, bs, past, H_B * D_B)

    xp = x_prompt.reshape(np_tok, D_MODEL)
    xs = x_sample.reshape(ns_tok, D_MODEL)
    p_states = [[] for _ in range(6)]
    s_states = [[] for _ in range(6)]

    for l in range(DEPTH):
        mods = [mod[l, :, i * D_MODEL:(i + 1) * D_MODEL] for i in range(6)]
        mp = [m[:bp].reshape(bp, 1, D_MODEL) for m in mods]
        ms = [jnp.repeat(m[bp:bp + bs], ts, axis=0).reshape(1, ns_tok, D_MODEL) for m in mods]
        pw, pw32 = _pack_mixer_weights(w_in[l], gla_w_gate2[l], gla_b_gate[l], fox_b_f[l], mla_q_norm_g[l],
                                       mla_kv_norm_g[l], mla_w_uq[l], mla_w_uk[l], mla_w_uv[l], gla_norm_g[l])
        wo = w_out[l].astype(BF16)
        g_mix = norm_mix_g[l].reshape(1, D_MODEL)
        g_ffn = norm_ffn_g[l].reshape(1, D_MODEL)

        (gq, gk, gv, ag, gla, fk, fv, ckv, kc, small, qs, fqh, fkh, fvh) = _inproj(
            xp, mp[0], mp[1], g_mix, pw, cos_p, sin_p, bp, tp, tm_p, False, True)
        logf = small[:, SM_BF:SM_BF + H_B]
        krope = small[:, SM_KR:SM_KR + ROPE_C]
        f_rows = jnp.transpose(logf.reshape(bp, tp, H_B), (0, 2, 1)).reshape(bp * H_B, tp)
        f_cum = _cumsum(f_rows, jnp.zeros((bp * H_B, 1), F32), 512)[1]
        ya, s_t = _gla(gq, gk, gv, gla, ag, pw["gnorm"], jnp.zeros((bp, V_A, QK_A), F32), bp, tp, 512, CHUNK, 2, False)
        g_fox = bp * H_B
        o_fox = _flash(fqh.reshape(g_fox, tp // tq_fox, tq_fox, D_B), fkh.reshape(g_fox, tp, D_B),
                       fvh.reshape(g_fox, tp, 2 * D_B), f_cum.reshape(g_fox, tp // tq_fox, tq_fox, 1),
                       f_cum.reshape(g_fox, 1, tp), tq=tq_fox, rep=1, tk=tk, q0=0, mode="causal", kv_len=tp, dv=D_B, hp=2)
        o_mla = _flash(qs.reshape(bp, tp // tm_p, H_C * tm_p, 2 * LANE), kc.reshape(bp, tp, 2 * LANE), None, None, None,
                       tq=tm_p, rep=H_C, tk=tk, q0=0, mode="chunk", kv_len=tp, dv=KV_LORA)
        xp = _mixout(xp, mp[2], ya, o_fox.reshape(bp, H_B, tp, D_B),
                     o_mla.reshape(np_tok // tm_p, H_C, tm_p, KV_LORA), pw["wuv"], wo, bp, tp, tm_p, False)
        for i, st in enumerate((_state_from_t(s_t), fk.reshape(bp, tp, H_B, D_B), fv.reshape(bp, tp, H_B, D_B),
                                logf.reshape(bp, tp, H_B), ckv.reshape(bp, tp, KV_LORA), krope.reshape(bp, tp, ROPE_C))):
            p_states[i].append(st)

        (gq, gk, gv, ag, gla, fk, fv, ckv, kc, small, qs, fq) = _inproj(
            xs, ms[0], ms[1], g_mix, pw32, cos_s, sin_s, 1, ns_tok, tm_s, True, False)
        logf = small[:, SM_BF:SM_BF + H_B]
        krope = small[:, SM_KR:SM_KR + ROPE_C]
        g_fox = bs * H_B
        f_rows = jnp.transpose(logf.reshape(bs, ts, H_B), (0, 2, 1)).reshape(g_fox, ts)
        f_new = _cumsum(f_rows, f_cache[l][:, past - 1:past], ts)[1]
        fq_col = jnp.transpose(f_new.reshape(bs, H_B, ts), (0, 2, 1))
        ya, s_t = _gla(gq, gk, gv, gla, ag, pw["gnorm"], _state_to_t(state_gla[l].astype(F32)), bs, ts, ts, ts, 4, True)
        o_fox = _fox_decode(fq, cache_k2d, cache_v2d, l, fk, fv, fq_col, f_cache2, f_new, bs, ts, tk)
        o_mla = _mla_decode(seq_major(qs).reshape(bs, H_C * ts, 2 * LANE), cache_mla_ckv.astype(F32),
                            cache_mla_krope.astype(F32), l, ckv, kc, bs, ts, tk)
        xs = _mixout(xs, ms[2], ya, o_fox, tok_major(o_mla.reshape(bs, H_C, ts, KV_LORA)), pw32["wuv"], w_out[l],
                     1, ns_tok, tm_s, True)
        for i, st in enumerate((_state_from_t(s_t), fk.reshape(bs, ts, H_B, D_B), fv.reshape(bs, ts, H_B, D_B),
                                logf.reshape(bs, ts, H_B), ckv.reshape(bs, ts, KV_LORA), krope.reshape(bs, ts, ROPE_C))):
            s_states[i].append(st)

        last = l == DEPTH - 1
        fg = final_norm_g.reshape(1, D_MODEL) if last else None
        tm_f = 512
        if l % 2 == 0:
            j = l // 2
            h_p = _normmod(xp, mp[3], mp[4], g_ffn, None, jnp.zeros((np_tok, D_MODEL), F32), 0, bp, tp, tm_p)[0]
            h_s = _normmod(xs, ms[3], ms[4], g_ffn, None, jnp.zeros((ns_tok, D_MODEL), F32), 0, 1, ns_tok, tm_s)[0]
            n_tiles = np_tok // tm_f
            y_p = _swiglu_grouped(h_p, jnp.zeros((n_tiles,), jnp.int32), jnp.full((1,), n_tiles, jnp.int32),
                                  ffn_w_gate[j:j + 1].astype(BF16), ffn_w_up[j:j + 1].astype(BF16),
                                  ffn_w_down[j:j + 1].astype(BF16), tm_f, 1408)
            y_s = _swiglu_grouped(h_s, jnp.zeros((1,), jnp.int32), jnp.ones((1,), jnp.int32),
                                  ffn_w_gate[j:j + 1], ffn_w_up[j:j + 1], ffn_w_down[j:j + 1], tm_s, 1408, precise=True)
            xp = _combine(xp, mp[5], y_p, 0, 0, None, fg, bp, tp, tm_p)
            xs = _combine(xs, ms[5], y_s, 0, 0, None, fg, 1, ns_tok, tm_s)
        else:
            j = l // 2
            router_pad = jnp.zeros((D_MODEL, LANE), F32).at[:, :N_EXPERTS].set(moe_router[j])
            h_all = jnp.zeros((n_all,) + TOK_TILE, F32)
            h_all, ids_p, gates_p = _normmod(xp, mp[3], mp[4], g_ffn, router_pad, h_all, 0, bp, tp, tm_p)
            h_all, ids_s, gates_s = _normmod(xs, ms[3], ms[4], g_ffn, router_pad, h_all, np_tok, 1, ns_tok, tm_s)
            ids = jnp.concatenate([ids_p[:, :2], ids_s[:, :2]], axis=0)
            e = jnp.transpose(ids).reshape(-1)
            onehot = (e[:, None] == jnp.arange(N_EXPERTS)[None, :]).astype(jnp.int32)
            rank = jnp.sum((jnp.cumsum(onehot, axis=0) - onehot) * onehot, axis=1)
            counts = jnp.sum(onehot, axis=0)
            padded = (counts + tm_f - 1) // tm_f * tm_f
            ends = jnp.cumsum(padded)
            starts = ends - padded
            pos = starts[e] + rank
            m_pad = _round_up(2 * n_all + N_EXPERTS * (tm_f - 1), tm_f)
            token = jnp.tile(jnp.arange(n_all, dtype=jnp.int32), 2)
            src = jnp.zeros((m_pad,), jnp.int32).at[pos].set(token)
            n_tiles = m_pad // tm_f
            tile_row0 = jnp.arange(n_tiles, dtype=jnp.int32) * tm_f
            tile_expert = jnp.minimum(jnp.sum((ends[None, :] <= tile_row0[:, None]).astype(jnp.int32), axis=1),
                                      N_EXPERTS - 1)
            n_used = (ends[-1] // tm_f).astype(jnp.int32).reshape(1)
            x_sorted = _gather_rows(h_all, src)
            y = _swiglu_grouped(x_sorted, tile_expert, n_used, moe_w_gate[j].astype(BF16), moe_w_up[j].astype(BF16),
                                moe_w_down[j].astype(BF16), tm_f, 1792)
            n_back = _round_up(n_all, GATHER_ROWS)
            back = jnp.zeros((2 * n_back,), jnp.int32).at[:n_all].set(pos[:n_all]).at[n_back:n_back + n_all].set(pos[n_all:])
            yg = _gather_rows(y, back)
            xp = _combine(xp, mp[5], yg, 0, n_back, gates_p, fg, bp, tp, tm_p)
            xs = _combine(xs, ms[5], yg, np_tok, n_back + np_tok, gates_s, fg, 1, ns_tok, tm_s)

    outs_p = [jnp.stack(s, axis=0) for s in p_states]
    outs_s = [jnp.stack(s, axis=0) for s in s_states]
    return (xp.reshape(bp, tp, D_MODEL), xs.reshape(bs, ts, D_MODEL), *outs_p, *outs_s)
```

```python
import functools

import numpy as np
import jax
import jax.numpy as jnp
from jax import lax
from jax.experimental import pallas as pl
from jax.experimental.pallas import tpu as pltpu

F32 = jnp.float32
BF16 = jnp.bfloat16
HIGHEST = lax.Precision.HIGHEST

D_MODEL = 1024
DEPTH = 2
CHUNK = 64
EPS = 1e-6
H_A, DK_A, DV_A = 6, 32, 64
GATE_RANK = 16
GATE_TAU = 16.0
H_B, D_B = 4, 64
H_C, NOPE_C, ROPE_C, V_C = 6, 64, 32, 64
Q_LORA, KV_LORA = 256, 128
ROPE_BASE = 10000.0
N_EXPERTS = 8
IN_SPLITS = (H_A * DK_A, H_A * DK_A, H_A * DV_A, H_A * DV_A, GATE_RANK,
             H_B * D_B, H_B * D_B, H_B * D_B, H_B, Q_LORA, KV_LORA, ROPE_C)

QK_A = H_A * DK_A
V_A = H_A * DV_A
QKV_B = H_B * D_B
LANE = 128
NEG = -1e30

C_GQ, C_GK, C_GV, C_AG = 0, 256, 512, 896
C_FQ, C_FK, C_FV = 1280, 1536, 1792
C_CQ, C_CKV, C_SM = 2048, 2304, 2432
N_PACK = 2560
SM_KR, SM_AR, SM_BF, SM_KRS = 0, 32, 48, 64

GLA_SUB = 16
FLASH_ROW_GROUP = 256
SWIGLU_COLS = 256
LOG2E = float(np.log2(np.e))
VMEM_LIMIT = 56 * 1024 * 1024


def _cparams(sem):
    return pltpu.CompilerParams(dimension_semantics=sem, vmem_limit_bytes=VMEM_LIMIT)


def _log_sigmoid(z):
    return jnp.minimum(z, 0.0) - jnp.log1p(jnp.exp(-jnp.abs(z)))


def _silu(z):
    return z * (1.0 / (1.0 + jnp.exp(-z)))


def _rms(x):
    return x * lax.rsqrt(jnp.mean(x * x, axis=-1, keepdims=True) + EPS)


def _dot(a, b):
    return jnp.dot(a, b, preferred_element_type=F32)


def _dot_nt(a, b):
    return lax.dot_general(a, b, (((1,), (1,)), ((), ())), preferred_element_type=F32)


def _dot_tn(a, b):
    return lax.dot_general(a, b, (((0,), (0,)), ((), ())), preferred_element_type=F32)


def _mm(a, b, dims=(((1,), (0,)), ((), ())), *, precise):
    if precise:
        return lax.dot_general(a.astype(F32), b.astype(F32), dims, precision=HIGHEST, preferred_element_type=F32)
    return lax.dot_general(a.astype(BF16), b.astype(BF16), dims, preferred_element_type=F32)


_NT = (((1,), (1,)), ((), ()))
_TN = (((0,), (0,)), ((), ()))

SUBLANE = 8
TOK_TILE = (D_MODEL // LANE, LANE)
assert TOK_TILE[0] == SUBLANE


def _block_rows(tm, arr, index_map):
    if arr.ndim == 3:
        return pl.BlockSpec((tm,) + TOK_TILE, lambda *a: index_map(*a) + (0,))
    return pl.BlockSpec((tm, D_MODEL), index_map)


def _load_rows(ref):
    if len(ref.shape) == 3:
        return jnp.concatenate([ref[:, s, :] for s in range(TOK_TILE[0])], axis=-1)
    return ref[...]


def _store_rows(ref, val):
    if len(ref.shape) == 3:
        for s in range(TOK_TILE[0]):
            ref[:, s, :] = val[:, s * LANE:(s + 1) * LANE]
    else:
        ref[...] = val


def _ada_kernel(c_ref, w_ref, b_ref, o_ref):
    s = _silu(c_ref[...])
    o_ref[0] = jnp.dot(s, w_ref[0], precision=HIGHEST, preferred_element_type=F32) + b_ref[0]


def _ada(c_all, ada_w, ada_b):
    nc = c_all.shape[0]
    tn = 1536
    return pl.pallas_call(
        _ada_kernel,
        grid=(DEPTH, 6 * D_MODEL // tn),
        in_specs=[pl.BlockSpec((nc, D_MODEL), lambda l, j: (0, 0)),
                  pl.BlockSpec((1, D_MODEL, tn), lambda l, j: (l, 0, j)),
                  pl.BlockSpec((1, 1, tn), lambda l, j: (l, 0, j))],
        out_specs=pl.BlockSpec((1, nc, tn), lambda l, j: (l, 0, j)),
        out_shape=jax.ShapeDtypeStruct((DEPTH, nc, 6 * D_MODEL), F32),
        compiler_params=_cparams(("arbitrary", "arbitrary")),
        name="ada",
    )(c_all, ada_w, ada_b.reshape(DEPTH, 1, 6 * D_MODEL))


def _inproj_kernel(x_ref, sh_ref, sc_ref, g_ref, w_ref, wg2_ref, bg_ref, bf_ref, qng_ref, kvng_ref,
                   wuqn_ref, wuk_ref, wr_ref, wrs_ref, cos_ref, sin_ref,
                   gq_ref, gk_ref, gv_ref, ag_ref, gla_ref, fk_ref, fv_ref,
                   ckv_ref, kc_ref, small_ref, qs_ref, *fox_refs, precise):
    mm = functools.partial(_mm, precise=precise)
    act = kc_ref.dtype
    x = x_ref[...]
    h = (_rms(x) * g_ref[...]) * (1.0 + sc_ref[0]) + sh_ref[0]
    p = mm(h, w_ref[...])

    gq_ref[...] = p[:, C_GQ:C_GQ + QK_A] * (DK_A ** -0.5)
    gk_ref[...] = p[:, C_GK:C_GK + QK_A]
    gv_ref[...] = p[:, C_GV:C_GV + V_A]
    ag_ref[...] = p[:, C_AG:C_AG + V_A]
    sm = p[:, C_SM:C_SM + LANE]
    z = mm(sm, wg2_ref[...]) + bg_ref[...]
    gla_ref[...] = _log_sigmoid(z[:, :QK_A]) * (1.0 / GATE_TAU)

    fq = p[:, C_FQ:C_FQ + QKV_B] * (D_B ** -0.5 * LOG2E)
    fk = p[:, C_FK:C_FK + QKV_B]
    fv = p[:, C_FV:C_FV + QKV_B]
    fk_ref[...] = fk
    fv_ref[...] = fv
    if len(fox_refs) == 1:
        fox_refs[0][...] = fq
    else:
        fqh_ref, fkh_ref, fvh_ref = fox_refs
        for hh in range(H_B):
            sl = slice(hh * D_B, (hh + 1) * D_B)
            fqh_ref[0, hh] = fq[:, sl].astype(act)
            fkh_ref[0, hh] = fk[:, sl].astype(act)
            fvh_ref[0, hh, :, :D_B] = fv[:, sl].astype(act)
            fvh_ref[0, hh, :, D_B:] = jnp.ones((fv.shape[0], D_B), act)

    cos = cos_ref[...]
    sin = sin_ref[...]
    lane = lax.broadcasted_iota(jnp.int32, sm.shape, 1)
    kr = sm * cos + pltpu.roll(sm, LANE - SM_KRS, 1) * sin
    logf = _log_sigmoid(sm + bf_ref[...])
    small_ref[...] = jnp.where((lane >= SM_BF) & (lane < SM_BF + H_B), logf, kr)

    ckv = _rms(p[:, C_CKV:C_CKV + KV_LORA]) * kvng_ref[...]
    ckv_ref[...] = ckv
    kc_ref[:, :KV_LORA] = ckv.astype(act)
    kc_ref[:, KV_LORA:] = jnp.where(lane < ROPE_C, kr, 0.0).astype(act)
    cqn = _rms(p[:, C_CQ:C_CQ + Q_LORA]) * qng_ref[...]
    nope = mm(cqn, wuqn_ref[...])
    qlat = mm(nope, wuk_ref[...])
    qa = mm(cqn, wr_ref[...])
    qb = mm(cqn, wrs_ref[...])
    scale = (NOPE_C + ROPE_C) ** -0.5 * LOG2E
    for hh in range(H_C):
        sl = slice(hh * LANE, (hh + 1) * LANE)
        qs_ref[0, hh, :, :KV_LORA] = (qlat[:, sl] * scale).astype(act)
        qs_ref[0, hh, :, KV_LORA:] = ((qa[:, sl] * cos + qb[:, sl] * sin) * scale).astype(act)


def _inproj(x2, shift, scale, g, pw, cos_tab, sin_tab, n_seq, t_len, tm, precise, fox_head_major):
    n_tok = n_seq * t_len
    nblk = t_len // tm
    act = F32 if precise else BF16
    mod_rows = shift.shape[1]

    def row(i):
        return (i, 0)

    def seq(i):
        return (i // nblk, 0, 0)

    def const2(i):
        return (0, 0)

    def tab(i):
        return (i % nblk, 0)

    def headmajor(i):
        return (i // nblk, 0, i % nblk, 0)

    sds = jax.ShapeDtypeStruct
    out_shape = (
        sds((n_tok, QK_A), F32), sds((n_tok, QK_A), F32), sds((n_tok, V_A), F32), sds((n_tok, V_A), F32),
        sds((n_tok, QK_A), F32),
        sds((n_tok, QKV_B), F32), sds((n_tok, QKV_B), F32),
        sds((n_tok, KV_LORA), F32), sds((n_tok, 2 * LANE), act), sds((n_tok, LANE), F32),
        sds((n_tok // tm, H_C, tm, 2 * LANE), act),
    )
    out_specs = (
        pl.BlockSpec((tm, QK_A), row), pl.BlockSpec((tm, QK_A), row), pl.BlockSpec((tm, V_A), row),
        pl.BlockSpec((tm, V_A), row), pl.BlockSpec((tm, QK_A), row),
        pl.BlockSpec((tm, QKV_B), row), pl.BlockSpec((tm, QKV_B), row),
        pl.BlockSpec((tm, KV_LORA), row), pl.BlockSpec((tm, 2 * LANE), row), pl.BlockSpec((tm, LANE), row),
        pl.BlockSpec((1, H_C, tm, 2 * LANE), lambda i: (i, 0, 0, 0)),
    )
    if fox_head_major:
        out_shape += (sds((n_seq, H_B, t_len, D_B), act), sds((n_seq, H_B, t_len, D_B), act),
                      sds((n_seq, H_B, t_len, 2 * D_B), act))
        out_specs += (pl.BlockSpec((1, H_B, tm, D_B), headmajor), pl.BlockSpec((1, H_B, tm, D_B), headmajor),
                      pl.BlockSpec((1, H_B, tm, 2 * D_B), headmajor))
    else:
        out_shape += (sds((n_tok, QKV_B), F32),)
        out_specs += (pl.BlockSpec((tm, QKV_B), row),)
    in_specs = [
        pl.BlockSpec((tm, D_MODEL), row), pl.BlockSpec((1, mod_rows, D_MODEL), seq),
        pl.BlockSpec((1, mod_rows, D_MODEL), seq), pl.BlockSpec((1, D_MODEL), const2),
        pl.BlockSpec((D_MODEL, N_PACK), const2), pl.BlockSpec((LANE, 2 * LANE), const2),
        pl.BlockSpec((1, 2 * LANE), const2), pl.BlockSpec((1, LANE), const2),
        pl.BlockSpec((1, Q_LORA), const2), pl.BlockSpec((1, KV_LORA), const2),
        pl.BlockSpec((Q_LORA, H_C * NOPE_C), const2), pl.BlockSpec((H_C * NOPE_C, H_C * KV_LORA), const2),
        pl.BlockSpec((Q_LORA, H_C * LANE), const2), pl.BlockSpec((Q_LORA, H_C * LANE), const2),
        pl.BlockSpec((tm, LANE), tab), pl.BlockSpec((tm, LANE), tab),
    ]
    return pl.pallas_call(
        functools.partial(_inproj_kernel, precise=precise), grid=(n_tok // tm,), in_specs=in_specs,
        out_specs=out_specs, out_shape=out_shape, compiler_params=_cparams(("arbitrary",)), name="inproj",
    )(x2, shift, scale, g, pw["w_in"], pw["wg2"], pw["bg"], pw["bf"], pw["qng"], pw["kvng"],
      pw["wuqn"], pw["wuk"], pw["wr"], pw["wrs"], cos_tab, sin_tab)


def _cumsum_kernel(x_ref, init_ref, o_ref, o2_ref, *, tb):
    n = x_ref.shape[1]
    upper = (lax.broadcasted_iota(jnp.int32, (tb, tb), 0) <= lax.broadcasted_iota(jnp.int32, (tb, tb), 1)).astype(F32)
    carry = init_ref[...]
    for j in range(n // tb):
        blk = jnp.dot(x_ref[:, j * tb:(j + 1) * tb], upper, precision=HIGHEST, preferred_element_type=F32) + carry
        o_ref[:, j * tb:(j + 1) * tb] = blk
        o2_ref[:, j * tb:(j + 1) * tb] = blk * LOG2E
        carry = blk[:, tb - 1:tb]


def _cumsum(x, init, tb):
    return pl.pallas_call(
        functools.partial(_cumsum_kernel, tb=tb),
        out_shape=(jax.ShapeDtypeStruct(x.shape, F32), jax.ShapeDtypeStruct(x.shape, F32)), name="cumsum",
    )(x, init)


def _fox_prep_kernel(sm_ref, q_ref, k_ref, qa_ref, ka_ref, carry_ref, *, tb):
    @pl.when(pl.program_id(1) == 0)
    def _():
        carry_ref[...] = jnp.zeros(carry_ref.shape, F32)

    tril = (lax.broadcasted_iota(jnp.int32, (tb, tb), 0) >= lax.broadcasted_iota(jnp.int32, (tb, tb), 1)).astype(BF16)
    x = sm_ref[...]
    x_hi = x.astype(BF16)
    x_mid = (x - x_hi.astype(F32)).astype(BF16)
    x_lo = (x - x_hi.astype(F32) - x_mid.astype(F32)).astype(BF16)
    cum = _dot(tril, x_hi) + _dot(tril, x_mid) + _dot(tril, x_lo) + carry_ref[...]
    carry_ref[...] = cum[tb - 1:tb]
    lane = lax.broadcasted_iota(jnp.int32, (tb, D_B), 1)
    for hh in range(H_B):
        f = jnp.broadcast_to(cum[:, SM_BF + hh:SM_BF + hh + 1] * LOG2E, (tb, D_B))
        hi = f.astype(BF16).astype(F32)
        mid = (f - hi).astype(BF16).astype(F32)
        lo = (f - hi - mid).astype(BF16).astype(F32)
        terms = jnp.where(lane % 3 == 0, hi, jnp.where(lane % 3 == 1, mid, lo))
        q_extra = jnp.where(lane < 3, terms, jnp.where(lane < 6, 1.0, 0.0))
        k_extra = jnp.where(lane < 3, 1.0, jnp.where(lane < 6, -terms, 0.0))
        qa_ref[0, hh, :, :D_B] = q_ref[0, hh]
        qa_ref[0, hh, :, D_B:] = q_extra.astype(BF16)
        ka_ref[0, hh, :, :D_B] = k_ref[0, hh]
        ka_ref[0, hh, :, D_B:] = k_extra.astype(BF16)


def _fox_prep(small, fqh, fkh, n_seq, t_len, tb):
    nblk = t_len // tb

    def hm(b, j):
        return (b, 0, j, 0)

    return pl.pallas_call(
        functools.partial(_fox_prep_kernel, tb=tb), grid=(n_seq, nblk),
        in_specs=[pl.BlockSpec((tb, LANE), lambda b, j: (b * nblk + j, 0)),
                  pl.BlockSpec((1, H_B, tb, D_B), hm), pl.BlockSpec((1, H_B, tb, D_B), hm)],
        out_specs=(pl.BlockSpec((1, H_B, tb, 2 * D_B), hm), pl.BlockSpec((1, H_B, tb, 2 * D_B), hm)),
        out_shape=(jax.ShapeDtypeStruct((n_seq, H_B, t_len, 2 * D_B), BF16),) * 2,
        scratch_shapes=[pltpu.VMEM((1, LANE), F32)],
        compiler_params=_cparams(("arbitrary", "arbitrary")), name="fox_prep",
    )(small, fqh, fkh)


def _gla_kernel(q_ref, k_ref, v_ref, la_ref, ag_ref, gn_ref, s0_ref, y_ref, sout_ref, s_ref, *, chunk, n_chunks,
                precise):
    mm = functools.partial(_mm, precise=precise)
    c = chunk
    nsub = c // GLA_SUB

    nb = q_ref.shape[0]

    @pl.when(pl.program_id(1) == 0)
    def _():
        s_ref[...] = s0_ref[...]

    lane_qk = lax.broadcasted_iota(jnp.int32, (GLA_SUB, QK_A), 1) // DK_A
    lane_v = lax.broadcasted_iota(jnp.int32, (GLA_SUB, V_A), 1) // DV_A
    bd = (lax.broadcasted_iota(jnp.int32, (V_A, QK_A), 0) // DV_A) == (lax.broadcasted_iota(jnp.int32, (V_A, QK_A), 1) // DK_A)
    tril = (lax.broadcasted_iota(jnp.int32, (c, c), 0) >= lax.broadcasted_iota(jnp.int32, (c, c), 1)).astype(F32)
    hm = (lax.broadcasted_iota(jnp.int32, (V_A, V_A), 0) // DV_A) == (lax.broadcasted_iota(jnp.int32, (V_A, V_A), 1) // DV_A)
    head_mean = jnp.where(hm, 1.0 / DV_A, 0.0).astype(F32)

    def cumsum_rows(la):
        if precise:
            return jnp.dot(tril, la, precision=HIGHEST, preferred_element_type=F32)
        hi = la.astype(BF16)
        lo = (la - hi.astype(F32)).astype(BF16)
        tb16 = tril.astype(BF16)
        return _dot(tb16, hi) + _dot(tb16, lo)

    def chunk_step(bb, r):
        q = q_ref[bb, pl.ds(r, c), :]
        k = k_ref[bb, pl.ds(r, c), :]
        v = v_ref[bb, pl.ds(r, c), :]
        la = la_ref[bb, pl.ds(r, c), :]
        b = cumsum_rows(la)
        s_t = s_ref[bb]
        vb = v if precise else v.astype(BF16)
        o_inter = mm(q * jnp.exp(b), s_t, _NT)
        outs = []
        for i in range(nsub):
            r0 = i * GLA_SUB
            r1 = r0 + GLA_SUB
            bi = b[r0 - 1:r0] if i > 0 else jnp.zeros((1, QK_A), F32)
            qi = q[r0:r1] * jnp.exp(b[r0:r1] - bi)
            kk = k[:r1] * jnp.exp(bi - b[:r1])
            qst = jnp.concatenate([jnp.where(lane_qk == hh, qi, 0.0) for hh in range(H_A)], axis=0)
            att = mm(qst, kk, _NT)
            t_idx = r0 + lax.broadcasted_iota(jnp.int32, att.shape, 0) % GLA_SUB
            s_idx = lax.broadcasted_iota(jnp.int32, att.shape, 1)
            att = jnp.where(s_idx <= t_idx, att, 0.0)
            oi = mm(att, vb[:r1])
            o = jnp.zeros((GLA_SUB, V_A), F32)
            for hh in range(H_A):
                o = o + jnp.where(lane_v == hh, oi[hh * GLA_SUB:(hh + 1) * GLA_SUB], 0.0)
            outs.append(o)
        o = jnp.concatenate(outs, axis=0) + o_inter if nsub > 1 else outs[0] + o_inter
        b_last = b[c - 1:c]
        kd = k * jnp.exp(b_last - b)
        s_ref[bb] = s_t * jnp.exp(b_last) + jnp.where(bd, mm(vb, kd, _TN), 0.0)
        ms = mm(o * o, head_mean)
        y = o * lax.rsqrt(ms + EPS) * gn_ref[...] * _silu(ag_ref[bb, pl.ds(r, c), :])
        y_ref[bb, pl.ds(r, c), :] = y.astype(y_ref.dtype)

    def body(ci, carry):
        r = pl.multiple_of(ci * c, c)
        for bb in range(nb):
            chunk_step(bb, r)
        return carry

    lax.fori_loop(0, n_chunks, body, 0)

    @pl.when(pl.program_id(1) == pl.num_programs(1) - 1)
    def _():
        sout_ref[...] = s_ref[...]


def _gla(gq, gk, gv, gla, ag, gnorm, s0_t, n_seq, t_len, tb, chunk, nb, precise):
    nblk = t_len // tb

    def row(b, j):
        return (b, j, 0)

    def st(b, j):
        return (b, 0, 0)

    def seq3(a):
        return a.reshape(n_seq, t_len, a.shape[-1])

    ya, s_t = pl.pallas_call(
        functools.partial(_gla_kernel, chunk=chunk, n_chunks=tb // chunk, precise=precise),
        grid=(n_seq // nb, nblk),
        in_specs=[pl.BlockSpec((nb, tb, QK_A), row), pl.BlockSpec((nb, tb, QK_A), row), pl.BlockSpec((nb, tb, V_A), row),
                  pl.BlockSpec((nb, tb, QK_A), row), pl.BlockSpec((nb, tb, V_A), row),
                  pl.BlockSpec((1, V_A), lambda b, j: (0, 0)), pl.BlockSpec((nb, V_A, QK_A), st)],
        out_specs=(pl.BlockSpec((nb, tb, V_A), row), pl.BlockSpec((nb, V_A, QK_A), st)),
        out_shape=(jax.ShapeDtypeStruct((n_seq, t_len, V_A), F32 if precise else BF16),
                   jax.ShapeDtypeStruct((n_seq, V_A, QK_A), F32)),
        scratch_shapes=[pltpu.VMEM((nb, V_A, QK_A), F32)],
        compiler_params=_cparams(("arbitrary", "arbitrary")), name="gla",
    )(seq3(gq), seq3(gk), seq3(gv), seq3(gla), seq3(ag), gnorm, s0_t)
    return ya.reshape(n_seq * t_len, V_A), s_t


def _flash_kernel(*refs, tq, rep, hp, tk, q0, mode, kv_len, dv, bias, v_from_k, precise):
    mm = functools.partial(_mm, precise=precise)
    refs = list(refs)
    q_ref = refs.pop(0)
    k_ref = refs.pop(0)
    v_ref = k_ref if v_from_k else refs.pop(0)
    fq_ref = refs.pop(0) if bias else None
    fk_ref = refs.pop(0) if bias else None
    o_ref, m_ref, acc_ref = refs
    i = pl.program_id(1)
    rows = rep * tq
    m_ref[...] = jnp.full(m_ref.shape, NEG, F32)
    acc_ref[...] = jnp.zeros(acc_ref.shape, F32)

    first_q = q0 + i * tq
    last_q = first_q + tq - 1
    if mode == "chunk":
        vis_all = (first_q // CHUNK) * CHUNK + CHUNK - 1
        vis_any = (last_q // CHUNK) * CHUNK + CHUNK - 1
    else:
        vis_all = first_q
        vis_any = last_q
    vis_all = jnp.minimum(vis_all, kv_len - 1)
    vis_any = jnp.minimum(vis_any, kv_len - 1)
    n_full = (vis_all + 1) // tk
    n_any = vis_any // tk + 1

    grp = min(rows, FLASH_ROW_GROUP)

    def step(jb, masked):
        for hh in range(hp):
            head_step(hh, jb, masked)

    def head_step(hh, jb, masked):
        k_start = pl.multiple_of(jb * tk, tk)
        k = k_ref[hh, pl.ds(k_start, tk), :]
        if v_from_k:
            v = jnp.where(lax.broadcasted_iota(jnp.int32, k.shape, 1) < dv, k, jnp.ones_like(k))
        else:
            v = v_ref[hh, pl.ds(k_start, tk), :]
        for g in range(rows // grp):
            rs = slice(g * grp, (g + 1) * grp)
            s = mm(q_ref[hh, 0, rs, :], k, _NT)
            if bias:
                s = s + fq_ref[hh, 0, rs, :] - fk_ref[hh, jb]
            rs = slice(hh * rows + g * grp, hh * rows + (g + 1) * grp)
            if masked:
                qpos = first_q + (g * grp + lax.broadcasted_iota(jnp.int32, (grp, tk), 0)) % tq
                kpos = k_start + lax.broadcasted_iota(jnp.int32, (grp, tk), 1)
                if mode == "chunk":
                    ok = (kpos // CHUNK) <= (qpos // CHUNK)
                else:
                    ok = kpos <= qpos
                ok = ok & (kpos < kv_len)
                s = jnp.where(ok, s, NEG)
            chunks = [s[:, c * LANE:(c + 1) * LANE] for c in range(tk // LANE)]
            smax = chunks[0]
            for ch in chunks[1:]:
                smax = jnp.maximum(smax, ch)
            m_prev = m_ref[rs, :]
            m_new = jnp.maximum(m_prev, jnp.max(smax, axis=-1, keepdims=True))
            alpha = jnp.exp2(m_prev - m_new)
            p = jnp.concatenate([jnp.exp2(ch - m_new) for ch in chunks], axis=1)
            acc = acc_ref[rs, :]
            alpha_w = alpha if acc.shape[1] == LANE else jnp.concatenate([alpha] * (acc.shape[1] // LANE), axis=1)
            acc_ref[rs, :] = alpha_w * acc + mm(p, v)
            m_ref[rs, :] = m_new

    def loop(lo, hi, masked):
        def body(jb, carry):
            step(jb, masked)
            return carry
        lax.fori_loop(lo, hi, body, 0)

    loop(0, n_full, False)
    loop(n_full, n_any, True)

    acc = acc_ref[...]
    if dv == LANE:
        out = acc[:, :dv] / acc[:, dv:]
    else:
        out = (acc / pltpu.roll(acc, LANE - dv, 1))[:, :dv]
    for hh in range(hp):
        o_ref[hh, 0] = out[hh * rows:(hh + 1) * rows].astype(o_ref.dtype)


def _flash(q, k, v, fq, fk, *, tq, rep, tk, q0, mode, kv_len, dv, hp=1, precise=False):
    g, nq, rows, dqk = q.shape
    t_k = k.shape[1]
    nk = t_k // tk
    bias = fq is not None
    v_from_k = v is None

    def qmap(b, i):
        return (b, i, 0, 0)

    def kmap(b, i):
        return (b, 0, 0)

    in_specs = [pl.BlockSpec((hp, 1, rows, dqk), qmap), pl.BlockSpec((hp, t_k, dqk), kmap)]
    args = [q, k]
    dva = dqk if v_from_k else v.shape[2]
    assert dva % LANE == 0 and dva > dv
    if not v_from_k:
        in_specs.append(pl.BlockSpec((hp, t_k, dva), kmap))
        args.append(v)
    if bias:
        in_specs += [pl.BlockSpec((hp, 1, rows, 1), qmap), pl.BlockSpec((hp, nk, 1, tk), lambda b, i: (b, 0, 0, 0))]
        args += [fq, fk.reshape(g, nk, 1, tk)]
    return pl.pallas_call(
        functools.partial(_flash_kernel, tq=tq, rep=rep, hp=hp, tk=tk, q0=q0, mode=mode, kv_len=kv_len, dv=dv,
                          bias=bias, v_from_k=v_from_k, precise=precise),
        grid=(g // hp, nq), in_specs=in_specs,
        out_specs=pl.BlockSpec((hp, 1, rows, dv), qmap),
        out_shape=jax.ShapeDtypeStruct((g, nq, rows, dv), F32 if precise else BF16),
        scratch_shapes=[pltpu.VMEM((hp * rows, LANE), F32), pltpu.VMEM((hp * rows, dva), F32)],
        compiler_params=_cparams(("arbitrary", "arbitrary")), name="flash_" + mode,
    )(*args)


def _hdot(a, b, dims=(((1,), (0,)), ((), ()))):
    return lax.dot_general(a, b, dims, precision=HIGHEST, preferred_element_type=F32)


def _online_softmax_step(s, v, m_ref, l_ref, acc_ref):
    m_prev = m_ref[...]
    m_new = jnp.maximum(m_prev, jnp.max(s, axis=-1, keepdims=True))
    alpha = jnp.exp2(m_prev - m_new)
    p = jnp.exp2(s - m_new)
    l_ref[...] = alpha * l_ref[...] + jnp.sum(p, axis=-1, keepdims=True)
    acc_ref[...] = alpha * acc_ref[...] + _hdot(p, v)
    m_ref[...] = m_new


def _fox_decode_kernel(q_ref, kc_ref, vc_ref, kn_ref, vn_ref, fq_ref, fkc_ref, fkn_ref, o_ref, m_ref, l_ref, acc_ref,
                       *, tk, past, ts):
    rows = H_B * ts
    q = q_ref[...]
    lane_head = lax.broadcasted_iota(jnp.int32, q.shape, 1) // D_B
    q_st = jnp.concatenate([jnp.where(lane_head == hh, q, 0.0) for hh in range(H_B)], axis=0)
    fq = fq_ref[0]
    fq_st = jnp.concatenate([fq[:, hh:hh + 1] for hh in range(H_B)], axis=0)
    m_ref[...] = jnp.full(m_ref.shape, NEG, F32)
    l_ref[...] = jnp.zeros(l_ref.shape, F32)
    acc_ref[...] = jnp.zeros(acc_ref.shape, F32)

    def cached(jb, carry):
        k0 = pl.multiple_of(jb * tk, tk)
        fk_st = jnp.concatenate([jnp.broadcast_to(fkc_ref[0, 0, hh, pl.ds(jb, 1), :], (ts, tk)) for hh in range(H_B)],
                                axis=0)
        s = _hdot(q_st, kc_ref[0, 0, pl.ds(k0, tk), :], _NT) + fq_st - fk_st
        _online_softmax_step(s, vc_ref[0, 0, pl.ds(k0, tk), :], m_ref, l_ref, acc_ref)
        return carry

    lax.fori_loop(0, past // tk, cached, 0)
    fkn = fkn_ref[0]
    fk_st = jnp.concatenate([jnp.broadcast_to(fkn[hh:hh + 1, :], (ts, ts)) for hh in range(H_B)], axis=0)
    s = _hdot(q_st, kn_ref[...], _NT) + fq_st - fk_st
    t_idx = lax.broadcasted_iota(jnp.int32, s.shape, 0) % ts
    s_idx = lax.broadcasted_iota(jnp.int32, s.shape, 1)
    _online_softmax_step(jnp.where(s_idx <= t_idx, s, NEG), vn_ref[...], m_ref, l_ref, acc_ref)
    res = acc_ref[...] / l_ref[...]
    out = jnp.zeros((ts, H_B * D_B), F32)
    for hh in range(H_B):
        out = out + jnp.where(lane_head == hh, res[hh * ts:(hh + 1) * ts], 0.0)
    o_ref[...] = out


def _fox_decode(q, cache_k, cache_v, layer, k_new, v_new, fq_col, f_cache, f_new, n_seq, ts, tk):
    past = cache_k.shape[2]
    width = H_B * D_B
    rows = H_B * ts

    def tok(b):
        return (b, 0)

    def cache(b):
        return (layer, b, 0, 0)

    return pl.pallas_call(
        functools.partial(_fox_decode_kernel, tk=tk, past=past, ts=ts), grid=(n_seq,),
        in_specs=[pl.BlockSpec((ts, width), tok), pl.BlockSpec((1, 1, past, width), cache),
                  pl.BlockSpec((1, 1, past, width), cache), pl.BlockSpec((ts, width), tok), pl.BlockSpec((ts, width), tok),
                  pl.BlockSpec((1, ts, H_B), lambda b: (b, 0, 0)),
                  pl.BlockSpec((1, 1, H_B, past // tk, tk), lambda b: (layer, b, 0, 0, 0)),
                  pl.BlockSpec((1, H_B, ts), lambda b: (b, 0, 0))],
        out_specs=pl.BlockSpec((ts, width), tok),
        out_shape=jax.ShapeDtypeStruct((n_seq * ts, width), F32),
        scratch_shapes=[pltpu.VMEM((rows, 1), F32), pltpu.VMEM((rows, 1), F32), pltpu.VMEM((rows, width), F32)],
        compiler_params=_cparams(("arbitrary",)), name="fox_decode",
    )(q, cache_k, cache_v, k_new, v_new, fq_col, f_cache, f_new.reshape(n_seq, H_B, ts))


def _mla_decode_kernel(q_ref, cc_ref, cr_ref, cn_ref, kn_ref, o_ref, m_ref, l_ref, acc_ref, *, tk, past, ts):
    q = q_ref[0]
    q_lat = q[:, :KV_LORA]
    q_rope = q[:, KV_LORA:KV_LORA + ROPE_C]
    m_ref[...] = jnp.full(m_ref.shape, NEG, F32)
    l_ref[...] = jnp.zeros(l_ref.shape, F32)
    acc_ref[...] = jnp.zeros(acc_ref.shape, F32)

    def cached(jb, carry):
        k0 = pl.multiple_of(jb * tk, tk)
        ck = cc_ref[0, 0, pl.ds(k0, tk), :]
        s = _hdot(q_lat, ck, _NT) + _hdot(q_rope, cr_ref[0, 0, pl.ds(k0, tk), :], _NT)
        _online_softmax_step(s, ck, m_ref, l_ref, acc_ref)
        return carry

    lax.fori_loop(0, past // tk, cached, 0)
    cn = cn_ref[...]
    s = _hdot(q_lat, cn, _NT) + _hdot(q_rope, kn_ref[:, KV_LORA:KV_LORA + ROPE_C], _NT)
    q_chunk = (past + lax.broadcasted_iota(jnp.int32, s.shape, 0) % ts) // CHUNK
    k_chunk = (past + lax.broadcasted_iota(jnp.int32, s.shape, 1)) // CHUNK
    _online_softmax_step(jnp.where(k_chunk <= q_chunk, s, NEG), cn, m_ref, l_ref, acc_ref)
    o_ref[0] = acc_ref[...] / l_ref[...]


def _mla_decode(q, cache_ckv, cache_krope, layer, ckv_new, kc_new, n_seq, ts, tk):
    past = cache_ckv.shape[2]
    rows = H_C * ts

    def tok(b):
        return (b, 0)

    def cache(b):
        return (layer, b, 0, 0)

    return pl.pallas_call(
        functools.partial(_mla_decode_kernel, tk=tk, past=past, ts=ts), grid=(n_seq,),
        in_specs=[pl.BlockSpec((1, rows, 2 * LANE), lambda b: (b, 0, 0)),
                  pl.BlockSpec((1, 1, past, KV_LORA), cache), pl.BlockSpec((1, 1, past, ROPE_C), cache),
                  pl.BlockSpec((ts, KV_LORA), tok), pl.BlockSpec((ts, 2 * LANE), tok)],
        out_specs=pl.BlockSpec((1, rows, KV_LORA), lambda b: (b, 0, 0)),
        out_shape=jax.ShapeDtypeStruct((n_seq, rows, KV_LORA), F32),
        scratch_shapes=[pltpu.VMEM((rows, 1), F32), pltpu.VMEM((rows, 1), F32), pltpu.VMEM((rows, KV_LORA), F32)],
        compiler_params=_cparams(("arbitrary",)), name="mla_decode",
    )(q, cache_ckv, cache_krope, ckv_new, kc_new)


def _mixout_kernel(*refs, precise, route):
    x_ref, gate_ref, ya_ref, of_ref, ol_ref, wuv_ref, wo_ref, sh_ref, sc_ref, g_ref = refs[:10]
    rest = list(refs[10:])
    wr_ref = rest.pop(0) if route else None
    rest.pop(0)
    o_ref, h_ref = rest[:2]
    ids_ref, gates_ref = rest[2:] if route else (None, None)
    mm = functools.partial(_mm, precise=precise)
    acc = mm(ya_ref[...], wo_ref[:V_A, :])
    if len(of_ref.shape) == 2:
        acc = acc + mm(of_ref[...], wo_ref[V_A:V_A + QKV_B, :])
    else:
        for hh in range(H_B):
            r0 = V_A + hh * D_B
            acc = acc + mm(of_ref[0, hh], wo_ref[r0:r0 + D_B, :])
    yc = mm(ol_ref[0, 0], wuv_ref[0])
    for hh in range(1, H_C):
        yc = yc + mm(ol_ref[0, hh], wuv_ref[hh])
    acc = acc + mm(yc, wo_ref[V_A + QKV_B:, :])
    x_new = x_ref[...] + gate_ref[0] * acc
    o_ref[...] = x_new
    _ffn_input(x_new, sh_ref, sc_ref, g_ref, wr_ref, h_ref, ids_ref, gates_ref)


def _mixout(x2, gate, ya, ofox, olat, wuv, wo, shift, scale, g_ffn, router_pad, hbuf, row0, n_seq, t_len, tm, precise):
    nblk = t_len // tm
    n_tok = n_seq * t_len
    mod_rows = gate.shape[1]
    route = router_pad is not None
    off = row0 // tm

    def row(i):
        return (i, 0)

    def seq(i):
        return (i // nblk, 0, 0)

    def const2(i):
        return (0, 0)

    mod_spec = pl.BlockSpec((1, mod_rows, D_MODEL), seq)
    in_specs = [pl.BlockSpec((tm, D_MODEL), row), mod_spec, pl.BlockSpec((tm, V_A), row),
                pl.BlockSpec((tm, QKV_B), row) if ofox.ndim == 2 else
                pl.BlockSpec((1, H_B, tm, D_B), lambda i: (i // nblk, 0, i % nblk, 0)),
                pl.BlockSpec((1, H_C, tm, KV_LORA), lambda i: (i, 0, 0, 0)),
                pl.BlockSpec((H_C, KV_LORA, V_A), lambda i: (0, 0, 0)),
                pl.BlockSpec((D_MODEL, D_MODEL), const2),
                mod_spec, mod_spec, pl.BlockSpec((1, D_MODEL), const2)]
    args = [x2, gate, ya, ofox, olat, wuv, wo, shift, scale, g_ffn]
    out_shape = [jax.ShapeDtypeStruct((n_tok, D_MODEL), F32), jax.ShapeDtypeStruct(hbuf.shape, F32)]
    out_specs = [pl.BlockSpec((tm, D_MODEL), row), _block_rows(tm, hbuf, lambda i: (off + i, 0))]
    if route:
        in_specs.append(pl.BlockSpec((D_MODEL, LANE), const2))
        args.append(router_pad)
        out_shape += [jax.ShapeDtypeStruct((n_tok, LANE), jnp.int32), jax.ShapeDtypeStruct((n_tok, LANE), F32)]
        out_specs += [pl.BlockSpec((tm, LANE), row), pl.BlockSpec((tm, LANE), row)]
    in_specs.append(pl.BlockSpec(memory_space=pl.ANY))
    args.append(hbuf)
    return pl.pallas_call(
        functools.partial(_mixout_kernel, precise=precise, route=route), grid=(n_tok // tm,),
        in_specs=in_specs, out_specs=tuple(out_specs), out_shape=tuple(out_shape),
        input_output_aliases={len(args) - 1: 1},
        compiler_params=_cparams(("arbitrary",)), name="mixout",
    )(*args)


def _ffn_input(x, sh_ref, sc_ref, g_ref, wr_ref, h_ref, ids_ref, gates_ref):
    h = (_rms(x) * g_ref[...]) * (1.0 + sc_ref[0]) + sh_ref[0]
    _store_rows(h_ref, h)
    if wr_ref is not None:
        logits = jnp.dot(h, wr_ref[...], precision=HIGHEST, preferred_element_type=F32)
        lane = lax.broadcasted_iota(jnp.int32, logits.shape, 1)
        logits = jnp.where(lane < N_EXPERTS, logits, NEG)
        m1 = jnp.max(logits, axis=-1, keepdims=True)
        i1 = jnp.min(jnp.where(logits == m1, lane, LANE), axis=-1, keepdims=True)
        rest = jnp.where(lane == i1, NEG, logits)
        m2 = jnp.max(rest, axis=-1, keepdims=True)
        i2 = jnp.min(jnp.where(rest == m2, lane, LANE), axis=-1, keepdims=True)
        e2 = jnp.exp(m2 - m1)
        g1 = 1.0 / (1.0 + e2)
        g2 = e2 / (1.0 + e2)
        ids_ref[...] = jnp.where(lane == 0, i1, i2)
        gates_ref[...] = jnp.where(lane == 0, g1, g2)


GATHER_ROWS = 256
GATHER_UNROLL = 8


def _gather_kernel(idx_ref, src_ref, out_ref, sem):
    def row_copy(r, src_row):
        return pltpu.make_async_copy(src_ref.at[src_row], out_ref.at[r], sem)

    def start(r, c):
        row_copy(r, idx_ref[0, 0, r]).start()
        return c

    def wait(r, c):
        row_copy(r, 0).wait()
        return c

    lax.fori_loop(0, GATHER_ROWS, start, 0, unroll=GATHER_UNROLL)
    lax.fori_loop(0, GATHER_ROWS, wait, 0, unroll=GATHER_UNROLL)


def _gather_rows(src, idx):
    m = idx.shape[0]
    return pl.pallas_call(
        _gather_kernel, grid=(m // GATHER_ROWS,),
        in_specs=[pl.BlockSpec((1, 1, GATHER_ROWS), lambda i: (i, 0, 0), memory_space=pltpu.SMEM),
                  pl.BlockSpec(memory_space=pl.ANY)],
        out_specs=pl.BlockSpec((GATHER_ROWS,) + src.shape[1:], lambda i: (i, 0, 0)),
        out_shape=jax.ShapeDtypeStruct((m,) + src.shape[1:], src.dtype),
        scratch_shapes=[pltpu.SemaphoreType.DMA(())],
        compiler_params=pltpu.CompilerParams(dimension_semantics=("arbitrary",)), name="gather_rows",
    )(idx.reshape(m // GATHER_ROWS, 1, GATHER_ROWS), src)


def _swiglu_kernel(te_ref, nt_ref, x_ref, wg_ref, wu_ref, wd_ref, o_ref, acc_ref, xs_ref, *, precise):
    mm = functools.partial(_mm, precise=precise)
    i = pl.program_id(0)
    j = pl.program_id(1)

    @pl.when(i < nt_ref[0])
    def _():
        @pl.when(j == 0)
        def _():
            acc_ref[...] = jnp.zeros(acc_ref.shape, F32)
            xs_ref[...] = _load_rows(x_ref).astype(xs_ref.dtype)

        x = xs_ref[...]
        tf = wg_ref.shape[2]
        cw = SWIGLU_COLS if tf % SWIGLU_COLS == 0 else tf
        part = None
        for c0 in range(0, tf, cw):
            a = mm(x, wg_ref[0, :, c0:c0 + cw])
            u = mm(x, wu_ref[0, :, c0:c0 + cw])
            d = mm(_silu(a) * u, wd_ref[0, c0:c0 + cw, :])
            part = d if part is None else part + d
        acc_ref[...] += part

        @pl.when(j == pl.num_programs(1) - 1)
        def _():
            _store_rows(o_ref, acc_ref[...])

    @pl.when((i >= nt_ref[0]) & (j == pl.num_programs(1) - 1))
    def _():
        o_ref[...] = jnp.zeros(o_ref.shape, F32)


def _swiglu_grouped(x, tile_expert, n_tiles_used, wg, wu, wd, tm, tf, precise=False):
    m = x.shape[0]
    f = wg.shape[2]
    grid_spec = pltpu.PrefetchScalarGridSpec(
        num_scalar_prefetch=2, grid=(m // tm, f // tf),
        in_specs=[_block_rows(tm, x, lambda i, j, te, nt: (i, 0)),
                  pl.BlockSpec((1, D_MODEL, tf), lambda i, j, te, nt: (te[i], 0, j)),
                  pl.BlockSpec((1, D_MODEL, tf), lambda i, j, te, nt: (te[i], 0, j)),
                  pl.BlockSpec((1, tf, D_MODEL), lambda i, j, te, nt: (te[i], j, 0))],
        out_specs=_block_rows(tm, x, lambda i, j, te, nt: (i, 0)),
        scratch_shapes=[pltpu.VMEM((tm, D_MODEL), F32), pltpu.VMEM((tm, D_MODEL), F32 if precise else BF16)])
    return pl.pallas_call(
        functools.partial(_swiglu_kernel, precise=precise), grid_spec=grid_spec,
        out_shape=jax.ShapeDtypeStruct(x.shape, F32),
        compiler_params=_cparams(("arbitrary", "arbitrary")), name="swiglu",
    )(tile_expert, n_tiles_used, x, wg, wu, wd)


def _combine_kernel(*refs, moe, final):
    refs = list(refs)
    x_ref = refs.pop(0)
    gate_ref = refs.pop(0)
    y1_ref = refs.pop(0)
    if moe:
        y2_ref = refs.pop(0)
        gates_ref = refs.pop(0)
    fg_ref = refs.pop(0) if final else None
    o_ref = refs.pop(0)
    y = _load_rows(y1_ref)
    if moe:
        gts = gates_ref[...]
        y = gts[:, 0:1] * y + gts[:, 1:2] * _load_rows(y2_ref)
    out = x_ref[...] + gate_ref[0] * y
    if final:
        out = _rms(out) * fg_ref[...]
    o_ref[...] = out


def _combine(x2, gate, y, y_row0, y2_row0, gates, final_g, n_seq, t_len, tm):
    nblk = t_len // tm
    n_tok = n_seq * t_len
    moe = gates is not None
    final = final_g is not None

    def row(i):
        return (i, 0)

    in_specs = [pl.BlockSpec((tm, D_MODEL), row), pl.BlockSpec((1, gate.shape[1], D_MODEL), lambda i: (i // nblk, 0, 0)),
                _block_rows(tm, y, lambda i: (y_row0 // tm + i, 0))]
    args = [x2, gate, y]
    if moe:
        in_specs += [_block_rows(tm, y, lambda i: (y2_row0 // tm + i, 0)), pl.BlockSpec((tm, LANE), row)]
        args += [y, gates]
    if final:
        in_specs.append(pl.BlockSpec((1, D_MODEL), lambda i: (0, 0)))
        args.append(final_g)
    return pl.pallas_call(
        functools.partial(_combine_kernel, moe=moe, final=final), grid=(n_tok // tm,),
        in_specs=in_specs, out_specs=pl.BlockSpec((tm, D_MODEL), row),
        out_shape=jax.ShapeDtypeStruct((n_tok, D_MODEL), F32),
        compiler_params=_cparams(("arbitrary",)), name="combine",
    )(*args)


def _pack_mixer_weights(w_in, w_gate2, b_gate, fox_b_f, qng, kvng, w_uq, w_uk, w_uv, gla_norm_g):
    offs = np.concatenate([[0], np.cumsum(IN_SPLITS)])
    cols = {n: (int(offs[i]), int(offs[i + 1])) for i, n in enumerate(
        ("gq", "gk", "gv", "ag", "ar", "fq", "fk", "fv", "bf", "cq", "ckv", "kr"))}

    def seg(n):
        return w_in[:, cols[n][0]:cols[n][1]]

    half = ROPE_C // 2
    kr = seg("kr")

    def zcols(n):
        return jnp.zeros((D_MODEL, n), F32)

    small = jnp.concatenate([kr, seg("ar"), seg("bf"), zcols(SM_KRS - SM_BF - H_B), -kr[:, half:], kr[:, :half],
                             zcols(LANE - SM_KRS - ROPE_C)], axis=1)
    w = jnp.concatenate([seg("gq"), zcols(C_GK - QK_A), seg("gk"), zcols(C_GV - C_GK - QK_A), seg("gv"), seg("ag"),
                         seg("fq"), seg("fk"), seg("fv"), seg("cq"), seg("ckv"), small], axis=1)
    assert w.shape == (D_MODEL, N_PACK) and small.shape[1] == LANE
    wg2 = jnp.pad(w_gate2, ((SM_AR, LANE - SM_AR - GATE_RANK), (0, 2 * LANE - QK_A)))
    bg = jnp.pad(b_gate, (0, 2 * LANE - QK_A)).reshape(1, 2 * LANE)
    bf = jnp.pad(fox_b_f, (SM_BF, LANE - SM_BF - H_B)).reshape(1, LANE)
    uq = w_uq.reshape(Q_LORA, H_C, NOPE_C + ROPE_C)
    wuqn = uq[:, :, :NOPE_C].reshape(Q_LORA, H_C * NOPE_C)
    x1 = uq[:, :, NOPE_C:NOPE_C + half]
    x2 = uq[:, :, NOPE_C + half:]
    pad = jnp.zeros((Q_LORA, H_C, LANE - ROPE_C), F32)
    wr = jnp.concatenate([x1, x2, pad], axis=2).reshape(Q_LORA, H_C * LANE)
    wrs = jnp.concatenate([-x2, x1, pad], axis=2).reshape(Q_LORA, H_C * LANE)
    eye = jnp.eye(H_C, dtype=F32)
    wuk = (jnp.transpose(w_uk, (1, 2, 0))[:, :, None, :] * eye[:, None, :, None]).reshape(H_C * NOPE_C, H_C * KV_LORA)
    wuv = (jnp.transpose(w_uv, (1, 0, 2))[:, :, None, :] * eye[:, None, :, None]).reshape(H_C, KV_LORA, V_A)
    full = dict(w_in=w, wg2=wg2, bg=bg, bf=bf, qng=qng.reshape(1, Q_LORA), kvng=kvng.reshape(1, KV_LORA), wuqn=wuqn,
                wuk=wuk, wr=wr, wrs=wrs, wuv=wuv, gnorm=jnp.tile(gla_norm_g, H_A).reshape(1, V_A))
    half_prec = dict(full)
    for n in ("w_in", "wg2", "wuqn", "wuk", "wr", "wrs", "wuv"):
        half_prec[n] = full[n].astype(BF16)
    return half_prec, full


def _rope_tables(pos):
    half = ROPE_C // 2
    inv_freq = ROPE_BASE ** (-jnp.arange(half, dtype=F32) / half)
    ang = pos.astype(F32)[:, None] * inv_freq[None, :]
    n = pos.shape[0]
    cos = jnp.concatenate([jnp.cos(ang), jnp.cos(ang), jnp.ones((n, LANE - ROPE_C), F32)], axis=1)
    sin = jnp.concatenate([jnp.sin(ang), jnp.sin(ang), jnp.zeros((n, LANE - ROPE_C), F32)], axis=1)
    return cos, sin


def _state_to_t(s):
    b = s.shape[0]
    eye = jnp.eye(H_A, dtype=F32)
    s_vk = jnp.swapaxes(s, 2, 3)
    return (s_vk[:, :, :, None, :] * eye[None, :, None, :, None]).reshape(b, V_A, QK_A)


def _state_from_t(s_t):
    b = s_t.shape[0]
    blocks = s_t.reshape(b, H_A, DV_A, H_A, DK_A)
    diag = jnp.stack([blocks[:, hh, :, hh, :] for hh in range(H_A)], axis=1)
    return jnp.swapaxes(diag, 2, 3)


def _round_up(a, b):
    return (a + b - 1) // b * b


def kernel(x_prompt, x_sample, c_prompt, c_sample, cache_fox_k, cache_fox_v, cache_fox_logf, cache_mla_ckv, cache_mla_krope, state_gla, ada_w, ada_b, norm_mix_g, norm_ffn_g, w_in, gla_w_gate2, gla_b_gate, gla_norm_g, fox_b_f, mla_q_norm_g, mla_kv_norm_g, mla_w_uq, mla_w_uk, mla_w_uv, w_out, ffn_w_gate, ffn_w_up, ffn_w_down, moe_router, moe_w_gate, moe_w_up, moe_w_down, final_norm_g):
    bp, tp, _ = x_prompt.shape
    bs, ts, _ = x_sample.shape
    past = cache_fox_k.shape[2]
    np_tok, ns_tok = bp * tp, bs * ts
    n_all = np_tok + ns_tok

    tm_p, tm_s = 256, ns_tok
    tq_fox, tk = 512, 512

    nc = _round_up(bp + bs, 8)
    c_all = jnp.zeros((nc, D_MODEL), F32).at[:bp].set(c_prompt).at[bp:bp + bs].set(c_sample)
    mod = _ada(c_all, ada_w, ada_b)

    cos_p, sin_p = _rope_tables(jnp.arange(tp))
    cos_s, sin_s = _rope_tables(past + jnp.tile(jnp.arange(ts), bs))

    def seq_major(a):
        hh, d = a.shape[1], a.shape[3]
        return jnp.transpose(a.reshape(hh, bs, ts, d), (1, 0, 2, 3))

    def tok_major(a):
        hh, d = a.shape[1], a.shape[3]
        return jnp.transpose(a, (1, 0, 2, 3)).reshape(1, hh, bs * ts, d)

    cl = jnp.transpose(cache_fox_logf.astype(F32), (0, 1, 3, 2)).reshape(DEPTH * bs * H_B, past)
    f_cache, f_cache2 = _cumsum(cl, jnp.zeros((cl.shape[0], 1), F32), 512)
    f_cache = f_cache.reshape(DEPTH, bs * H_B, past)
    f_cache2 = f_cache2.reshape(DEPTH, bs, H_B, past // tk, tk)
    cache_k2d = cache_fox_k.astype(F32).reshape(DEPTH, bs, past, H_B * D_B)
    cache_v2d = cache_fox_v.astype(F32).reshape(DEPTH, bs, past, H_B * D_B)

    xp = x_prompt.reshape(np_tok, D_MODEL)
    xs = x_sample.reshape(ns_tok, D_MODEL)
    p_states = [[] for _ in range(6)]
    s_states = [[] for _ in range(6)]

    for l in range(DEPTH):
        mods = [mod[l, :, i * D_MODEL:(i + 1) * D_MODEL] for i in range(6)]
        mp = [m[:bp].reshape(bp, 1, D_MODEL) for m in mods]
        ms = [jnp.repeat(m[bp:bp + bs], ts, axis=0).reshape(1, ns_tok, D_MODEL) for m in mods]
        pw, pw32 = _pack_mixer_weights(w_in[l], gla_w_gate2[l], gla_b_gate[l], fox_b_f[l], mla_q_norm_g[l],
                                       mla_kv_norm_g[l], mla_w_uq[l], mla_w_uk[l], mla_w_uv[l], gla_norm_g[l])
        wo = w_out[l].astype(BF16)
        g_mix = norm_mix_g[l].reshape(1, D_MODEL)
        g_ffn = norm_ffn_g[l].reshape(1, D_MODEL)
        moe = l % 2 == 1
        if moe:
            router_pad = jnp.pad(moe_router[l // 2], ((0, 0), (0, LANE - N_EXPERTS)))
            hbuf_p, hbuf_s = jnp.zeros((n_all,) + TOK_TILE, F32), None
        else:
            router_pad = None
            hbuf_p, hbuf_s = jnp.zeros((np_tok, D_MODEL), F32), jnp.zeros((ns_tok, D_MODEL), F32)

        (gq, gk, gv, ag, gla, fk, fv, ckv, kc, small, qs, fqh, fkh, fvh) = _inproj(
            xp, mp[0], mp[1], g_mix, pw, cos_p, sin_p, bp, tp, tm_p, False, True)
        logf = small[:, SM_BF:SM_BF + H_B]
        krope = small[:, SM_KR:SM_KR + ROPE_C]
        ya, s_t = _gla(gq, gk, gv, gla, ag, pw["gnorm"], jnp.zeros((bp, V_A, QK_A), F32), bp, tp, 512, CHUNK, bp, False)
        g_fox = bp * H_B
        fqa, fka = _fox_prep(small, fqh, fkh, bp, tp, 512)
        o_fox = _flash(fqa.reshape(g_fox, tp // tq_fox, tq_fox, 2 * D_B), fka.reshape(g_fox, tp, 2 * D_B),
                       fvh.reshape(g_fox, tp, 2 * D_B), None, None,
                       tq=tq_fox, rep=1, tk=tk, q0=0, mode="causal", kv_len=tp, dv=D_B, hp=H_B)
        o_mla = _flash(qs.reshape(bp, tp // tm_p, H_C * tm_p, 2 * LANE), kc.reshape(bp, tp, 2 * LANE), None, None, None,
                       tq=tm_p, rep=H_C, tk=tk, q0=0, mode="chunk", kv_len=tp, dv=KV_LORA)
        res = _mixout(xp, mp[2], ya, o_fox.reshape(bp, H_B, tp, D_B), o_mla.reshape(np_tok // tm_p, H_C, tm_p, KV_LORA),
                      pw["wuv"], wo, mp[3], mp[4], g_ffn, router_pad, hbuf_p, 0, bp, tp, tm_p, False)
        xp, h_p = res[0], res[1]
        for i, st in enumerate((_state_from_t(s_t), fk.reshape(bp, tp, H_B, D_B), fv.reshape(bp, tp, H_B, D_B),
                                logf.reshape(bp, tp, H_B), ckv.reshape(bp, tp, KV_LORA), krope.reshape(bp, tp, ROPE_C))):
            p_states[i].append(st)

        (gq, gk, gv, ag, gla, fk, fv, ckv, kc, small, qs, fq) = _inproj(
            xs, ms[0], ms[1], g_mix, pw32, cos_s, sin_s, 1, ns_tok, tm_s, True, False)
        logf = small[:, SM_BF:SM_BF + H_B]
        krope = small[:, SM_KR:SM_KR + ROPE_C]
        g_fox = bs * H_B
        f_rows = jnp.transpose(logf.reshape(bs, ts, H_B), (0, 2, 1)).reshape(g_fox, ts)
        f_new = _cumsum(f_rows, f_cache[l][:, past - 1:past], ts)[1]
        fq_col = jnp.transpose(f_new.reshape(bs, H_B, ts), (0, 2, 1))
        ya, s_t = _gla(gq, gk, gv, gla, ag, pw["gnorm"], _state_to_t(state_gla[l].astype(F32)), bs, ts, ts, ts, 4, True)
        o_fox = _fox_decode(fq, cache_k2d, cache_v2d, l, fk, fv, fq_col, f_cache2, f_new, bs, ts, tk)
        o_mla = _mla_decode(seq_major(qs).reshape(bs, H_C * ts, 2 * LANE), cache_mla_ckv.astype(F32),
                            cache_mla_krope.astype(F32), l, ckv, kc, bs, ts, tk)
        res_s = _mixout(xs, ms[2], ya, o_fox, tok_major(o_mla.reshape(bs, H_C, ts, KV_LORA)), pw32["wuv"], w_out[l],
                        ms[3], ms[4], g_ffn, router_pad, h_p if moe else hbuf_s, np_tok if moe else 0,
                        1, ns_tok, tm_s, True)
        xs, h_s = res_s[0], res_s[1]
        for i, st in enumerate((_state_from_t(s_t), fk.reshape(bs, ts, H_B, D_B), fv.reshape(bs, ts, H_B, D_B),
                                logf.reshape(bs, ts, H_B), ckv.reshape(bs, ts, KV_LORA), krope.reshape(bs, ts, ROPE_C))):
            s_states[i].append(st)

        last = l == DEPTH - 1
        fg = final_norm_g.reshape(1, D_MODEL) if last else None
        tm_f = 512
        if l % 2 == 0:
            j = l // 2
            n_tiles = np_tok // tm_f
            y_p = _swiglu_grouped(h_p, jnp.zeros((n_tiles,), jnp.int32), jnp.full((1,), n_tiles, jnp.int32),
                                  ffn_w_gate[j:j + 1].astype(BF16), ffn_w_up[j:j + 1].astype(BF16),
                                  ffn_w_down[j:j + 1].astype(BF16), tm_f, ffn_w_gate.shape[2])
            y_s = _swiglu_grouped(h_s, jnp.zeros((1,), jnp.int32), jnp.ones((1,), jnp.int32),
                                  ffn_w_gate[j:j + 1], ffn_w_up[j:j + 1], ffn_w_down[j:j + 1], tm_s, 1408, precise=True)
            xp = _combine(xp, mp[5], y_p, 0, 0, None, fg, bp, tp, tm_p)
            xs = _combine(xs, ms[5], y_s, 0, 0, None, fg, 1, ns_tok, tm_s)
        else:
            j = l // 2
            h_all = h_s
            ids_p, gates_p = res[2:]
            ids_s, gates_s = res_s[2:]
            ids = jnp.concatenate([ids_p[:, :2], ids_s[:, :2]], axis=0)
            e = jnp.transpose(ids).reshape(-1)
            onehot = (e[:, None] == jnp.arange(N_EXPERTS)[None, :]).astype(jnp.int32)
            rank = jnp.sum((jnp.cumsum(onehot, axis=0) - onehot) * onehot, axis=1)
            counts = jnp.sum(onehot, axis=0)
            padded = (counts + tm_f - 1) // tm_f * tm_f
            ends = jnp.cumsum(padded)
            starts = ends - padded
            pos = starts[e] + rank
            m_pad = _round_up(2 * n_all + N_EXPERTS * (tm_f - 1), tm_f)
            token = jnp.tile(jnp.arange(n_all, dtype=jnp.int32), 2)
            src = jnp.zeros((m_pad,), jnp.int32).at[pos].set(token)
            n_tiles = m_pad // tm_f
            tile_row0 = jnp.arange(n_tiles, dtype=jnp.int32) * tm_f
            tile_expert = jnp.minimum(jnp.sum((ends[None, :] <= tile_row0[:, None]).astype(jnp.int32), axis=1),
                                      N_EXPERTS - 1)
            n_used = (ends[-1] // tm_f).astype(jnp.int32).reshape(1)
            x_sorted = _gather_rows(h_all, src)
            y = _swiglu_grouped(x_sorted, tile_expert, n_used, moe_w_gate[j].astype(BF16), moe_w_up[j].astype(BF16),
                                moe_w_down[j].astype(BF16), tm_f, 1792)
            n_back = _round_up(n_all, GATHER_ROWS)
            back = jnp.zeros((2 * n_back,), jnp.int32).at[:n_all].set(pos[:n_all]).at[n_back:n_back + n_all].set(pos[n_all:])
            yg = _gather_rows(y, back)
            xp = _combine(xp, mp[5], yg, 0, n_back, gates_p, fg, bp, tp, tm_p)
            xs = _combine(xs, ms[5], yg, np_tok, n_back + np_tok, gates_s, fg, 1, ns_tok, tm_s)

    outs_p = [jnp.stack(s, axis=0) for s in p_states]
    outs_s = [jnp.stack(s, axis=0) for s in s_states]
    return (xp.reshape(bp, tp, D_MODEL), xs.reshape(bs, ts, D_MODEL), *outs_p, *outs_s)
```

```python
import functools

import numpy as np
import jax
import jax.numpy as jnp
from jax import lax
from jax.experimental import pallas as pl
from jax.experimental.pallas import tpu as pltpu

F32 = jnp.float32
BF16 = jnp.bfloat16
HIGHEST = lax.Precision.HIGHEST

D_MODEL = 1024
DEPTH = 2
CHUNK = 64
EPS = 1e-6
H_A, DK_A, DV_A = 6, 32, 64
GATE_RANK = 16
GATE_TAU = 16.0
H_B, D_B = 4, 64
H_C, NOPE_C, ROPE_C, V_C = 6, 64, 32, 64
Q_LORA, KV_LORA = 256, 128
ROPE_BASE = 10000.0
N_EXPERTS = 8
IN_SPLITS = (H_A * DK_A, H_A * DK_A, H_A * DV_A, H_A * DV_A, GATE_RANK,
             H_B * D_B, H_B * D_B, H_B * D_B, H_B, Q_LORA, KV_LORA, ROPE_C)

QK_A = H_A * DK_A
V_A = H_A * DV_A
QKV_B = H_B * D_B
LANE = 128
NEG = -1e30

C_GQ, C_GK, C_GV, C_AG = 0, 256, 512, 896
C_FQ, C_FK, C_FV = 1280, 1536, 1792
C_CQ, C_CKV, C_SM = 2048, 2304, 2432
N_PACK = 2560
SM_KR, SM_AR, SM_BF, SM_KRS = 0, 32, 48, 64

GLA_SUB = 16
FLASH_ROW_GROUP = 256
SWIGLU_COLS = 256
LOG2E = float(np.log2(np.e))
VMEM_LIMIT = 56 * 1024 * 1024


def _cparams(sem):
    return pltpu.CompilerParams(dimension_semantics=sem, vmem_limit_bytes=VMEM_LIMIT)


def _log_sigmoid(z):
    return jnp.minimum(z, 0.0) - jnp.log1p(jnp.exp(-jnp.abs(z)))


def _silu(z):
    return z * (1.0 / (1.0 + jnp.exp(-z)))


def _rms(x):
    return x * lax.rsqrt(jnp.mean(x * x, axis=-1, keepdims=True) + EPS)


def _dot(a, b):
    return jnp.dot(a, b, preferred_element_type=F32)


def _dot_nt(a, b):
    return lax.dot_general(a, b, (((1,), (1,)), ((), ())), preferred_element_type=F32)


def _dot_tn(a, b):
    return lax.dot_general(a, b, (((0,), (0,)), ((), ())), preferred_element_type=F32)


def _split2(a):
    a = a.astype(F32)
    hi = a.astype(BF16)
    return hi, (a - hi.astype(F32)).astype(BF16)


def _dot_f32(a, b, dims=(((1,), (0,)), ((), ()))):
    a_hi, a_lo = _split2(a)
    b_hi, b_lo = _split2(b)

    def dg(x, y):
        return lax.dot_general(x, y, dims, preferred_element_type=F32)

    return dg(a_hi, b_hi) + (dg(a_hi, b_lo) + dg(a_lo, b_hi))


def _mm(a, b, dims=(((1,), (0,)), ((), ())), *, precise):
    if precise:
        return _dot_f32(a, b, dims)
    return lax.dot_general(a.astype(BF16), b.astype(BF16), dims, preferred_element_type=F32)


_NT = (((1,), (1,)), ((), ()))
_TN = (((0,), (0,)), ((), ()))

SUBLANE = 8
assert D_MODEL == SUBLANE * LANE


def _is_tiled(shape):
    return len(shape) == 2 and shape[1] == LANE


def _tiled(a):
    return a.reshape(a.shape[0] * SUBLANE, LANE)


def _untiled(a):
    return a.reshape(a.shape[0] // SUBLANE, SUBLANE, LANE)


def _block_rows(tm, arr, index_map):
    if _is_tiled(arr.shape):
        return pl.BlockSpec((tm * SUBLANE, LANE), index_map)
    return pl.BlockSpec((tm, D_MODEL), index_map)


def _load_rows(ref):
    if _is_tiled(ref.shape):
        tm = ref.shape[0] // SUBLANE
        return jnp.concatenate([ref[pl.ds(s, tm, stride=SUBLANE), :] for s in range(SUBLANE)], axis=-1)
    return ref[...]


def _store_rows(ref, val):
    if _is_tiled(ref.shape):
        tm = ref.shape[0] // SUBLANE
        for s in range(SUBLANE):
            ref[pl.ds(s, tm, stride=SUBLANE), :] = val[:, s * LANE:(s + 1) * LANE]
    else:
        ref[...] = val


def _ada_kernel(c_ref, w_ref, b_ref, o_ref):
    s = _silu(c_ref[...])
    o_ref[0] = jnp.dot(s, w_ref[0], precision=HIGHEST, preferred_element_type=F32) + b_ref[0]


def _ada(c_all, ada_w, ada_b):
    nc = c_all.shape[0]
    tn = 1536
    return pl.pallas_call(
        _ada_kernel,
        grid=(DEPTH, 6 * D_MODEL // tn),
        in_specs=[pl.BlockSpec((nc, D_MODEL), lambda l, j: (0, 0)),
                  pl.BlockSpec((1, D_MODEL, tn), lambda l, j: (l, 0, j)),
                  pl.BlockSpec((1, 1, tn), lambda l, j: (l, 0, j))],
        out_specs=pl.BlockSpec((1, nc, tn), lambda l, j: (l, 0, j)),
        out_shape=jax.ShapeDtypeStruct((DEPTH, nc, 6 * D_MODEL), F32),
        compiler_params=_cparams(("arbitrary", "arbitrary")),
        name="ada",
    )(c_all, ada_w, ada_b.reshape(DEPTH, 1, 6 * D_MODEL))


def _inproj_kernel(x_ref, sh_ref, sc_ref, g_ref, w_ref, wg2_ref, bg_ref, bf_ref, qng_ref, kvng_ref,
                   wuqn_ref, wuk_ref, wr_ref, wrs_ref, cos_ref, sin_ref,
                   gq_ref, gk_ref, gv_ref, ag_ref, gla_ref, fk_ref, fv_ref,
                   ckv_ref, kc_ref, small_ref, qs_ref, *fox_refs, precise):
    mm = functools.partial(_mm, precise=precise)
    act = kc_ref.dtype
    x = x_ref[...]
    h = (_rms(x) * g_ref[...]) * (1.0 + sc_ref[0]) + sh_ref[0]
    p = mm(h, w_ref[...])

    gq_ref[...] = p[:, C_GQ:C_GQ + QK_A] * (DK_A ** -0.5)
    gk_ref[...] = p[:, C_GK:C_GK + QK_A]
    gv_ref[...] = p[:, C_GV:C_GV + V_A]
    ag_ref[...] = p[:, C_AG:C_AG + V_A]
    sm = p[:, C_SM:C_SM + LANE]
    z = mm(sm, wg2_ref[...]) + bg_ref[...]
    gla_ref[...] = _log_sigmoid(z[:, :QK_A]) * (1.0 / GATE_TAU)

    fq = p[:, C_FQ:C_FQ + QKV_B] * (D_B ** -0.5 * LOG2E)
    fk = p[:, C_FK:C_FK + QKV_B]
    fv = p[:, C_FV:C_FV + QKV_B]
    fk_ref[...] = fk
    fv_ref[...] = fv
    if len(fox_refs) == 1:
        fox_refs[0][...] = fq
    else:
        fqh_ref, fkh_ref, fvh_ref = fox_refs
        for hh in range(H_B):
            sl = slice(hh * D_B, (hh + 1) * D_B)
            fqh_ref[0, hh] = fq[:, sl].astype(act)
            fkh_ref[0, hh] = fk[:, sl].astype(act)
            fvh_ref[0, hh, :, :D_B] = fv[:, sl].astype(act)
            fvh_ref[0, hh, :, D_B:] = jnp.ones((fv.shape[0], D_B), act)

    cos = cos_ref[...]
    sin = sin_ref[...]
    lane = lax.broadcasted_iota(jnp.int32, sm.shape, 1)
    kr = sm * cos + pltpu.roll(sm, LANE - SM_KRS, 1) * sin
    logf = _log_sigmoid(sm + bf_ref[...])
    small_ref[...] = jnp.where((lane >= SM_BF) & (lane < SM_BF + H_B), logf, kr)

    ckv = _rms(p[:, C_CKV:C_CKV + KV_LORA]) * kvng_ref[...]
    ckv_ref[...] = ckv
    kc_ref[:, :KV_LORA] = ckv.astype(act)
    kc_ref[:, KV_LORA:] = jnp.where(lane < ROPE_C, kr, 0.0).astype(act)
    cqn = _rms(p[:, C_CQ:C_CQ + Q_LORA]) * qng_ref[...]
    nope = mm(cqn, wuqn_ref[...])
    qlat = mm(nope, wuk_ref[...])
    qa = mm(cqn, wr_ref[...])
    qb = mm(cqn, wrs_ref[...])
    scale = (NOPE_C + ROPE_C) ** -0.5 * LOG2E
    for hh in range(H_C):
        sl = slice(hh * LANE, (hh + 1) * LANE)
        qs_ref[0, hh, :, :KV_LORA] = (qlat[:, sl] * scale).astype(act)
        qs_ref[0, hh, :, KV_LORA:] = ((qa[:, sl] * cos + qb[:, sl] * sin) * scale).astype(act)


def _inproj(x2, shift, scale, g, pw, cos_tab, sin_tab, n_seq, t_len, tm, precise, fox_head_major):
    n_tok = n_seq * t_len
    nblk = t_len // tm
    act = F32 if precise else BF16
    mod_rows = shift.shape[1]

    def row(i):
        return (i, 0)

    def seq(i):
        return (i // nblk, 0, 0)

    def const2(i):
        return (0, 0)

    def tab(i):
        return (i % nblk, 0)

    def headmajor(i):
        return (i // nblk, 0, i % nblk, 0)

    sds = jax.ShapeDtypeStruct
    out_shape = (
        sds((n_tok, QK_A), F32), sds((n_tok, QK_A), F32), sds((n_tok, V_A), F32), sds((n_tok, V_A), F32),
        sds((n_tok, QK_A), F32),
        sds((n_tok, QKV_B), F32), sds((n_tok, QKV_B), F32),
        sds((n_tok, KV_LORA), F32), sds((n_tok, 2 * LANE), act), sds((n_tok, LANE), F32),
        sds((n_tok // tm, H_C, tm, 2 * LANE), act),
    )
    out_specs = (
        pl.BlockSpec((tm, QK_A), row), pl.BlockSpec((tm, QK_A), row), pl.BlockSpec((tm, V_A), row),
        pl.BlockSpec((tm, V_A), row), pl.BlockSpec((tm, QK_A), row),
        pl.BlockSpec((tm, QKV_B), row), pl.BlockSpec((tm, QKV_B), row),
        pl.BlockSpec((tm, KV_LORA), row), pl.BlockSpec((tm, 2 * LANE), row), pl.BlockSpec((tm, LANE), row),
        pl.BlockSpec((1, H_C, tm, 2 * LANE), lambda i: (i, 0, 0, 0)),
    )
    if fox_head_major:
        out_shape += (sds((n_seq, H_B, t_len, D_B), act), sds((n_seq, H_B, t_len, D_B), act),
                      sds((n_seq, H_B, t_len, 2 * D_B), act))
        out_specs += (pl.BlockSpec((1, H_B, tm, D_B), headmajor), pl.BlockSpec((1, H_B, tm, D_B), headmajor),
                      pl.BlockSpec((1, H_B, tm, 2 * D_B), headmajor))
    else:
        out_shape += (sds((n_tok, QKV_B), F32),)
        out_specs += (pl.BlockSpec((tm, QKV_B), row),)
    in_specs = [
        pl.BlockSpec((tm, D_MODEL), row), pl.BlockSpec((1, mod_rows, D_MODEL), seq),
        pl.BlockSpec((1, mod_rows, D_MODEL), seq), pl.BlockSpec((1, D_MODEL), const2),
        pl.BlockSpec((D_MODEL, N_PACK), const2), pl.BlockSpec((LANE, 2 * LANE), const2),
        pl.BlockSpec((1, 2 * LANE), const2), pl.BlockSpec((1, LANE), const2),
        pl.BlockSpec((1, Q_LORA), const2), pl.BlockSpec((1, KV_LORA), const2),
        pl.BlockSpec((Q_LORA, H_C * NOPE_C), const2), pl.BlockSpec((H_C * NOPE_C, H_C * KV_LORA), const2),
        pl.BlockSpec((Q_LORA, H_C * LANE), const2), pl.BlockSpec((Q_LORA, H_C * LANE), const2),
        pl.BlockSpec((tm, LANE), tab), pl.BlockSpec((tm, LANE), tab),
    ]
    return pl.pallas_call(
        functools.partial(_inproj_kernel, precise=precise), grid=(n_tok // tm,), in_specs=in_specs,
        out_specs=out_specs, out_shape=out_shape, compiler_params=_cparams(("arbitrary",)), name="inproj",
    )(x2, shift, scale, g, pw["w_in"], pw["wg2"], pw["bg"], pw["bf"], pw["qng"], pw["kvng"],
      pw["wuqn"], pw["wuk"], pw["wr"], pw["wrs"], cos_tab, sin_tab)


def _cumsum_kernel(x_ref, init_ref, o_ref, o2_ref, *, tb):
    n = x_ref.shape[1]
    upper = (lax.broadcasted_iota(jnp.int32, (tb, tb), 0) <= lax.broadcasted_iota(jnp.int32, (tb, tb), 1)).astype(F32)
    carry = init_ref[...]
    for j in range(n // tb):
        blk = jnp.dot(x_ref[:, j * tb:(j + 1) * tb], upper, precision=HIGHEST, preferred_element_type=F32) + carry
        o_ref[:, j * tb:(j + 1) * tb] = blk
        o2_ref[:, j * tb:(j + 1) * tb] = blk * LOG2E
        carry = blk[:, tb - 1:tb]


def _cumsum(x, init, tb):
    return pl.pallas_call(
        functools.partial(_cumsum_kernel, tb=tb),
        out_shape=(jax.ShapeDtypeStruct(x.shape, F32), jax.ShapeDtypeStruct(x.shape, F32)), name="cumsum",
    )(x, init)


def _fox_prep_kernel(sm_ref, q_ref, k_ref, qa_ref, ka_ref, carry_ref, *, tb):
    @pl.when(pl.program_id(1) == 0)
    def _():
        carry_ref[...] = jnp.zeros(carry_ref.shape, F32)

    tril = (lax.broadcasted_iota(jnp.int32, (tb, tb), 0) >= lax.broadcasted_iota(jnp.int32, (tb, tb), 1)).astype(BF16)
    x = sm_ref[...]
    x_hi = x.astype(BF16)
    x_mid = (x - x_hi.astype(F32)).astype(BF16)
    x_lo = (x - x_hi.astype(F32) - x_mid.astype(F32)).astype(BF16)
    cum = _dot(tril, x_hi) + _dot(tril, x_mid) + _dot(tril, x_lo) + carry_ref[...]
    carry_ref[...] = cum[tb - 1:tb]
    lane = lax.broadcasted_iota(jnp.int32, (tb, D_B), 1)
    for hh in range(H_B):
        f = jnp.broadcast_to(cum[:, SM_BF + hh:SM_BF + hh + 1] * LOG2E, (tb, D_B))
        hi = f.astype(BF16).astype(F32)
        mid = (f - hi).astype(BF16).astype(F32)
        lo = (f - hi - mid).astype(BF16).astype(F32)
        terms = jnp.where(lane % 3 == 0, hi, jnp.where(lane % 3 == 1, mid, lo))
        q_extra = jnp.where(lane < 3, terms, jnp.where(lane < 6, 1.0, 0.0))
        k_extra = jnp.where(lane < 3, 1.0, jnp.where(lane < 6, -terms, 0.0))
        qa_ref[0, hh, :, :D_B] = q_ref[0, hh]
        qa_ref[0, hh, :, D_B:] = q_extra.astype(BF16)
        ka_ref[0, hh, :, :D_B] = k_ref[0, hh]
        ka_ref[0, hh, :, D_B:] = k_extra.astype(BF16)


def _fox_prep(small, fqh, fkh, n_seq, t_len, tb):
    nblk = t_len // tb

    def hm(b, j):
        return (b, 0, j, 0)

    return pl.pallas_call(
        functools.partial(_fox_prep_kernel, tb=tb), grid=(n_seq, nblk),
        in_specs=[pl.BlockSpec((tb, LANE), lambda b, j: (b * nblk + j, 0)),
                  pl.BlockSpec((1, H_B, tb, D_B), hm), pl.BlockSpec((1, H_B, tb, D_B), hm)],
        out_specs=(pl.BlockSpec((1, H_B, tb, 2 * D_B), hm), pl.BlockSpec((1, H_B, tb, 2 * D_B), hm)),
        out_shape=(jax.ShapeDtypeStruct((n_seq, H_B, t_len, 2 * D_B), BF16),) * 2,
        scratch_shapes=[pltpu.VMEM((1, LANE), F32)],
        compiler_params=_cparams(("arbitrary", "arbitrary")), name="fox_prep",
    )(small, fqh, fkh)


def _gla_kernel(q_ref, k_ref, v_ref, la_ref, ag_ref, gn_ref, s0_ref, y_ref, sout_ref, s_ref, *, chunk, n_chunks,
                precise):
    mm = functools.partial(_mm, precise=precise)
    c = chunk
    nsub = c // GLA_SUB

    nb = q_ref.shape[0]

    @pl.when(pl.program_id(1) == 0)
    def _():
        s_ref[...] = s0_ref[...]

    lane_qk = lax.broadcasted_iota(jnp.int32, (GLA_SUB, QK_A), 1) // DK_A
    lane_v = lax.broadcasted_iota(jnp.int32, (GLA_SUB, V_A), 1) // DV_A
    bd = (lax.broadcasted_iota(jnp.int32, (V_A, QK_A), 0) // DV_A) == (lax.broadcasted_iota(jnp.int32, (V_A, QK_A), 1) // DK_A)
    tril = (lax.broadcasted_iota(jnp.int32, (c, c), 0) >= lax.broadcasted_iota(jnp.int32, (c, c), 1)).astype(F32)
    hm = (lax.broadcasted_iota(jnp.int32, (V_A, V_A), 0) // DV_A) == (lax.broadcasted_iota(jnp.int32, (V_A, V_A), 1) // DV_A)
    head_mean = jnp.where(hm, 1.0 / DV_A, 0.0).astype(F32)

    def cumsum_rows(la):
        if precise:
            return jnp.dot(tril, la, precision=HIGHEST, preferred_element_type=F32)
        hi = la.astype(BF16)
        lo = (la - hi.astype(F32)).astype(BF16)
        tb16 = tril.astype(BF16)
        return _dot(tb16, hi) + _dot(tb16, lo)

    def chunk_step(bb, r):
        q = q_ref[bb, pl.ds(r, c), :]
        k = k_ref[bb, pl.ds(r, c), :]
        v = v_ref[bb, pl.ds(r, c), :]
        la = la_ref[bb, pl.ds(r, c), :]
        b = cumsum_rows(la)
        s_t = s_ref[bb]
        vb = v if precise else v.astype(BF16)
        o_inter = mm(q * jnp.exp(b), s_t, _NT)
        outs = []
        for i in range(nsub):
            r0 = i * GLA_SUB
            r1 = r0 + GLA_SUB
            bi = b[r0 - 1:r0] if i > 0 else jnp.zeros((1, QK_A), F32)
            qi = q[r0:r1] * jnp.exp(b[r0:r1] - bi)
            kk = k[:r1] * jnp.exp(bi - b[:r1])
            qst = jnp.concatenate([jnp.where(lane_qk == hh, qi, 0.0) for hh in range(H_A)], axis=0)
            att = mm(qst, kk, _NT)
            t_idx = r0 + lax.broadcasted_iota(jnp.int32, att.shape, 0) % GLA_SUB
            s_idx = lax.broadcasted_iota(jnp.int32, att.shape, 1)
            att = jnp.where(s_idx <= t_idx, att, 0.0)
            oi = mm(att, vb[:r1])
            o = jnp.zeros((GLA_SUB, V_A), F32)
            for hh in range(H_A):
                o = o + jnp.where(lane_v == hh, oi[hh * GLA_SUB:(hh + 1) * GLA_SUB], 0.0)
            outs.append(o)
        o = jnp.concatenate(outs, axis=0) + o_inter if nsub > 1 else outs[0] + o_inter
        b_last = b[c - 1:c]
        kd = k * jnp.exp(b_last - b)
        s_ref[bb] = s_t * jnp.exp(b_last) + jnp.where(bd, mm(vb, kd, _TN), 0.0)
        ms = mm(o * o, head_mean)
        y = o * lax.rsqrt(ms + EPS) * gn_ref[...] * _silu(ag_ref[bb, pl.ds(r, c), :])
        y_ref[bb, pl.ds(r, c), :] = y.astype(y_ref.dtype)

    def body(ci, carry):
        r = pl.multiple_of(ci * c, c)
        for bb in range(nb):
            chunk_step(bb, r)
        return carry

    lax.fori_loop(0, n_chunks, body, 0)

    @pl.when(pl.program_id(1) == pl.num_programs(1) - 1)
    def _():
        sout_ref[...] = s_ref[...]


def _gla(gq, gk, gv, gla, ag, gnorm, s0_t, n_seq, t_len, tb, chunk, nb, precise):
    nblk = t_len // tb

    def row(b, j):
        return (b, j, 0)

    def st(b, j):
        return (b, 0, 0)

    def seq3(a):
        return a.reshape(n_seq, t_len, a.shape[-1])

    ya, s_t = pl.pallas_call(
        functools.partial(_gla_kernel, chunk=chunk, n_chunks=tb // chunk, precise=precise),
        grid=(n_seq // nb, nblk),
        in_specs=[pl.BlockSpec((nb, tb, QK_A), row), pl.BlockSpec((nb, tb, QK_A), row), pl.BlockSpec((nb, tb, V_A), row),
                  pl.BlockSpec((nb, tb, QK_A), row), pl.BlockSpec((nb, tb, V_A), row),
                  pl.BlockSpec((1, V_A), lambda b, j: (0, 0)), pl.BlockSpec((nb, V_A, QK_A), st)],
        out_specs=(pl.BlockSpec((nb, tb, V_A), row), pl.BlockSpec((nb, V_A, QK_A), st)),
        out_shape=(jax.ShapeDtypeStruct((n_seq, t_len, V_A), F32 if precise else BF16),
                   jax.ShapeDtypeStruct((n_seq, V_A, QK_A), F32)),
        scratch_shapes=[pltpu.VMEM((nb, V_A, QK_A), F32)],
        compiler_params=_cparams(("arbitrary", "arbitrary")), name="gla",
    )(seq3(gq), seq3(gk), seq3(gv), seq3(gla), seq3(ag), gnorm, s0_t)
    return ya.reshape(n_seq * t_len, V_A), s_t


def _flash_kernel(*refs, tq, rep, hp, tk, q0, mode, kv_len, dv, bias, v_from_k, precise):
    mm = functools.partial(_mm, precise=precise)
    refs = list(refs)
    q_ref = refs.pop(0)
    k_ref = refs.pop(0)
    v_ref = k_ref if v_from_k else refs.pop(0)
    fq_ref = refs.pop(0) if bias else None
    fk_ref = refs.pop(0) if bias else None
    o_ref, m_ref, acc_ref = refs
    i = pl.program_id(1)
    rows = rep * tq
    m_ref[...] = jnp.full(m_ref.shape, NEG, F32)
    acc_ref[...] = jnp.zeros(acc_ref.shape, F32)

    first_q = q0 + i * tq
    last_q = first_q + tq - 1
    if mode == "chunk":
        vis_all = (first_q // CHUNK) * CHUNK + CHUNK - 1
        vis_any = (last_q // CHUNK) * CHUNK + CHUNK - 1
    else:
        vis_all = first_q
        vis_any = last_q
    vis_all = jnp.minimum(vis_all, kv_len - 1)
    vis_any = jnp.minimum(vis_any, kv_len - 1)
    n_full = (vis_all + 1) // tk
    n_any = vis_any // tk + 1

    grp = min(rows, FLASH_ROW_GROUP)

    def step(jb, masked):
        for hh in range(hp):
            head_step(hh, jb, masked)

    def head_step(hh, jb, masked):
        k_start = pl.multiple_of(jb * tk, tk)
        k = k_ref[hh, pl.ds(k_start, tk), :]
        if v_from_k:
            v = jnp.where(lax.broadcasted_iota(jnp.int32, k.shape, 1) < dv, k, jnp.ones_like(k))
        else:
            v = v_ref[hh, pl.ds(k_start, tk), :]
        for g in range(rows // grp):
            rs = slice(g * grp, (g + 1) * grp)
            s = mm(q_ref[hh, 0, rs, :], k, _NT)
            if bias:
                s = s + fq_ref[hh, 0, rs, :] - fk_ref[hh, jb]
            rs = slice(hh * rows + g * grp, hh * rows + (g + 1) * grp)
            if masked:
                qpos = first_q + (g * grp + lax.broadcasted_iota(jnp.int32, (grp, tk), 0)) % tq
                kpos = k_start + lax.broadcasted_iota(jnp.int32, (grp, tk), 1)
                if mode == "chunk":
                    ok = (kpos // CHUNK) <= (qpos // CHUNK)
                else:
                    ok = kpos <= qpos
                ok = ok & (kpos < kv_len)
                s = jnp.where(ok, s, NEG)
            chunks = [s[:, c * LANE:(c + 1) * LANE] for c in range(tk // LANE)]
            smax = chunks[0]
            for ch in chunks[1:]:
                smax = jnp.maximum(smax, ch)
            m_prev = m_ref[rs, :]
            m_new = jnp.maximum(m_prev, jnp.max(smax, axis=-1, keepdims=True))
            alpha = jnp.exp2(m_prev - m_new)
            p = jnp.concatenate([jnp.exp2(ch - m_new) for ch in chunks], axis=1)
            acc = acc_ref[rs, :]
            alpha_w = alpha if acc.shape[1] == LANE else jnp.concatenate([alpha] * (acc.shape[1] // LANE), axis=1)
            acc_ref[rs, :] = alpha_w * acc + mm(p, v)
            m_ref[rs, :] = m_new

    def loop(lo, hi, masked):
        def body(jb, carry):
            step(jb, masked)
            return carry
        lax.fori_loop(lo, hi, body, 0)

    loop(0, n_full, False)
    loop(n_full, n_any, True)

    acc = acc_ref[...]
    if dv == LANE:
        out = acc[:, :dv] / acc[:, dv:]
    else:
        out = (acc / pltpu.roll(acc, LANE - dv, 1))[:, :dv]
    for hh in range(hp):
        o_ref[hh, 0] = out[hh * rows:(hh + 1) * rows].astype(o_ref.dtype)


def _flash(q, k, v, fq, fk, *, tq, rep, tk, q0, mode, kv_len, dv, hp=1, precise=False):
    g, nq, rows, dqk = q.shape
    t_k = k.shape[1]
    nk = t_k // tk
    bias = fq is not None
    v_from_k = v is None

    def qmap(b, i):
        return (b, i, 0, 0)

    def kmap(b, i):
        return (b, 0, 0)

    in_specs = [pl.BlockSpec((hp, 1, rows, dqk), qmap), pl.BlockSpec((hp, t_k, dqk), kmap)]
    args = [q, k]
    dva = dqk if v_from_k else v.shape[2]
    assert dva % LANE == 0 and dva > dv
    if not v_from_k:
        in_specs.append(pl.BlockSpec((hp, t_k, dva), kmap))
        args.append(v)
    if bias:
        in_specs += [pl.BlockSpec((hp, 1, rows, 1), qmap), pl.BlockSpec((hp, nk, 1, tk), lambda b, i: (b, 0, 0, 0))]
        args += [fq, fk.reshape(g, nk, 1, tk)]
    return pl.pallas_call(
        functools.partial(_flash_kernel, tq=tq, rep=rep, hp=hp, tk=tk, q0=q0, mode=mode, kv_len=kv_len, dv=dv,
                          bias=bias, v_from_k=v_from_k, precise=precise),
        grid=(g // hp, nq), in_specs=in_specs,
        out_specs=pl.BlockSpec((hp, 1, rows, dv), qmap),
        out_shape=jax.ShapeDtypeStruct((g, nq, rows, dv), F32 if precise else BF16),
        scratch_shapes=[pltpu.VMEM((hp * rows, LANE), F32), pltpu.VMEM((hp * rows, dva), F32)],
        compiler_params=_cparams(("arbitrary", "arbitrary")), name="flash_" + mode,
    )(*args)


_hdot = _dot_f32


def _online_softmax_step(s, v, m_ref, l_ref, acc_ref):
    m_prev = m_ref[...]
    m_new = jnp.maximum(m_prev, jnp.max(s, axis=-1, keepdims=True))
    alpha = jnp.exp2(m_prev - m_new)
    p = jnp.exp2(s - m_new)
    l_ref[...] = alpha * l_ref[...] + jnp.sum(p, axis=-1, keepdims=True)
    acc_ref[...] = alpha * acc_ref[...] + _hdot(p, v)
    m_ref[...] = m_new


def _fox_decode_kernel(q_ref, kc_ref, vc_ref, kn_ref, vn_ref, fq_ref, fkc_ref, fkn_ref, o_ref, m_ref, l_ref, acc_ref,
                       *, tk, past, ts):
    rows = H_B * ts
    q = q_ref[...]
    lane_head = lax.broadcasted_iota(jnp.int32, q.shape, 1) // D_B
    q_st = jnp.concatenate([jnp.where(lane_head == hh, q, 0.0) for hh in range(H_B)], axis=0)
    fq = fq_ref[0]
    fq_st = jnp.concatenate([fq[:, hh:hh + 1] for hh in range(H_B)], axis=0)
    m_ref[...] = jnp.full(m_ref.shape, NEG, F32)
    l_ref[...] = jnp.zeros(l_ref.shape, F32)
    acc_ref[...] = jnp.zeros(acc_ref.shape, F32)

    def cached(jb, carry):
        k0 = pl.multiple_of(jb * tk, tk)
        fk_st = jnp.concatenate([jnp.broadcast_to(fkc_ref[0, 0, hh, pl.ds(jb, 1), :], (ts, tk)) for hh in range(H_B)],
                                axis=0)
        s = _hdot(q_st, kc_ref[0, 0, pl.ds(k0, tk), :], _NT) + fq_st - fk_st
        _online_softmax_step(s, vc_ref[0, 0, pl.ds(k0, tk), :], m_ref, l_ref, acc_ref)
        return carry

    lax.fori_loop(0, past // tk, cached, 0)
    fkn = fkn_ref[0]
    fk_st = jnp.concatenate([jnp.broadcast_to(fkn[hh:hh + 1, :], (ts, ts)) for hh in range(H_B)], axis=0)
    s = _hdot(q_st, kn_ref[...], _NT) + fq_st - fk_st
    t_idx = lax.broadcasted_iota(jnp.int32, s.shape, 0) % ts
    s_idx = lax.broadcasted_iota(jnp.int32, s.shape, 1)
    _online_softmax_step(jnp.where(s_idx <= t_idx, s, NEG), vn_ref[...], m_ref, l_ref, acc_ref)
    res = acc_ref[...] / l_ref[...]
    out = jnp.zeros((ts, H_B * D_B), F32)
    for hh in range(H_B):
        out = out + jnp.where(lane_head == hh, res[hh * ts:(hh + 1) * ts], 0.0)
    o_ref[...] = out


def _fox_decode(q, cache_k, cache_v, layer, k_new, v_new, fq_col, f_cache, f_new, n_seq, ts, tk):
    past = cache_k.shape[2]
    width = H_B * D_B
    rows = H_B * ts

    def tok(b):
        return (b, 0)

    def cache(b):
        return (layer, b, 0, 0)

    return pl.pallas_call(
        functools.partial(_fox_decode_kernel, tk=tk, past=past, ts=ts), grid=(n_seq,),
        in_specs=[pl.BlockSpec((ts, width), tok), pl.BlockSpec((1, 1, past, width), cache),
                  pl.BlockSpec((1, 1, past, width), cache), pl.BlockSpec((ts, width), tok), pl.BlockSpec((ts, width), tok),
                  pl.BlockSpec((1, ts, H_B), lambda b: (b, 0, 0)),
                  pl.BlockSpec((1, 1, H_B, past // tk, tk), lambda b: (layer, b, 0, 0, 0)),
                  pl.BlockSpec((1, H_B, ts), lambda b: (b, 0, 0))],
        out_specs=pl.BlockSpec((ts, width), tok),
        out_shape=jax.ShapeDtypeStruct((n_seq * ts, width), F32),
        scratch_shapes=[pltpu.VMEM((rows, 1), F32), pltpu.VMEM((rows, 1), F32), pltpu.VMEM((rows, width), F32)],
        compiler_params=_cparams(("arbitrary",)), name="fox_decode",
    )(q, cache_k, cache_v, k_new, v_new, fq_col, f_cache, f_new.reshape(n_seq, H_B, ts))


def _mla_decode_kernel(q_ref, cc_ref, cr_ref, cn_ref, kn_ref, o_ref, m_ref, l_ref, acc_ref, *, tk, past, ts):
    q = q_ref[0]
    q_lat = q[:, :KV_LORA]
    q_rope = q[:, KV_LORA:KV_LORA + ROPE_C]
    m_ref[...] = jnp.full(m_ref.shape, NEG, F32)
    l_ref[...] = jnp.zeros(l_ref.shape, F32)
    acc_ref[...] = jnp.zeros(acc_ref.shape, F32)

    def cached(jb, carry):
        k0 = pl.multiple_of(jb * tk, tk)
        ck = cc_ref[0, 0, pl.ds(k0, tk), :]
        s = _hdot(q_lat, ck, _NT) + _hdot(q_rope, cr_ref[0, 0, pl.ds(k0, tk), :], _NT)
        _online_softmax_step(s, ck, m_ref, l_ref, acc_ref)
        return carry

    lax.fori_loop(0, past // tk, cached, 0)
    cn = cn_ref[...]
    s = _hdot(q_lat, cn, _NT) + _hdot(q_rope, kn_ref[:, KV_LORA:KV_LORA + ROPE_C], _NT)
    q_chunk = (past + lax.broadcasted_iota(jnp.int32, s.shape, 0) % ts) // CHUNK
    k_chunk = (past + lax.broadcasted_iota(jnp.int32, s.shape, 1)) // CHUNK
    _online_softmax_step(jnp.where(k_chunk <= q_chunk, s, NEG), cn, m_ref, l_ref, acc_ref)
    o_ref[0] = acc_ref[...] / l_ref[...]


def _mla_decode(q, cache_ckv, cache_krope, layer, ckv_new, kc_new, n_seq, ts, tk):
    past = cache_ckv.shape[2]
    rows = H_C * ts

    def tok(b):
        return (b, 0)

    def cache(b):
        return (layer, b, 0, 0)

    return pl.pallas_call(
        functools.partial(_mla_decode_kernel, tk=tk, past=past, ts=ts), grid=(n_seq,),
        in_specs=[pl.BlockSpec((1, rows, 2 * LANE), lambda b: (b, 0, 0)),
                  pl.BlockSpec((1, 1, past, KV_LORA), cache), pl.BlockSpec((1, 1, past, ROPE_C), cache),
                  pl.BlockSpec((ts, KV_LORA), tok), pl.BlockSpec((ts, 2 * LANE), tok)],
        out_specs=pl.BlockSpec((1, rows, KV_LORA), lambda b: (b, 0, 0)),
        out_shape=jax.ShapeDtypeStruct((n_seq, rows, KV_LORA), F32),
        scratch_shapes=[pltpu.VMEM((rows, 1), F32), pltpu.VMEM((rows, 1), F32), pltpu.VMEM((rows, KV_LORA), F32)],
        compiler_params=_cparams(("arbitrary",)), name="mla_decode",
    )(q, cache_ckv, cache_krope, ckv_new, kc_new)


def _mixout_kernel(*refs, precise, route):
    x_ref, gate_ref, ya_ref, of_ref, ol_ref, wuv_ref, wo_ref, sh_ref, sc_ref, g_ref = refs[:10]
    rest = list(refs[10:])
    wr_ref = rest.pop(0) if route else None
    rest.pop(0)
    o_ref, h_ref = rest[:2]
    ids_ref, gates_ref = rest[2:] if route else (None, None)
    mm = functools.partial(_mm, precise=precise)
    acc = mm(ya_ref[...], wo_ref[:V_A, :])
    if len(of_ref.shape) == 2:
        acc = acc + mm(of_ref[...], wo_ref[V_A:V_A + QKV_B, :])
    else:
        for hh in range(H_B):
            r0 = V_A + hh * D_B
            acc = acc + mm(of_ref[0, hh], wo_ref[r0:r0 + D_B, :])
    yc = mm(ol_ref[0, 0], wuv_ref[0])
    for hh in range(1, H_C):
        yc = yc + mm(ol_ref[0, hh], wuv_ref[hh])
    acc = acc + mm(yc, wo_ref[V_A + QKV_B:, :])
    x_new = x_ref[...] + gate_ref[0] * acc
    o_ref[...] = x_new
    _ffn_input(x_new, sh_ref, sc_ref, g_ref, wr_ref, h_ref, ids_ref, gates_ref)


def _mixout(x2, gate, ya, ofox, olat, wuv, wo, shift, scale, g_ffn, router_pad, hbuf, row0, n_seq, t_len, tm, precise):
    nblk = t_len // tm
    n_tok = n_seq * t_len
    mod_rows = gate.shape[1]
    route = router_pad is not None
    off = row0 // tm

    def row(i):
        return (i, 0)

    def seq(i):
        return (i // nblk, 0, 0)

    def const2(i):
        return (0, 0)

    mod_spec = pl.BlockSpec((1, mod_rows, D_MODEL), seq)
    in_specs = [pl.BlockSpec((tm, D_MODEL), row), mod_spec, pl.BlockSpec((tm, V_A), row),
                pl.BlockSpec((tm, QKV_B), row) if ofox.ndim == 2 else
                pl.BlockSpec((1, H_B, tm, D_B), lambda i: (i // nblk, 0, i % nblk, 0)),
                pl.BlockSpec((1, H_C, tm, KV_LORA), lambda i: (i, 0, 0, 0)),
                pl.BlockSpec((H_C, KV_LORA, V_A), lambda i: (0, 0, 0)),
                pl.BlockSpec((D_MODEL, D_MODEL), const2),
                mod_spec, mod_spec, pl.BlockSpec((1, D_MODEL), const2)]
    args = [x2, gate, ya, ofox, olat, wuv, wo, shift, scale, g_ffn]
    out_shape = [jax.ShapeDtypeStruct((n_tok, D_MODEL), F32), jax.ShapeDtypeStruct(hbuf.shape, F32)]
    out_specs = [pl.BlockSpec((tm, D_MODEL), row), _block_rows(tm, hbuf, lambda i: (off + i, 0))]
    if route:
        in_specs.append(pl.BlockSpec((D_MODEL, LANE), const2))
        args.append(router_pad)
        out_shape += [jax.ShapeDtypeStruct((n_tok, LANE), jnp.int32), jax.ShapeDtypeStruct((n_tok, LANE), F32)]
        out_specs += [pl.BlockSpec((tm, LANE), row), pl.BlockSpec((tm, LANE), row)]
    in_specs.append(pl.BlockSpec(memory_space=pl.ANY))
    args.append(hbuf)
    return pl.pallas_call(
        functools.partial(_mixout_kernel, precise=precise, route=route), grid=(n_tok // tm,),
        in_specs=in_specs, out_specs=tuple(out_specs), out_shape=tuple(out_shape),
        input_output_aliases={len(args) - 1: 1},
        compiler_params=_cparams(("arbitrary",)), name="mixout",
    )(*args)


def _ffn_input(x, sh_ref, sc_ref, g_ref, wr_ref, h_ref, ids_ref, gates_ref):
    h = (_rms(x) * g_ref[...]) * (1.0 + sc_ref[0]) + sh_ref[0]
    _store_rows(h_ref, h)
    if wr_ref is not None:
        logits = _dot_f32(h, wr_ref[...])
        lane = lax.broadcasted_iota(jnp.int32, logits.shape, 1)
        logits = jnp.where(lane < N_EXPERTS, logits, NEG)
        m1 = jnp.max(logits, axis=-1, keepdims=True)
        i1 = jnp.min(jnp.where(logits == m1, lane, LANE), axis=-1, keepdims=True)
        rest = jnp.where(lane == i1, NEG, logits)
        m2 = jnp.max(rest, axis=-1, keepdims=True)
        i2 = jnp.min(jnp.where(rest == m2, lane, LANE), axis=-1, keepdims=True)
        e2 = jnp.exp(m2 - m1)
        g1 = 1.0 / (1.0 + e2)
        g2 = e2 / (1.0 + e2)
        ids_ref[...] = jnp.where(lane == 0, i1, i2)
        gates_ref[...] = jnp.where(lane == 0, g1, g2)


GATHER_ROWS = 512
GATHER_UNROLL = 32


def _gather_kernel(idx_ref, src_ref, out_ref, sem):
    def issue(c, carry):
        base = pl.multiple_of(c * GATHER_UNROLL, GATHER_UNROLL)
        for r in range(GATHER_UNROLL):
            pltpu.make_async_copy(src_ref.at[idx_ref[0, 0, base + r]], out_ref.at[base + r], sem).start(priority=r % 2)
        return carry

    lax.fori_loop(0, GATHER_ROWS // GATHER_UNROLL, issue, 0)
    pltpu.make_async_copy(src_ref.at[pl.ds(0, GATHER_ROWS)], out_ref, sem).wait()


def _gather_rows(src, idx):
    m = idx.shape[0]
    return pl.pallas_call(
        _gather_kernel, grid=(m // GATHER_ROWS,),
        in_specs=[pl.BlockSpec((1, 1, GATHER_ROWS), lambda i: (i, 0, 0), memory_space=pltpu.SMEM),
                  pl.BlockSpec(memory_space=pl.ANY)],
        out_specs=pl.BlockSpec((GATHER_ROWS,) + src.shape[1:], lambda i: (i, 0, 0)),
        out_shape=jax.ShapeDtypeStruct((m,) + src.shape[1:], src.dtype),
        scratch_shapes=[pltpu.SemaphoreType.DMA(())],
        compiler_params=pltpu.CompilerParams(dimension_semantics=("arbitrary",)), name="gather_rows",
    )(idx.reshape(m // GATHER_ROWS, 1, GATHER_ROWS), src)


def _swiglu_kernel(te_ref, nt_ref, x_ref, wg_ref, wu_ref, wd_ref, o_ref, acc_ref, xs_ref, *, precise):
    mm = functools.partial(_mm, precise=precise)
    i = pl.program_id(0)
    j = pl.program_id(1)

    @pl.when(i < nt_ref[0])
    def _():
        @pl.when(j == 0)
        def _():
            acc_ref[...] = jnp.zeros(acc_ref.shape, F32)
            xs_ref[...] = _load_rows(x_ref).astype(xs_ref.dtype)

        x = xs_ref[...]
        tf = wg_ref.shape[2]
        cw = SWIGLU_COLS if tf % SWIGLU_COLS == 0 else tf
        part = None
        for c0 in range(0, tf, cw):
            a = mm(x, wg_ref[0, :, c0:c0 + cw])
            u = mm(x, wu_ref[0, :, c0:c0 + cw])
            d = mm(_silu(a) * u, wd_ref[0, c0:c0 + cw, :])
            part = d if part is None else part + d
        acc_ref[...] += part

        @pl.when(j == pl.num_programs(1) - 1)
        def _():
            _store_rows(o_ref, acc_ref[...])

    @pl.when((i >= nt_ref[0]) & (j == pl.num_programs(1) - 1))
    def _():
        o_ref[...] = jnp.zeros(o_ref.shape, F32)


def _swiglu_grouped(x, tile_expert, n_tiles_used, wg, wu, wd, tm, tf, precise=False):
    m = x.shape[0] // SUBLANE if _is_tiled(x.shape) else x.shape[0]
    f = wg.shape[2]
    grid_spec = pltpu.PrefetchScalarGridSpec(
        num_scalar_prefetch=2, grid=(m // tm, f // tf),
        in_specs=[_block_rows(tm, x, lambda i, j, te, nt: (i, 0)),
                  pl.BlockSpec((1, D_MODEL, tf), lambda i, j, te, nt: (te[i], 0, j)),
                  pl.BlockSpec((1, D_MODEL, tf), lambda i, j, te, nt: (te[i], 0, j)),
                  pl.BlockSpec((1, tf, D_MODEL), lambda i, j, te, nt: (te[i], j, 0))],
        out_specs=_block_rows(tm, x, lambda i, j, te, nt: (i, 0)),
        scratch_shapes=[pltpu.VMEM((tm, D_MODEL), F32), pltpu.VMEM((tm, D_MODEL), F32 if precise else BF16)])
    return pl.pallas_call(
        functools.partial(_swiglu_kernel, precise=precise), grid_spec=grid_spec,
        out_shape=jax.ShapeDtypeStruct(x.shape, F32),
        compiler_params=_cparams(("arbitrary", "arbitrary")), name="swiglu",
    )(tile_expert, n_tiles_used, x, wg, wu, wd)


def _combine_kernel(*refs, moe, final):
    refs = list(refs)
    x_ref = refs.pop(0)
    gate_ref = refs.pop(0)
    y1_ref = refs.pop(0)
    if moe:
        y2_ref = refs.pop(0)
        gates_ref = refs.pop(0)
    fg_ref = refs.pop(0) if final else None
    o_ref = refs.pop(0)
    y = _load_rows(y1_ref)
    if moe:
        gts = gates_ref[...]
        y = gts[:, 0:1] * y + gts[:, 1:2] * _load_rows(y2_ref)
    out = x_ref[...] + gate_ref[0] * y
    if final:
        out = _rms(out) * fg_ref[...]
    o_ref[...] = out


def _combine(x2, gate, y, y_row0, y2_row0, gates, final_g, n_seq, t_len, tm):
    nblk = t_len // tm
    n_tok = n_seq * t_len
    moe = gates is not None
    final = final_g is not None

    def row(i):
        return (i, 0)

    in_specs = [pl.BlockSpec((tm, D_MODEL), row), pl.BlockSpec((1, gate.shape[1], D_MODEL), lambda i: (i // nblk, 0, 0)),
                _block_rows(tm, y, lambda i: (y_row0 // tm + i, 0))]
    args = [x2, gate, y]
    if moe:
        in_specs += [_block_rows(tm, y, lambda i: (y2_row0 // tm + i, 0)), pl.BlockSpec((tm, LANE), row)]
        args += [y, gates]
    if final:
        in_specs.append(pl.BlockSpec((1, D_MODEL), lambda i: (0, 0)))
        args.append(final_g)
    return pl.pallas_call(
        functools.partial(_combine_kernel, moe=moe, final=final), grid=(n_tok // tm,),
        in_specs=in_specs, out_specs=pl.BlockSpec((tm, D_MODEL), row),
        out_shape=jax.ShapeDtypeStruct((n_tok, D_MODEL), F32),
        compiler_params=_cparams(("arbitrary",)), name="combine",
    )(*args)


def _pack_mixer_weights(w_in, w_gate2, b_gate, fox_b_f, qng, kvng, w_uq, w_uk, w_uv, gla_norm_g):
    offs = np.concatenate([[0], np.cumsum(IN_SPLITS)])
    cols = {n: (int(offs[i]), int(offs[i + 1])) for i, n in enumerate(
        ("gq", "gk", "gv", "ag", "ar", "fq", "fk", "fv", "bf", "cq", "ckv", "kr"))}

    def seg(n):
        return w_in[:, cols[n][0]:cols[n][1]]

    half = ROPE_C // 2
    kr = seg("kr")

    def zcols(n):
        return jnp.zeros((D_MODEL, n), F32)

    small = jnp.concatenate([kr, seg("ar"), seg("bf"), zcols(SM_KRS - SM_BF - H_B), -kr[:, half:], kr[:, :half],
                             zcols(LANE - SM_KRS - ROPE_C)], axis=1)
    w = jnp.concatenate([seg("gq"), zcols(C_GK - QK_A), seg("gk"), zcols(C_GV - C_GK - QK_A), seg("gv"), seg("ag"),
                         seg("fq"), seg("fk"), seg("fv"), seg("cq"), seg("ckv"), small], axis=1)
    assert w.shape == (D_MODEL, N_PACK) and small.shape[1] == LANE
    wg2 = jnp.pad(w_gate2, ((SM_AR, LANE - SM_AR - GATE_RANK), (0, 2 * LANE - QK_A)))
    bg = jnp.pad(b_gate, (0, 2 * LANE - QK_A)).reshape(1, 2 * LANE)
    bf = jnp.pad(fox_b_f, (SM_BF, LANE - SM_BF - H_B)).reshape(1, LANE)
    uq = w_uq.reshape(Q_LORA, H_C, NOPE_C + ROPE_C)
    wuqn = uq[:, :, :NOPE_C].reshape(Q_LORA, H_C * NOPE_C)
    x1 = uq[:, :, NOPE_C:NOPE_C + half]
    x2 = uq[:, :, NOPE_C + half:]
    pad = jnp.zeros((Q_LORA, H_C, LANE - ROPE_C), F32)
    wr = jnp.concatenate([x1, x2, pad], axis=2).reshape(Q_LORA, H_C * LANE)
    wrs = jnp.concatenate([-x2, x1, pad], axis=2).reshape(Q_LORA, H_C * LANE)
    eye = jnp.eye(H_C, dtype=F32)
    wuk = (jnp.transpose(w_uk, (1, 2, 0))[:, :, None, :] * eye[:, None, :, None]).reshape(H_C * NOPE_C, H_C * KV_LORA)
    wuv = (jnp.transpose(w_uv, (1, 0, 2))[:, :, None, :] * eye[:, None, :, None]).reshape(H_C, KV_LORA, V_A)
    full = dict(w_in=w, wg2=wg2, bg=bg, bf=bf, qng=qng.reshape(1, Q_LORA), kvng=kvng.reshape(1, KV_LORA), wuqn=wuqn,
                wuk=wuk, wr=wr, wrs=wrs, wuv=wuv, gnorm=jnp.tile(gla_norm_g, H_A).reshape(1, V_A))
    half_prec = dict(full)
    for n in ("w_in", "wg2", "wuqn", "wuk", "wr", "wrs", "wuv"):
        half_prec[n] = full[n].astype(BF16)
    return half_prec, full


def _rope_tables(pos):
    half = ROPE_C // 2
    inv_freq = ROPE_BASE ** (-jnp.arange(half, dtype=F32) / half)
    ang = pos.astype(F32)[:, None] * inv_freq[None, :]
    n = pos.shape[0]
    cos = jnp.concatenate([jnp.cos(ang), jnp.cos(ang), jnp.ones((n, LANE - ROPE_C), F32)], axis=1)
    sin = jnp.concatenate([jnp.sin(ang), jnp.sin(ang), jnp.zeros((n, LANE - ROPE_C), F32)], axis=1)
    return cos, sin


def _state_to_t(s):
    b = s.shape[0]
    eye = jnp.eye(H_A, dtype=F32)
    s_vk = jnp.swapaxes(s, 2, 3)
    return (s_vk[:, :, :, None, :] * eye[None, :, None, :, None]).reshape(b, V_A, QK_A)


def _state_from_t(s_t):
    b = s_t.shape[0]
    blocks = s_t.reshape(b, H_A, DV_A, H_A, DK_A)
    diag = jnp.stack([blocks[:, hh, :, hh, :] for hh in range(H_A)], axis=1)
    return jnp.swapaxes(diag, 2, 3)


def _round_up(a, b):
    return (a + b - 1) // b * b


def kernel(x_prompt, x_sample, c_prompt, c_sample, cache_fox_k, cache_fox_v, cache_fox_logf, cache_mla_ckv, cache_mla_krope, state_gla, ada_w, ada_b, norm_mix_g, norm_ffn_g, w_in, gla_w_gate2, gla_b_gate, gla_norm_g, fox_b_f, mla_q_norm_g, mla_kv_norm_g, mla_w_uq, mla_w_uk, mla_w_uv, w_out, ffn_w_gate, ffn_w_up, ffn_w_down, moe_router, moe_w_gate, moe_w_up, moe_w_down, final_norm_g):
    bp, tp, _ = x_prompt.shape
    bs, ts, _ = x_sample.shape
    past = cache_fox_k.shape[2]
    np_tok, ns_tok = bp * tp, bs * ts
    n_all = np_tok + ns_tok

    tm_p, tm_s = 256, ns_tok
    tq_fox, tk = 512, 512

    nc = _round_up(bp + bs, 8)
    c_all = jnp.zeros((nc, D_MODEL), F32).at[:bp].set(c_prompt).at[bp:bp + bs].set(c_sample)
    mod = _ada(c_all, ada_w, ada_b)

    cos_p, sin_p = _rope_tables(jnp.arange(tp))
    cos_s, sin_s = _rope_tables(past + jnp.tile(jnp.arange(ts), bs))

    def seq_major(a):
        hh, d = a.shape[1], a.shape[3]
        return jnp.transpose(a.reshape(hh, bs, ts, d), (1, 0, 2, 3))

    def tok_major(a):
        hh, d = a.shape[1], a.shape[3]
        return jnp.transpose(a, (1, 0, 2, 3)).reshape(1, hh, bs * ts, d)

    cl = jnp.transpose(cache_fox_logf.astype(F32), (0, 1, 3, 2)).reshape(DEPTH * bs * H_B, past)
    f_cache, f_cache2 = _cumsum(cl, jnp.zeros((cl.shape[0], 1), F32), 512)
    f_cache = f_cache.reshape(DEPTH, bs * H_B, past)
    f_cache2 = f_cache2.reshape(DEPTH, bs, H_B, past // tk, tk)
    cache_k2d = cache_fox_k.astype(F32).reshape(DEPTH, bs, past, H_B * D_B)
    cache_v2d = cache_fox_v.astype(F32).reshape(DEPTH, bs, past, H_B * D_B)

    xp = x_prompt.reshape(np_tok, D_MODEL)
    xs = x_sample.reshape(ns_tok, D_MODEL)
    p_states = [[] for _ in range(6)]
    s_states = [[] for _ in range(6)]

    for l in range(DEPTH):
        mods = [mod[l, :, i * D_MODEL:(i + 1) * D_MODEL] for i in range(6)]
        mp = [m[:bp].reshape(bp, 1, D_MODEL) for m in mods]
        ms = [jnp.repeat(m[bp:bp + bs], ts, axis=0).reshape(1, ns_tok, D_MODEL) for m in mods]
        pw, pw32 = _pack_mixer_weights(w_in[l], gla_w_gate2[l], gla_b_gate[l], fox_b_f[l], mla_q_norm_g[l],
                                       mla_kv_norm_g[l], mla_w_uq[l], mla_w_uk[l], mla_w_uv[l], gla_norm_g[l])
        wo = w_out[l].astype(BF16)
        g_mix = norm_mix_g[l].reshape(1, D_MODEL)
        g_ffn = norm_ffn_g[l].reshape(1, D_MODEL)
        moe = l % 2 == 1
        if moe:
            router_pad = jnp.pad(moe_router[l // 2], ((0, 0), (0, LANE - N_EXPERTS)))
            hbuf_p, hbuf_s = jnp.zeros((n_all * SUBLANE, LANE), F32), None
        else:
            router_pad = None
            hbuf_p, hbuf_s = jnp.zeros((np_tok, D_MODEL), F32), jnp.zeros((ns_tok, D_MODEL), F32)

        (gq, gk, gv, ag, gla, fk, fv, ckv, kc, small, qs, fqh, fkh, fvh) = _inproj(
            xp, mp[0], mp[1], g_mix, pw, cos_p, sin_p, bp, tp, tm_p, False, True)
        logf = small[:, SM_BF:SM_BF + H_B]
        krope = small[:, SM_KR:SM_KR + ROPE_C]
        ya, s_t = _gla(gq, gk, gv, gla, ag, pw["gnorm"], jnp.zeros((bp, V_A, QK_A), F32), bp, tp, 512, CHUNK, bp, False)
        g_fox = bp * H_B
        fqa, fka = _fox_prep(small, fqh, fkh, bp, tp, 512)
        o_fox = _flash(fqa.reshape(g_fox, tp // tq_fox, tq_fox, 2 * D_B), fka.reshape(g_fox, tp, 2 * D_B),
                       fvh.reshape(g_fox, tp, 2 * D_B), None, None,
                       tq=tq_fox, rep=1, tk=tk, q0=0, mode="causal", kv_len=tp, dv=D_B, hp=H_B)
        o_mla = _flash(qs.reshape(bp, tp // tm_p, H_C * tm_p, 2 * LANE), kc.reshape(bp, tp, 2 * LANE), None, None, None,
                       tq=tm_p, rep=H_C, tk=tk, q0=0, mode="chunk", kv_len=tp, dv=KV_LORA)
        res = _mixout(xp, mp[2], ya, o_fox.reshape(bp, H_B, tp, D_B), o_mla.reshape(np_tok // tm_p, H_C, tm_p, KV_LORA),
                      pw["wuv"], wo, mp[3], mp[4], g_ffn, router_pad, hbuf_p, 0, bp, tp, tm_p, False)
        xp, h_p = res[0], res[1]
        for i, st in enumerate((_state_from_t(s_t), fk.reshape(bp, tp, H_B, D_B), fv.reshape(bp, tp, H_B, D_B),
                                logf.reshape(bp, tp, H_B), ckv.reshape(bp, tp, KV_LORA), krope.reshape(bp, tp, ROPE_C))):
            p_states[i].append(st)

        (gq, gk, gv, ag, gla, fk, fv, ckv, kc, small, qs, fq) = _inproj(
            xs, ms[0], ms[1], g_mix, pw32, cos_s, sin_s, 1, ns_tok, tm_s, True, False)
        logf = small[:, SM_BF:SM_BF + H_B]
        krope = small[:, SM_KR:SM_KR + ROPE_C]
        g_fox = bs * H_B
        f_rows = jnp.transpose(logf.reshape(bs, ts, H_B), (0, 2, 1)).reshape(g_fox, ts)
        f_new = _cumsum(f_rows, f_cache[l][:, past - 1:past], ts)[1]
        fq_col = jnp.transpose(f_new.reshape(bs, H_B, ts), (0, 2, 1))
        ya, s_t = _gla(gq, gk, gv, gla, ag, pw["gnorm"], _state_to_t(state_gla[l].astype(F32)), bs, ts, ts, ts, 4, True)
        o_fox = _fox_decode(fq, cache_k2d, cache_v2d, l, fk, fv, fq_col, f_cache2, f_new, bs, ts, tk)
        o_mla = _mla_decode(seq_major(qs).reshape(bs, H_C * ts, 2 * LANE), cache_mla_ckv.astype(F32),
                            cache_mla_krope.astype(F32), l, ckv, kc, bs, ts, tk)
        res_s = _mixout(xs, ms[2], ya, o_fox, tok_major(o_mla.reshape(bs, H_C, ts, KV_LORA)), pw32["wuv"], w_out[l],
                        ms[3], ms[4], g_ffn, router_pad, h_p if moe else hbuf_s, np_tok if moe else 0,
                        1, ns_tok, tm_s, True)
        xs, h_s = res_s[0], res_s[1]
        for i, st in enumerate((_state_from_t(s_t), fk.reshape(bs, ts, H_B, D_B), fv.reshape(bs, ts, H_B, D_B),
                                logf.reshape(bs, ts, H_B), ckv.reshape(bs, ts, KV_LORA), krope.reshape(bs, ts, ROPE_C))):
            s_states[i].append(st)

        last = l == DEPTH - 1
        fg = final_norm_g.reshape(1, D_MODEL) if last else None
        tm_f = 512
        if l % 2 == 0:
            j = l // 2
            n_tiles = np_tok // tm_f
            y_p = _swiglu_grouped(h_p, jnp.zeros((n_tiles,), jnp.int32), jnp.full((1,), n_tiles, jnp.int32),
                                  ffn_w_gate[j:j + 1].astype(BF16), ffn_w_up[j:j + 1].astype(BF16),
                                  ffn_w_down[j:j + 1].astype(BF16), tm_f, ffn_w_gate.shape[2])
            y_s = _swiglu_grouped(h_s, jnp.zeros((1,), jnp.int32), jnp.ones((1,), jnp.int32),
                                  ffn_w_gate[j:j + 1], ffn_w_up[j:j + 1], ffn_w_down[j:j + 1], tm_s, 1408, precise=True)
            xp = _combine(xp, mp[5], y_p, 0, 0, None, fg, bp, tp, tm_p)
            xs = _combine(xs, ms[5], y_s, 0, 0, None, fg, 1, ns_tok, tm_s)
        else:
            j = l // 2
            h_all = h_s
            ids_p, gates_p = res[2:]
            ids_s, gates_s = res_s[2:]
            ids = jnp.concatenate([ids_p[:, :2], ids_s[:, :2]], axis=0)
            e = jnp.transpose(ids).reshape(-1)
            onehot = (e[:, None] == jnp.arange(N_EXPERTS)[None, :]).astype(jnp.int32)
            rank = jnp.sum((jnp.cumsum(onehot, axis=0) - onehot) * onehot, axis=1)
            counts = jnp.sum(onehot, axis=0)
            padded = (counts + tm_f - 1) // tm_f * tm_f
            ends = jnp.cumsum(padded)
            starts = ends - padded
            pos = starts[e] + rank
            m_pad = _round_up(2 * n_all + N_EXPERTS * (tm_f - 1), tm_f)
            token = jnp.tile(jnp.arange(n_all, dtype=jnp.int32), 2)
            src = jnp.zeros((m_pad,), jnp.int32).at[pos].set(token)
            n_tiles = m_pad // tm_f
            tile_row0 = jnp.arange(n_tiles, dtype=jnp.int32) * tm_f
            tile_expert = jnp.minimum(jnp.sum((ends[None, :] <= tile_row0[:, None]).astype(jnp.int32), axis=1),
                                      N_EXPERTS - 1)
            n_used = (ends[-1] // tm_f).astype(jnp.int32).reshape(1)
            x_sorted = _gather_rows(_untiled(h_all), src)
            y = _swiglu_grouped(_tiled(x_sorted), tile_expert, n_used, moe_w_gate[j].astype(BF16),
                                moe_w_up[j].astype(BF16), moe_w_down[j].astype(BF16), tm_f, 1792)
            n_back = _round_up(n_all, GATHER_ROWS)
            back = jnp.zeros((2 * n_back,), jnp.int32).at[:n_all].set(pos[:n_all]).at[n_back:n_back + n_all].set(pos[n_all:])
            yg = _tiled(_gather_rows(_untiled(y), back))
            xp = _combine(xp, mp[5], yg, 0, n_back, gates_p, fg, bp, tp, tm_p)
            xs = _combine(xs, ms[5], yg, np_tok, n_back + np_tok, gates_s, fg, 1, ns_tok, tm_s)

    outs_p = [jnp.stack(s, axis=0) for s in p_states]
    outs_s = [jnp.stack(s, axis=0) for s in s_states]
    return (xp.reshape(bp, tp, D_MODEL), xs.reshape(bs, ts, D_MODEL), *outs_p, *outs_s)
```

```python
import functools

import numpy as np
import jax
import jax.numpy as jnp
from jax import lax
from jax.experimental import pallas as pl
from jax.experimental.pallas import tpu as pltpu

F32 = jnp.float32
BF16 = jnp.bfloat16
HIGHEST = lax.Precision.HIGHEST

D_MODEL = 1024
DEPTH = 2
CHUNK = 64
EPS = 1e-6
H_A, DK_A, DV_A = 6, 32, 64
GATE_RANK = 16
GATE_TAU = 16.0
H_B, D_B = 4, 64
H_C, NOPE_C, ROPE_C, V_C = 6, 64, 32, 64
Q_LORA, KV_LORA = 256, 128
ROPE_BASE = 10000.0
N_EXPERTS = 8
IN_SPLITS = (H_A * DK_A, H_A * DK_A, H_A * DV_A, H_A * DV_A, GATE_RANK,
             H_B * D_B, H_B * D_B, H_B * D_B, H_B, Q_LORA, KV_LORA, ROPE_C)

QK_A = H_A * DK_A
V_A = H_A * DV_A
QKV_B = H_B * D_B
LANE = 128
NEG = -1e30

C_GQ, C_GK, C_GV, C_AG = 0, 256, 512, 896
C_FQ, C_FK, C_FV = 1280, 1536, 1792
C_CQ, C_CKV, C_SM = 2048, 2304, 2432
N_PACK = 2560
SM_KR, SM_AR, SM_BF, SM_KRS = 0, 32, 48, 64

GLA_SUB = 16
FLASH_ROW_GROUP = 256
SWIGLU_COLS = 256
LOG2E = float(np.log2(np.e))
VMEM_LIMIT = 56 * 1024 * 1024


def _cparams(sem):
    return pltpu.CompilerParams(dimension_semantics=sem, vmem_limit_bytes=VMEM_LIMIT)


def _log_sigmoid(z):
    return jnp.minimum(z, 0.0) - jnp.log1p(jnp.exp(-jnp.abs(z)))


def _silu(z):
    return z * (1.0 / (1.0 + jnp.exp(-z)))


def _rms(x):
    return x * lax.rsqrt(jnp.mean(x * x, axis=-1, keepdims=True) + EPS)


def _dot(a, b):
    return jnp.dot(a, b, preferred_element_type=F32)


def _dot_nt(a, b):
    return lax.dot_general(a, b, (((1,), (1,)), ((), ())), preferred_element_type=F32)


def _dot_tn(a, b):
    return lax.dot_general(a, b, (((0,), (0,)), ((), ())), preferred_element_type=F32)


def _split2(a):
    a = a.astype(F32)
    hi = a.astype(BF16)
    return hi, (a - hi.astype(F32)).astype(BF16)


def _dot_f32(a, b, dims=(((1,), (0,)), ((), ()))):
    a_hi, a_lo = _split2(a)
    b_hi, b_lo = _split2(b)

    def dg(x, y):
        return lax.dot_general(x, y, dims, preferred_element_type=F32)

    return dg(a_hi, b_hi) + (dg(a_hi, b_lo) + dg(a_lo, b_hi))


def _mm(a, b, dims=(((1,), (0,)), ((), ())), *, precise):
    if precise:
        return _dot_f32(a, b, dims)
    return lax.dot_general(a.astype(BF16), b.astype(BF16), dims, preferred_element_type=F32)


_NT = (((1,), (1,)), ((), ()))
_TN = (((0,), (0,)), ((), ()))

SUBLANE = 8
assert D_MODEL == SUBLANE * LANE


def _is_tiled(shape):
    return len(shape) == 2 and shape[1] == LANE


def _tiled(a):
    return a.reshape(a.shape[0] * SUBLANE, LANE)


def _untiled(a):
    return a.reshape(a.shape[0] // SUBLANE, SUBLANE, LANE)


def _block_rows(tm, arr, index_map):
    if _is_tiled(arr.shape):
        return pl.BlockSpec((tm * SUBLANE, LANE), index_map)
    return pl.BlockSpec((tm, D_MODEL), index_map)


def _load_rows(ref):
    if _is_tiled(ref.shape):
        tm = ref.shape[0] // SUBLANE
        return jnp.concatenate([ref[pl.ds(s, tm, stride=SUBLANE), :] for s in range(SUBLANE)], axis=-1)
    return ref[...]


def _store_rows(ref, val):
    if _is_tiled(ref.shape):
        tm = ref.shape[0] // SUBLANE
        for s in range(SUBLANE):
            ref[pl.ds(s, tm, stride=SUBLANE), :] = val[:, s * LANE:(s + 1) * LANE]
    else:
        ref[...] = val


def _ada_kernel(c_ref, w_ref, b_ref, o_ref):
    s = _silu(c_ref[...])
    o_ref[0] = jnp.dot(s, w_ref[0], precision=HIGHEST, preferred_element_type=F32) + b_ref[0]


def _ada(c_all, ada_w, ada_b):
    nc = c_all.shape[0]
    tn = 1536
    return pl.pallas_call(
        _ada_kernel,
        grid=(DEPTH, 6 * D_MODEL // tn),
        in_specs=[pl.BlockSpec((nc, D_MODEL), lambda l, j: (0, 0)),
                  pl.BlockSpec((1, D_MODEL, tn), lambda l, j: (l, 0, j)),
                  pl.BlockSpec((1, 1, tn), lambda l, j: (l, 0, j))],
        out_specs=pl.BlockSpec((1, nc, tn), lambda l, j: (l, 0, j)),
        out_shape=jax.ShapeDtypeStruct((DEPTH, nc, 6 * D_MODEL), F32),
        compiler_params=_cparams(("arbitrary", "arbitrary")),
        name="ada",
    )(c_all, ada_w, ada_b.reshape(DEPTH, 1, 6 * D_MODEL))


def _inproj_kernel(x_ref, sh_ref, sc_ref, g_ref, w_ref, wg2_ref, bg_ref, bf_ref, qng_ref, kvng_ref,
                   wuqn_ref, wuk_ref, wr_ref, wrs_ref, cos_ref, sin_ref,
                   gq_ref, gk_ref, gv_ref, ag_ref, gla_ref, fk_ref, fv_ref,
                   ckv_ref, kc_ref, small_ref, qs_ref, *fox_refs, precise):
    mm = functools.partial(_mm, precise=precise)
    act = kc_ref.dtype
    x = x_ref[...]
    h = (_rms(x) * g_ref[...]) * (1.0 + sc_ref[0]) + sh_ref[0]
    p = mm(h, w_ref[...])

    gq_ref[...] = p[:, C_GQ:C_GQ + QK_A] * (DK_A ** -0.5)
    gk_ref[...] = p[:, C_GK:C_GK + QK_A]
    gv_ref[...] = p[:, C_GV:C_GV + V_A]
    ag_ref[...] = p[:, C_AG:C_AG + V_A]
    sm = p[:, C_SM:C_SM + LANE]
    z = mm(sm, wg2_ref[...]) + bg_ref[...]
    gla_ref[...] = _log_sigmoid(z[:, :QK_A]) * (1.0 / GATE_TAU)

    fq = p[:, C_FQ:C_FQ + QKV_B] * (D_B ** -0.5 * LOG2E)
    fk = p[:, C_FK:C_FK + QKV_B]
    fv = p[:, C_FV:C_FV + QKV_B]
    fk_ref[...] = fk
    fv_ref[...] = fv
    if len(fox_refs) == 1:
        fox_refs[0][...] = fq
    else:
        fqh_ref, fkh_ref, fvh_ref = fox_refs
        for hh in range(H_B):
            sl = slice(hh * D_B, (hh + 1) * D_B)
            fqh_ref[0, hh] = fq[:, sl].astype(act)
            fkh_ref[0, hh] = fk[:, sl].astype(act)
            fvh_ref[0, hh, :, :D_B] = fv[:, sl].astype(act)
            fvh_ref[0, hh, :, D_B:] = jnp.ones((fv.shape[0], D_B), act)

    cos = cos_ref[...]
    sin = sin_ref[...]
    lane = lax.broadcasted_iota(jnp.int32, sm.shape, 1)
    kr = sm * cos + pltpu.roll(sm, LANE - SM_KRS, 1) * sin
    logf = _log_sigmoid(sm + bf_ref[...])
    small_ref[...] = jnp.where((lane >= SM_BF) & (lane < SM_BF + H_B), logf, kr)

    ckv = _rms(p[:, C_CKV:C_CKV + KV_LORA]) * kvng_ref[...]
    ckv_ref[...] = ckv
    kc_ref[:, :KV_LORA] = ckv.astype(act)
    kc_ref[:, KV_LORA:] = jnp.where(lane < ROPE_C, kr, 0.0).astype(act)
    cqn = _rms(p[:, C_CQ:C_CQ + Q_LORA]) * qng_ref[...]
    nope = mm(cqn, wuqn_ref[...])
    qlat = mm(nope, wuk_ref[...])
    qa = mm(cqn, wr_ref[...])
    qb = mm(cqn, wrs_ref[...])
    scale = (NOPE_C + ROPE_C) ** -0.5 * LOG2E
    for hh in range(H_C):
        sl = slice(hh * LANE, (hh + 1) * LANE)
        qs_ref[0, hh, :, :KV_LORA] = (qlat[:, sl] * scale).astype(act)
        qs_ref[0, hh, :, KV_LORA:] = ((qa[:, sl] * cos + qb[:, sl] * sin) * scale).astype(act)


def _inproj(x2, shift, scale, g, pw, cos_tab, sin_tab, n_seq, t_len, tm, precise, fox_head_major):
    n_tok = n_seq * t_len
    nblk = t_len // tm
    act = F32 if precise else BF16
    mod_rows = shift.shape[1]

    def row(i):
        return (i, 0)

    def seq(i):
        return (i // nblk, 0, 0)

    def const2(i):
        return (0, 0)

    def tab(i):
        return (i % nblk, 0)

    def headmajor(i):
        return (i // nblk, 0, i % nblk, 0)

    sds = jax.ShapeDtypeStruct
    out_shape = (
        sds((n_tok, QK_A), F32), sds((n_tok, QK_A), F32), sds((n_tok, V_A), F32), sds((n_tok, V_A), F32),
        sds((n_tok, QK_A), F32),
        sds((n_tok, QKV_B), F32), sds((n_tok, QKV_B), F32),
        sds((n_tok, KV_LORA), F32), sds((n_tok, 2 * LANE), act), sds((n_tok, LANE), F32),
        sds((n_tok // tm, H_C, tm, 2 * LANE), act),
    )
    out_specs = (
        pl.BlockSpec((tm, QK_A), row), pl.BlockSpec((tm, QK_A), row), pl.BlockSpec((tm, V_A), row),
        pl.BlockSpec((tm, V_A), row), pl.BlockSpec((tm, QK_A), row),
        pl.BlockSpec((tm, QKV_B), row), pl.BlockSpec((tm, QKV_B), row),
        pl.BlockSpec((tm, KV_LORA), row), pl.BlockSpec((tm, 2 * LANE), row), pl.BlockSpec((tm, LANE), row),
        pl.BlockSpec((1, H_C, tm, 2 * LANE), lambda i: (i, 0, 0, 0)),
    )
    if fox_head_major:
        out_shape += (sds((n_seq, H_B, t_len, D_B), act), sds((n_seq, H_B, t_len, D_B), act),
                      sds((n_seq, H_B, t_len, 2 * D_B), act))
        out_specs += (pl.BlockSpec((1, H_B, tm, D_B), headmajor), pl.BlockSpec((1, H_B, tm, D_B), headmajor),
                      pl.BlockSpec((1, H_B, tm, 2 * D_B), headmajor))
    else:
        out_shape += (sds((n_tok, QKV_B), F32),)
        out_specs += (pl.BlockSpec((tm, QKV_B), row),)
    in_specs = [
        pl.BlockSpec((tm, D_MODEL), row), pl.BlockSpec((1, mod_rows, D_MODEL), seq),
        pl.BlockSpec((1, mod_rows, D_MODEL), seq), pl.BlockSpec((1, D_MODEL), const2),
        pl.BlockSpec((D_MODEL, N_PACK), const2), pl.BlockSpec((LANE, 2 * LANE), const2),
        pl.BlockSpec((1, 2 * LANE), const2), pl.BlockSpec((1, LANE), const2),
        pl.BlockSpec((1, Q_LORA), const2), pl.BlockSpec((1, KV_LORA), const2),
        pl.BlockSpec((Q_LORA, H_C * NOPE_C), const2), pl.BlockSpec((H_C * NOPE_C, H_C * KV_LORA), const2),
        pl.BlockSpec((Q_LORA, H_C * LANE), const2), pl.BlockSpec((Q_LORA, H_C * LANE), const2),
        pl.BlockSpec((tm, LANE), tab), pl.BlockSpec((tm, LANE), tab),
    ]
    return pl.pallas_call(
        functools.partial(_inproj_kernel, precise=precise), grid=(n_tok // tm,), in_specs=in_specs,
        out_specs=out_specs, out_shape=out_shape, compiler_params=_cparams(("arbitrary",)), name="inproj",
    )(x2, shift, scale, g, pw["w_in"], pw["wg2"], pw["bg"], pw["bf"], pw["qng"], pw["kvng"],
      pw["wuqn"], pw["wuk"], pw["wr"], pw["wrs"], cos_tab, sin_tab)


def _cumsum_kernel(x_ref, init_ref, o_ref, o2_ref, *, tb):
    n = x_ref.shape[1]
    upper = (lax.broadcasted_iota(jnp.int32, (tb, tb), 0) <= lax.broadcasted_iota(jnp.int32, (tb, tb), 1)).astype(F32)
    carry = init_ref[...]
    for j in range(n // tb):
        blk = jnp.dot(x_ref[:, j * tb:(j + 1) * tb], upper, precision=HIGHEST, preferred_element_type=F32) + carry
        o_ref[:, j * tb:(j + 1) * tb] = blk
        o2_ref[:, j * tb:(j + 1) * tb] = blk * LOG2E
        carry = blk[:, tb - 1:tb]


def _cumsum(x, init, tb):
    return pl.pallas_call(
        functools.partial(_cumsum_kernel, tb=tb),
        out_shape=(jax.ShapeDtypeStruct(x.shape, F32), jax.ShapeDtypeStruct(x.shape, F32)), name="cumsum",
    )(x, init)


def _fox_prep_kernel(sm_ref, q_ref, k_ref, qa_ref, ka_ref, carry_ref, *, tb):
    @pl.when(pl.program_id(1) == 0)
    def _():
        carry_ref[...] = jnp.zeros(carry_ref.shape, F32)

    tril = (lax.broadcasted_iota(jnp.int32, (tb, tb), 0) >= lax.broadcasted_iota(jnp.int32, (tb, tb), 1)).astype(BF16)
    x = sm_ref[...]
    x_hi = x.astype(BF16)
    x_mid = (x - x_hi.astype(F32)).astype(BF16)
    x_lo = (x - x_hi.astype(F32) - x_mid.astype(F32)).astype(BF16)
    cum = _dot(tril, x_hi) + _dot(tril, x_mid) + _dot(tril, x_lo) + carry_ref[...]
    carry_ref[...] = cum[tb - 1:tb]
    lane = lax.broadcasted_iota(jnp.int32, (tb, D_B), 1)
    for hh in range(H_B):
        f = jnp.broadcast_to(cum[:, SM_BF + hh:SM_BF + hh + 1] * LOG2E, (tb, D_B))
        hi = f.astype(BF16).astype(F32)
        mid = (f - hi).astype(BF16).astype(F32)
        lo = (f - hi - mid).astype(BF16).astype(F32)
        terms = jnp.where(lane % 3 == 0, hi, jnp.where(lane % 3 == 1, mid, lo))
        q_extra = jnp.where(lane < 3, terms, jnp.where(lane < 6, 1.0, 0.0))
        k_extra = jnp.where(lane < 3, 1.0, jnp.where(lane < 6, -terms, 0.0))
        qa_ref[0, hh, :, :D_B] = q_ref[0, hh]
        qa_ref[0, hh, :, D_B:] = q_extra.astype(BF16)
        ka_ref[0, hh, :, :D_B] = k_ref[0, hh]
        ka_ref[0, hh, :, D_B:] = k_extra.astype(BF16)


def _fox_prep(small, fqh, fkh, n_seq, t_len, tb):
    nblk = t_len // tb

    def hm(b, j):
        return (b, 0, j, 0)

    return pl.pallas_call(
        functools.partial(_fox_prep_kernel, tb=tb), grid=(n_seq, nblk),
        in_specs=[pl.BlockSpec((tb, LANE), lambda b, j: (b * nblk + j, 0)),
                  pl.BlockSpec((1, H_B, tb, D_B), hm), pl.BlockSpec((1, H_B, tb, D_B), hm)],
        out_specs=(pl.BlockSpec((1, H_B, tb, 2 * D_B), hm), pl.BlockSpec((1, H_B, tb, 2 * D_B), hm)),
        out_shape=(jax.ShapeDtypeStruct((n_seq, H_B, t_len, 2 * D_B), BF16),) * 2,
        scratch_shapes=[pltpu.VMEM((1, LANE), F32)],
        compiler_params=_cparams(("arbitrary", "arbitrary")), name="fox_prep",
    )(small, fqh, fkh)


def _gla_kernel(q_ref, k_ref, v_ref, la_ref, ag_ref, gn_ref, s0_ref, y_ref, sout_ref, s_ref, *, chunk, n_chunks,
                precise):
    mm = functools.partial(_mm, precise=precise)
    c = chunk
    nsub = c // GLA_SUB

    nb = q_ref.shape[0]

    @pl.when(pl.program_id(1) == 0)
    def _():
        s_ref[...] = s0_ref[...]

    lane_qk = lax.broadcasted_iota(jnp.int32, (GLA_SUB, QK_A), 1) // DK_A
    lane_v = lax.broadcasted_iota(jnp.int32, (GLA_SUB, V_A), 1) // DV_A
    bd = (lax.broadcasted_iota(jnp.int32, (V_A, QK_A), 0) // DV_A) == (lax.broadcasted_iota(jnp.int32, (V_A, QK_A), 1) // DK_A)
    tril = (lax.broadcasted_iota(jnp.int32, (c, c), 0) >= lax.broadcasted_iota(jnp.int32, (c, c), 1)).astype(F32)
    hm = (lax.broadcasted_iota(jnp.int32, (V_A, V_A), 0) // DV_A) == (lax.broadcasted_iota(jnp.int32, (V_A, V_A), 1) // DV_A)
    head_mean = jnp.where(hm, 1.0 / DV_A, 0.0).astype(F32)

    def cumsum_rows(la):
        if precise:
            return jnp.dot(tril, la, precision=HIGHEST, preferred_element_type=F32)
        hi = la.astype(BF16)
        lo = (la - hi.astype(F32)).astype(BF16)
        tb16 = tril.astype(BF16)
        return _dot(tb16, hi) + _dot(tb16, lo)

    def chunk_step(bb, r):
        q = q_ref[bb, pl.ds(r, c), :]
        k = k_ref[bb, pl.ds(r, c), :]
        v = v_ref[bb, pl.ds(r, c), :]
        la = la_ref[bb, pl.ds(r, c), :]
        b = cumsum_rows(la)
        s_t = s_ref[bb]
        vb = v if precise else v.astype(BF16)
        o_inter = mm(q * jnp.exp(b), s_t, _NT)
        outs = []
        for i in range(nsub):
            r0 = i * GLA_SUB
            r1 = r0 + GLA_SUB
            bi = b[r0 - 1:r0] if i > 0 else jnp.zeros((1, QK_A), F32)
            qi = q[r0:r1] * jnp.exp(b[r0:r1] - bi)
            kk = k[:r1] * jnp.exp(bi - b[:r1])
            qst = jnp.concatenate([jnp.where(lane_qk == hh, qi, 0.0) for hh in range(H_A)], axis=0)
            att = mm(qst, kk, _NT)
            t_idx = r0 + lax.broadcasted_iota(jnp.int32, att.shape, 0) % GLA_SUB
            s_idx = lax.broadcasted_iota(jnp.int32, att.shape, 1)
            att = jnp.where(s_idx <= t_idx, att, 0.0)
            oi = mm(att, vb[:r1])
            o = jnp.zeros((GLA_SUB, V_A), F32)
            for hh in range(H_A):
                o = o + jnp.where(lane_v == hh, oi[hh * GLA_SUB:(hh + 1) * GLA_SUB], 0.0)
            outs.append(o)
        o = jnp.concatenate(outs, axis=0) + o_inter if nsub > 1 else outs[0] + o_inter
        b_last = b[c - 1:c]
        kd = k * jnp.exp(b_last - b)
        s_ref[bb] = s_t * jnp.exp(b_last) + jnp.where(bd, mm(vb, kd, _TN), 0.0)
        ms = mm(o * o, head_mean)
        y = o * lax.rsqrt(ms + EPS) * gn_ref[...] * _silu(ag_ref[bb, pl.ds(r, c), :])
        y_ref[bb, pl.ds(r, c), :] = y.astype(y_ref.dtype)

    def body(ci, carry):
        r = pl.multiple_of(ci * c, c)
        for bb in range(nb):
            chunk_step(bb, r)
        return carry

    lax.fori_loop(0, n_chunks, body, 0)

    @pl.when(pl.program_id(1) == pl.num_programs(1) - 1)
    def _():
        sout_ref[...] = s_ref[...]


def _gla(gq, gk, gv, gla, ag, gnorm, s0_t, n_seq, t_len, tb, chunk, nb, precise):
    nblk = t_len // tb

    def row(b, j):
        return (b, j, 0)

    def st(b, j):
        return (b, 0, 0)

    def seq3(a):
        return a.reshape(n_seq, t_len, a.shape[-1])

    ya, s_t = pl.pallas_call(
        functools.partial(_gla_kernel, chunk=chunk, n_chunks=tb // chunk, precise=precise),
        grid=(n_seq // nb, nblk),
        in_specs=[pl.BlockSpec((nb, tb, QK_A), row), pl.BlockSpec((nb, tb, QK_A), row), pl.BlockSpec((nb, tb, V_A), row),
                  pl.BlockSpec((nb, tb, QK_A), row), pl.BlockSpec((nb, tb, V_A), row),
                  pl.BlockSpec((1, V_A), lambda b, j: (0, 0)), pl.BlockSpec((nb, V_A, QK_A), st)],
        out_specs=(pl.BlockSpec((nb, tb, V_A), row), pl.BlockSpec((nb, V_A, QK_A), st)),
        out_shape=(jax.ShapeDtypeStruct((n_seq, t_len, V_A), F32 if precise else BF16),
                   jax.ShapeDtypeStruct((n_seq, V_A, QK_A), F32)),
        scratch_shapes=[pltpu.VMEM((nb, V_A, QK_A), F32)],
        compiler_params=_cparams(("arbitrary", "arbitrary")), name="gla",
    )(seq3(gq), seq3(gk), seq3(gv), seq3(gla), seq3(ag), gnorm, s0_t)
    return ya.reshape(n_seq * t_len, V_A), s_t


def _flash_kernel(*refs, tq, rep, hp, tk, q0, mode, kv_len, dv, bias, v_from_k, precise):
    mm = functools.partial(_mm, precise=precise)
    refs = list(refs)
    q_ref = refs.pop(0)
    k_ref = refs.pop(0)
    v_ref = k_ref if v_from_k else refs.pop(0)
    fq_ref = refs.pop(0) if bias else None
    fk_ref = refs.pop(0) if bias else None
    o_ref, m_ref, acc_ref = refs
    i = pl.program_id(1)
    rows = rep * tq
    m_ref[...] = jnp.full(m_ref.shape, NEG, F32)
    acc_ref[...] = jnp.zeros(acc_ref.shape, F32)

    first_q = q0 + i * tq
    last_q = first_q + tq - 1
    if mode == "chunk":
        vis_all = (first_q // CHUNK) * CHUNK + CHUNK - 1
        vis_any = (last_q // CHUNK) * CHUNK + CHUNK - 1
    else:
        vis_all = first_q
        vis_any = last_q
    vis_all = jnp.minimum(vis_all, kv_len - 1)
    vis_any = jnp.minimum(vis_any, kv_len - 1)
    n_full = (vis_all + 1) // tk
    n_any = vis_any // tk + 1

    grp = min(rows, FLASH_ROW_GROUP)

    def step(jb, masked):
        for hh in range(hp):
            head_step(hh, jb, masked)

    def head_step(hh, jb, masked):
        k_start = pl.multiple_of(jb * tk, tk)
        k = k_ref[hh, pl.ds(k_start, tk), :]
        if v_from_k:
            v = jnp.where(lax.broadcasted_iota(jnp.int32, k.shape, 1) < dv, k, jnp.ones_like(k))
        else:
            v = v_ref[hh, pl.ds(k_start, tk), :]
        for g in range(rows // grp):
            rs = slice(g * grp, (g + 1) * grp)
            s = mm(q_ref[hh, 0, rs, :], k, _NT)
            if bias:
                s = s + fq_ref[hh, 0, rs, :] - fk_ref[hh, jb]
            rs = slice(hh * rows + g * grp, hh * rows + (g + 1) * grp)
            if masked:
                qpos = first_q + (g * grp + lax.broadcasted_iota(jnp.int32, (grp, tk), 0)) % tq
                kpos = k_start + lax.broadcasted_iota(jnp.int32, (grp, tk), 1)
                if mode == "chunk":
                    ok = (kpos // CHUNK) <= (qpos // CHUNK)
                else:
                    ok = kpos <= qpos
                ok = ok & (kpos < kv_len)
                s = jnp.where(ok, s, NEG)
            chunks = [s[:, c * LANE:(c + 1) * LANE] for c in range(tk // LANE)]
            smax = chunks[0]
            for ch in chunks[1:]:
                smax = jnp.maximum(smax, ch)
            m_prev = m_ref[rs, :]
            m_new = jnp.maximum(m_prev, jnp.max(smax, axis=-1, keepdims=True))
            alpha = jnp.exp2(m_prev - m_new)
            p = jnp.concatenate([jnp.exp2(ch - m_new) for ch in chunks], axis=1)
            acc = acc_ref[rs, :]
            alpha_w = alpha if acc.shape[1] == LANE else jnp.concatenate([alpha] * (acc.shape[1] // LANE), axis=1)
            acc_ref[rs, :] = alpha_w * acc + mm(p, v)
            m_ref[rs, :] = m_new

    def loop(lo, hi, masked):
        def body(jb, carry):
            step(jb, masked)
            return carry
        lax.fori_loop(lo, hi, body, 0)

    loop(0, n_full, False)
    loop(n_full, n_any, True)

    acc = acc_ref[...]
    if dv == LANE:
        out = acc[:, :dv] / acc[:, dv:]
    else:
        out = (acc / pltpu.roll(acc, LANE - dv, 1))[:, :dv]
    for hh in range(hp):
        o_ref[hh, 0] = out[hh * rows:(hh + 1) * rows].astype(o_ref.dtype)


def _flash(q, k, v, fq, fk, *, tq, rep, tk, q0, mode, kv_len, dv, hp=1, precise=False):
    g, nq, rows, dqk = q.shape
    t_k = k.shape[1]
    nk = t_k // tk
    bias = fq is not None
    v_from_k = v is None

    def qmap(b, i):
        return (b, i, 0, 0)

    def kmap(b, i):
        return (b, 0, 0)

    in_specs = [pl.BlockSpec((hp, 1, rows, dqk), qmap), pl.BlockSpec((hp, t_k, dqk), kmap)]
    args = [q, k]
    dva = dqk if v_from_k else v.shape[2]
    assert dva % LANE == 0 and dva > dv
    if not v_from_k:
        in_specs.append(pl.BlockSpec((hp, t_k, dva), kmap))
        args.append(v)
    if bias:
        in_specs += [pl.BlockSpec((hp, 1, rows, 1), qmap), pl.BlockSpec((hp, nk, 1, tk), lambda b, i: (b, 0, 0, 0))]
        args += [fq, fk.reshape(g, nk, 1, tk)]
    return pl.pallas_call(
        functools.partial(_flash_kernel, tq=tq, rep=rep, hp=hp, tk=tk, q0=q0, mode=mode, kv_len=kv_len, dv=dv,
                          bias=bias, v_from_k=v_from_k, precise=precise),
        grid=(g // hp, nq), in_specs=in_specs,
        out_specs=pl.BlockSpec((hp, 1, rows, dv), qmap),
        out_shape=jax.ShapeDtypeStruct((g, nq, rows, dv), F32 if precise else BF16),
        scratch_shapes=[pltpu.VMEM((hp * rows, LANE), F32), pltpu.VMEM((hp * rows, dva), F32)],
        compiler_params=_cparams(("arbitrary", "arbitrary")), name="flash_" + mode,
    )(*args)


_hdot = _dot_f32


def _online_softmax_step(s, v, m_ref, l_ref, acc_ref):
    m_prev = m_ref[...]
    m_new = jnp.maximum(m_prev, jnp.max(s, axis=-1, keepdims=True))
    alpha = jnp.exp2(m_prev - m_new)
    p = jnp.exp2(s - m_new)
    l_ref[...] = alpha * l_ref[...] + jnp.sum(p, axis=-1, keepdims=True)
    acc_ref[...] = alpha * acc_ref[...] + _hdot(p, v)
    m_ref[...] = m_new


def _fox_decode_kernel(q_ref, kc_ref, vc_ref, kn_ref, vn_ref, fq_ref, fkc_ref, fkn_ref, o_ref, m_ref, l_ref, acc_ref,
                       *, tk, past, ts):
    rows = H_B * ts
    q = q_ref[...]
    lane_head = lax.broadcasted_iota(jnp.int32, q.shape, 1) // D_B
    q_st = jnp.concatenate([jnp.where(lane_head == hh, q, 0.0) for hh in range(H_B)], axis=0)
    fq = fq_ref[0]
    fq_st = jnp.concatenate([fq[:, hh:hh + 1] for hh in range(H_B)], axis=0)
    m_ref[...] = jnp.full(m_ref.shape, NEG, F32)
    l_ref[...] = jnp.zeros(l_ref.shape, F32)
    acc_ref[...] = jnp.zeros(acc_ref.shape, F32)

    def cached(jb, carry):
        k0 = pl.multiple_of(jb * tk, tk)
        fk_st = jnp.concatenate([jnp.broadcast_to(fkc_ref[0, 0, hh, pl.ds(jb, 1), :], (ts, tk)) for hh in range(H_B)],
                                axis=0)
        s = _hdot(q_st, kc_ref[0, 0, pl.ds(k0, tk), :], _NT) + fq_st - fk_st
        _online_softmax_step(s, vc_ref[0, 0, pl.ds(k0, tk), :], m_ref, l_ref, acc_ref)
        return carry

    lax.fori_loop(0, past // tk, cached, 0)
    fkn = fkn_ref[0]
    fk_st = jnp.concatenate([jnp.broadcast_to(fkn[hh:hh + 1, :], (ts, ts)) for hh in range(H_B)], axis=0)
    s = _hdot(q_st, kn_ref[...], _NT) + fq_st - fk_st
    t_idx = lax.broadcasted_iota(jnp.int32, s.shape, 0) % ts
    s_idx = lax.broadcasted_iota(jnp.int32, s.shape, 1)
    _online_softmax_step(jnp.where(s_idx <= t_idx, s, NEG), vn_ref[...], m_ref, l_ref, acc_ref)
    res = acc_ref[...] / l_ref[...]
    out = jnp.zeros((ts, H_B * D_B), F32)
    for hh in range(H_B):
        out = out + jnp.where(lane_head == hh, res[hh * ts:(hh + 1) * ts], 0.0)
    o_ref[...] = out


def _fox_decode(q, cache_k, cache_v, layer, k_new, v_new, fq_col, f_cache, f_new, n_seq, ts, tk):
    past = cache_k.shape[2]
    width = H_B * D_B
    rows = H_B * ts

    def tok(b):
        return (b, 0)

    def cache(b):
        return (layer, b, 0, 0)

    return pl.pallas_call(
        functools.partial(_fox_decode_kernel, tk=tk, past=past, ts=ts), grid=(n_seq,),
        in_specs=[pl.BlockSpec((ts, width), tok), pl.BlockSpec((1, 1, past, width), cache),
                  pl.BlockSpec((1, 1, past, width), cache), pl.BlockSpec((ts, width), tok), pl.BlockSpec((ts, width), tok),
                  pl.BlockSpec((1, ts, H_B), lambda b: (b, 0, 0)),
                  pl.BlockSpec((1, 1, H_B, past // tk, tk), lambda b: (layer, b, 0, 0, 0)),
                  pl.BlockSpec((1, H_B, ts), lambda b: (b, 0, 0))],
        out_specs=pl.BlockSpec((ts, width), tok),
        out_shape=jax.ShapeDtypeStruct((n_seq * ts, width), F32),
        scratch_shapes=[pltpu.VMEM((rows, 1), F32), pltpu.VMEM((rows, 1), F32), pltpu.VMEM((rows, width), F32)],
        compiler_params=_cparams(("arbitrary",)), name="fox_decode",
    )(q, cache_k, cache_v, k_new, v_new, fq_col, f_cache, f_new.reshape(n_seq, H_B, ts))


def _mla_decode_kernel(q_ref, cc_ref, cr_ref, cn_ref, kn_ref, o_ref, m_ref, l_ref, acc_ref, *, tk, past, ts):
    q = q_ref[0]
    q_lat = q[:, :KV_LORA]
    q_rope = q[:, KV_LORA:KV_LORA + ROPE_C]
    m_ref[...] = jnp.full(m_ref.shape, NEG, F32)
    l_ref[...] = jnp.zeros(l_ref.shape, F32)
    acc_ref[...] = jnp.zeros(acc_ref.shape, F32)

    def cached(jb, carry):
        k0 = pl.multiple_of(jb * tk, tk)
        ck = cc_ref[0, 0, pl.ds(k0, tk), :]
        s = _hdot(q_lat, ck, _NT) + _hdot(q_rope, cr_ref[0, 0, pl.ds(k0, tk), :], _NT)
        _online_softmax_step(s, ck, m_ref, l_ref, acc_ref)
        return carry

    lax.fori_loop(0, past // tk, cached, 0)
    cn = cn_ref[...]
    s = _hdot(q_lat, cn, _NT) + _hdot(q_rope, kn_ref[:, KV_LORA:KV_LORA + ROPE_C], _NT)
    q_chunk = (past + lax.broadcasted_iota(jnp.int32, s.shape, 0) % ts) // CHUNK
    k_chunk = (past + lax.broadcasted_iota(jnp.int32, s.shape, 1)) // CHUNK
    _online_softmax_step(jnp.where(k_chunk <= q_chunk, s, NEG), cn, m_ref, l_ref, acc_ref)
    o_ref[0] = acc_ref[...] / l_ref[...]


def _mla_decode(q, cache_ckv, cache_krope, layer, ckv_new, kc_new, n_seq, ts, tk):
    past = cache_ckv.shape[2]
    rows = H_C * ts

    def tok(b):
        return (b, 0)

    def cache(b):
        return (layer, b, 0, 0)

    return pl.pallas_call(
        functools.partial(_mla_decode_kernel, tk=tk, past=past, ts=ts), grid=(n_seq,),
        in_specs=[pl.BlockSpec((1, rows, 2 * LANE), lambda b: (b, 0, 0)),
                  pl.BlockSpec((1, 1, past, KV_LORA), cache), pl.BlockSpec((1, 1, past, ROPE_C), cache),
                  pl.BlockSpec((ts, KV_LORA), tok), pl.BlockSpec((ts, 2 * LANE), tok)],
        out_specs=pl.BlockSpec((1, rows, KV_LORA), lambda b: (b, 0, 0)),
        out_shape=jax.ShapeDtypeStruct((n_seq, rows, KV_LORA), F32),
        scratch_shapes=[pltpu.VMEM((rows, 1), F32), pltpu.VMEM((rows, 1), F32), pltpu.VMEM((rows, KV_LORA), F32)],
        compiler_params=_cparams(("arbitrary",)), name="mla_decode",
    )(q, cache_ckv, cache_krope, ckv_new, kc_new)


def _mixout_kernel(*refs, precise, route):
    x_ref, gate_ref, ya_ref, of_ref, ol_ref, wuv_ref, wo_ref, sh_ref, sc_ref, g_ref = refs[:10]
    rest = list(refs[10:])
    wr_ref = rest.pop(0) if route else None
    rest.pop(0)
    o_ref, h_ref = rest[:2]
    ids_ref, gates_ref = rest[2:] if route else (None, None)
    mm = functools.partial(_mm, precise=precise)
    acc = mm(ya_ref[...], wo_ref[:V_A, :])
    if len(of_ref.shape) == 2:
        acc = acc + mm(of_ref[...], wo_ref[V_A:V_A + QKV_B, :])
    else:
        for hh in range(H_B):
            r0 = V_A + hh * D_B
            acc = acc + mm(of_ref[0, hh], wo_ref[r0:r0 + D_B, :])
    yc = mm(ol_ref[0, 0], wuv_ref[0])
    for hh in range(1, H_C):
        yc = yc + mm(ol_ref[0, hh], wuv_ref[hh])
    acc = acc + mm(yc, wo_ref[V_A + QKV_B:, :])
    x_new = x_ref[...] + gate_ref[0] * acc
    o_ref[...] = x_new
    _ffn_input(x_new, sh_ref, sc_ref, g_ref, wr_ref, h_ref, ids_ref, gates_ref)


def _mixout(x2, gate, ya, ofox, olat, wuv, wo, shift, scale, g_ffn, router_pad, hbuf, row0, n_seq, t_len, tm, precise):
    nblk = t_len // tm
    n_tok = n_seq * t_len
    mod_rows = gate.shape[1]
    route = router_pad is not None
    off = row0 // tm

    def row(i):
        return (i, 0)

    def seq(i):
        return (i // nblk, 0, 0)

    def const2(i):
        return (0, 0)

    mod_spec = pl.BlockSpec((1, mod_rows, D_MODEL), seq)
    in_specs = [pl.BlockSpec((tm, D_MODEL), row), mod_spec, pl.BlockSpec((tm, V_A), row),
                pl.BlockSpec((tm, QKV_B), row) if ofox.ndim == 2 else
                pl.BlockSpec((1, H_B, tm, D_B), lambda i: (i // nblk, 0, i % nblk, 0)),
                pl.BlockSpec((1, H_C, tm, KV_LORA), lambda i: (i, 0, 0, 0)),
                pl.BlockSpec((H_C, KV_LORA, V_A), lambda i: (0, 0, 0)),
                pl.BlockSpec((D_MODEL, D_MODEL), const2),
                mod_spec, mod_spec, pl.BlockSpec((1, D_MODEL), const2)]
    args = [x2, gate, ya, ofox, olat, wuv, wo, shift, scale, g_ffn]
    out_shape = [jax.ShapeDtypeStruct((n_tok, D_MODEL), F32), jax.ShapeDtypeStruct(hbuf.shape, F32)]
    out_specs = [pl.BlockSpec((tm, D_MODEL), row), _block_rows(tm, hbuf, lambda i: (off + i, 0))]
    if route:
        in_specs.append(pl.BlockSpec((D_MODEL, LANE), const2))
        args.append(router_pad)
        out_shape += [jax.ShapeDtypeStruct((n_tok, LANE), jnp.int32), jax.ShapeDtypeStruct((n_tok, LANE), F32)]
        out_specs += [pl.BlockSpec((tm, LANE), row), pl.BlockSpec((tm, LANE), row)]
    in_specs.append(pl.BlockSpec(memory_space=pl.ANY))
    args.append(hbuf)
    return pl.pallas_call(
        functools.partial(_mixout_kernel, precise=precise, route=route), grid=(n_tok // tm,),
        in_specs=in_specs, out_specs=tuple(out_specs), out_shape=tuple(out_shape),
        input_output_aliases={len(args) - 1: 1},
        compiler_params=_cparams(("arbitrary",)), name="mixout",
    )(*args)


def _ffn_input(x, sh_ref, sc_ref, g_ref, wr_ref, h_ref, ids_ref, gates_ref):
    h = (_rms(x) * g_ref[...]) * (1.0 + sc_ref[0]) + sh_ref[0]
    _store_rows(h_ref, h)
    if wr_ref is not None:
        logits = _dot_f32(h, wr_ref[...])
        lane = lax.broadcasted_iota(jnp.int32, logits.shape, 1)
        logits = jnp.where(lane < N_EXPERTS, logits, NEG)
        m1 = jnp.max(logits, axis=-1, keepdims=True)
        i1 = jnp.min(jnp.where(logits == m1, lane, LANE), axis=-1, keepdims=True)
        rest = jnp.where(lane == i1, NEG, logits)
        m2 = jnp.max(rest, axis=-1, keepdims=True)
        i2 = jnp.min(jnp.where(rest == m2, lane, LANE), axis=-1, keepdims=True)
        e2 = jnp.exp(m2 - m1)
        g1 = 1.0 / (1.0 + e2)
        g2 = e2 / (1.0 + e2)
        ids_ref[...] = jnp.where(lane == 0, i1, i2)
        gates_ref[...] = jnp.where(lane == 0, g1, g2)


GATHER_ROWS = 512
GATHER_UNROLL = 32


def _gather_kernel(idx_ref, src_ref, out_ref, sem):
    def issue(c, carry):
        base = pl.multiple_of(c * GATHER_UNROLL, GATHER_UNROLL)
        for r in range(GATHER_UNROLL):
            pltpu.make_async_copy(src_ref.at[idx_ref[0, 0, base + r]], out_ref.at[base + r], sem).start(priority=r % 2)
        return carry

    lax.fori_loop(0, GATHER_ROWS // GATHER_UNROLL, issue, 0)
    pltpu.make_async_copy(src_ref.at[pl.ds(0, GATHER_ROWS)], out_ref, sem).wait()


def _gather_rows(src, idx):
    m = idx.shape[0]
    return pl.pallas_call(
        _gather_kernel, grid=(m // GATHER_ROWS,),
        in_specs=[pl.BlockSpec((1, 1, GATHER_ROWS), lambda i: (i, 0, 0), memory_space=pltpu.SMEM),
                  pl.BlockSpec(memory_space=pl.ANY)],
        out_specs=pl.BlockSpec((GATHER_ROWS,) + src.shape[1:], lambda i: (i, 0, 0)),
        out_shape=jax.ShapeDtypeStruct((m,) + src.shape[1:], src.dtype),
        scratch_shapes=[pltpu.SemaphoreType.DMA(())],
        compiler_params=pltpu.CompilerParams(dimension_semantics=("arbitrary",)), name="gather_rows",
    )(idx.reshape(m // GATHER_ROWS, 1, GATHER_ROWS), src)


def _swiglu_hidden_block(xs_ref, wg_ref, wu_ref, wd_ref, acc_ref, mm):
    x = xs_ref[...]
    tf = wg_ref.shape[2]
    cw = SWIGLU_COLS if tf % SWIGLU_COLS == 0 else tf
    part = None
    for c0 in range(0, tf, cw):
        a = mm(x, wg_ref[0, :, c0:c0 + cw])
        u = mm(x, wu_ref[0, :, c0:c0 + cw])
        d = mm(_silu(a) * u, wd_ref[0, c0:c0 + cw, :])
        part = d if part is None else part + d
    acc_ref[...] += part


def _swiglu_kernel(te_ref, nt_ref, x_ref, wg_ref, wu_ref, wd_ref, o_ref, acc_ref, xs_ref, *, precise):
    mm = functools.partial(_mm, precise=precise)
    i = pl.program_id(0)
    j = pl.program_id(1)

    @pl.when(i < nt_ref[0])
    def _():
        @pl.when(j == 0)
        def _():
            acc_ref[...] = jnp.zeros(acc_ref.shape, F32)
            xs_ref[...] = _load_rows(x_ref).astype(xs_ref.dtype)

        _swiglu_hidden_block(xs_ref, wg_ref, wu_ref, wd_ref, acc_ref, mm)

        @pl.when(j == pl.num_programs(1) - 1)
        def _():
            _store_rows(o_ref, acc_ref[...])

    @pl.when((i >= nt_ref[0]) & (j == pl.num_programs(1) - 1))
    def _():
        o_ref[...] = jnp.zeros(o_ref.shape, F32)


def _swiglu_gather_kernel(te_ref, nt_ref, idx_ref, idx_next_ref, h_ref, wg_ref, wu_ref, wd_ref, o_ref,
                          acc_ref, xs_ref, xbuf_ref, sem_ref):
    mm = functools.partial(_mm, precise=False)
    i = pl.program_id(0)
    j = pl.program_id(1)
    nt = nt_ref[0]
    tm = acc_ref.shape[0]
    slot = i % 2

    def start_rows(ids_ref, dst_slot):
        def issue(c, carry):
            base = pl.multiple_of(c * GATHER_UNROLL, GATHER_UNROLL)
            for r in range(GATHER_UNROLL):
                src0 = pl.multiple_of(ids_ref[0, 0, base + r], SUBLANE)
                dst0 = pl.multiple_of((base + r) * SUBLANE, SUBLANE)
                pltpu.make_async_copy(h_ref.at[pl.ds(src0, SUBLANE), :], xbuf_ref.at[dst_slot, pl.ds(dst0, SUBLANE), :],
                                      sem_ref.at[dst_slot]).start(priority=r % 2)
            return carry

        lax.fori_loop(0, tm // GATHER_UNROLL, issue, 0)

    @pl.when((i == 0) & (j == 0) & (nt > 0))
    def _():
        start_rows(idx_ref, 0)

    @pl.when(i < nt)
    def _():
        @pl.when(j == 0)
        def _():
            pltpu.make_async_copy(h_ref.at[pl.ds(0, tm * SUBLANE), :], xbuf_ref.at[slot], sem_ref.at[slot]).wait()
            acc_ref[...] = jnp.zeros(acc_ref.shape, F32)
            xs_ref[...] = jnp.concatenate([xbuf_ref[slot, pl.ds(s, tm, stride=SUBLANE), :] for s in range(SUBLANE)],
                                          axis=-1).astype(xs_ref.dtype)

            @pl.when(i + 1 < nt)
            def _():
                start_rows(idx_next_ref, 1 - slot)

        _swiglu_hidden_block(xs_ref, wg_ref, wu_ref, wd_ref, acc_ref, mm)

        @pl.when(j == pl.num_programs(1) - 1)
        def _():
            _store_rows(o_ref, acc_ref[...])

    @pl.when((i >= nt) & (j == pl.num_programs(1) - 1))
    def _():
        o_ref[...] = jnp.zeros(o_ref.shape, F32)


def _swiglu_gathered(h_tiled, src_rows8, tile_expert, n_tiles_used, wg, wu, wd, tm, tf):
    m = src_rows8.shape[0]
    n_tiles = m // tm
    f = wg.shape[2]
    idx = src_rows8.reshape(n_tiles, 1, tm)
    grid_spec = pltpu.PrefetchScalarGridSpec(
        num_scalar_prefetch=2, grid=(n_tiles, f // tf),
        in_specs=[pl.BlockSpec((1, 1, tm), lambda i, j, te, nt: (i, 0, 0), memory_space=pltpu.SMEM),
                  pl.BlockSpec((1, 1, tm), lambda i, j, te, nt: (jnp.minimum(i + 1, n_tiles - 1), 0, 0),
                               memory_space=pltpu.SMEM),
                  pl.BlockSpec(memory_space=pl.ANY),
                  pl.BlockSpec((1, D_MODEL, tf), lambda i, j, te, nt: (te[i], 0, j)),
                  pl.BlockSpec((1, D_MODEL, tf), lambda i, j, te, nt: (te[i], 0, j)),
                  pl.BlockSpec((1, tf, D_MODEL), lambda i, j, te, nt: (te[i], j, 0))],
        out_specs=pl.BlockSpec((tm * SUBLANE, LANE), lambda i, j, te, nt: (i, 0)),
        scratch_shapes=[pltpu.VMEM((tm, D_MODEL), F32), pltpu.VMEM((tm, D_MODEL), BF16),
                        pltpu.VMEM((2, tm * SUBLANE, LANE), F32), pltpu.SemaphoreType.DMA((2,))])
    return pl.pallas_call(
        _swiglu_gather_kernel, grid_spec=grid_spec,
        out_shape=jax.ShapeDtypeStruct((m * SUBLANE, LANE), F32),
        compiler_params=_cparams(("arbitrary", "arbitrary")), name="swiglu_gather",
    )(tile_expert, n_tiles_used, idx, idx, h_tiled, wg, wu, wd)


def _swiglu_grouped(x, tile_expert, n_tiles_used, wg, wu, wd, tm, tf, precise=False):
    m = x.shape[0] // SUBLANE if _is_tiled(x.shape) else x.shape[0]
    f = wg.shape[2]
    grid_spec = pltpu.PrefetchScalarGridSpec(
        num_scalar_prefetch=2, grid=(m // tm, f // tf),
        in_specs=[_block_rows(tm, x, lambda i, j, te, nt: (i, 0)),
                  pl.BlockSpec((1, D_MODEL, tf), lambda i, j, te, nt: (te[i], 0, j)),
                  pl.BlockSpec((1, D_MODEL, tf), lambda i, j, te, nt: (te[i], 0, j)),
                  pl.BlockSpec((1, tf, D_MODEL), lambda i, j, te, nt: (te[i], j, 0))],
        out_specs=_block_rows(tm, x, lambda i, j, te, nt: (i, 0)),
        scratch_shapes=[pltpu.VMEM((tm, D_MODEL), F32), pltpu.VMEM((tm, D_MODEL), F32 if precise else BF16)])
    return pl.pallas_call(
        functools.partial(_swiglu_kernel, precise=precise), grid_spec=grid_spec,
        out_shape=jax.ShapeDtypeStruct(x.shape, F32),
        compiler_params=_cparams(("arbitrary", "arbitrary")), name="swiglu",
    )(tile_expert, n_tiles_used, x, wg, wu, wd)


def _combine_kernel(*refs, moe, final):
    refs = list(refs)
    x_ref = refs.pop(0)
    gate_ref = refs.pop(0)
    y1_ref = refs.pop(0)
    if moe:
        y2_ref = refs.pop(0)
        gates_ref = refs.pop(0)
    fg_ref = refs.pop(0) if final else None
    o_ref = refs.pop(0)
    y = _load_rows(y1_ref)
    if moe:
        gts = gates_ref[...]
        y = gts[:, 0:1] * y + gts[:, 1:2] * _load_rows(y2_ref)
    out = x_ref[...] + gate_ref[0] * y
    if final:
        out = _rms(out) * fg_ref[...]
    o_ref[...] = out


def _combine(x2, gate, y, y_row0, y2_row0, gates, final_g, n_seq, t_len, tm):
    nblk = t_len // tm
    n_tok = n_seq * t_len
    moe = gates is not None
    final = final_g is not None

    def row(i):
        return (i, 0)

    in_specs = [pl.BlockSpec((tm, D_MODEL), row), pl.BlockSpec((1, gate.shape[1], D_MODEL), lambda i: (i // nblk, 0, 0)),
                _block_rows(tm, y, lambda i: (y_row0 // tm + i, 0))]
    args = [x2, gate, y]
    if moe:
        in_specs += [_block_rows(tm, y, lambda i: (y2_row0 // tm + i, 0)), pl.BlockSpec((tm, LANE), row)]
        args += [y, gates]
    if final:
        in_specs.append(pl.BlockSpec((1, D_MODEL), lambda i: (0, 0)))
        args.append(final_g)
    return pl.pallas_call(
        functools.partial(_combine_kernel, moe=moe, final=final), grid=(n_tok // tm,),
        in_specs=in_specs, out_specs=pl.BlockSpec((tm, D_MODEL), row),
        out_shape=jax.ShapeDtypeStruct((n_tok, D_MODEL), F32),
        compiler_params=_cparams(("arbitrary",)), name="combine",
    )(*args)


def _pack_mixer_weights(w_in, w_gate2, b_gate, fox_b_f, qng, kvng, w_uq, w_uk, w_uv, gla_norm_g):
    offs = np.concatenate([[0], np.cumsum(IN_SPLITS)])
    cols = {n: (int(offs[i]), int(offs[i + 1])) for i, n in enumerate(
        ("gq", "gk", "gv", "ag", "ar", "fq", "fk", "fv", "bf", "cq", "ckv", "kr"))}

    def seg(n):
        return w_in[:, cols[n][0]:cols[n][1]]

    half = ROPE_C // 2
    kr = seg("kr")

    def zcols(n):
        return jnp.zeros((D_MODEL, n), F32)

    small = jnp.concatenate([kr, seg("ar"), seg("bf"), zcols(SM_KRS - SM_BF - H_B), -kr[:, half:], kr[:, :half],
                             zcols(LANE - SM_KRS - ROPE_C)], axis=1)
    w = jnp.concatenate([seg("gq"), zcols(C_GK - QK_A), seg("gk"), zcols(C_GV - C_GK - QK_A), seg("gv"), seg("ag"),
                         seg("fq"), seg("fk"), seg("fv"), seg("cq"), seg("ckv"), small], axis=1)
    assert w.shape == (D_MODEL, N_PACK) and small.shape[1] == LANE
    wg2 = jnp.pad(w_gate2, ((SM_AR, LANE - SM_AR - GATE_RANK), (0, 2 * LANE - QK_A)))
    bg = jnp.pad(b_gate, (0, 2 * LANE - QK_A)).reshape(1, 2 * LANE)
    bf = jnp.pad(fox_b_f, (SM_BF, LANE - SM_BF - H_B)).reshape(1, LANE)
    uq = w_uq.reshape(Q_LORA, H_C, NOPE_C + ROPE_C)
    wuqn = uq[:, :, :NOPE_C].reshape(Q_LORA, H_C * NOPE_C)
    x1 = uq[:, :, NOPE_C:NOPE_C + half]
    x2 = uq[:, :, NOPE_C + half:]
    pad = jnp.zeros((Q_LORA, H_C, LANE - ROPE_C), F32)
    wr = jnp.concatenate([x1, x2, pad], axis=2).reshape(Q_LORA, H_C * LANE)
    wrs = jnp.concatenate([-x2, x1, pad], axis=2).reshape(Q_LORA, H_C * LANE)
    eye = jnp.eye(H_C, dtype=F32)
    wuk = (jnp.transpose(w_uk, (1, 2, 0))[:, :, None, :] * eye[:, None, :, None]).reshape(H_C * NOPE_C, H_C * KV_LORA)
    wuv = (jnp.transpose(w_uv, (1, 0, 2))[:, :, None, :] * eye[:, None, :, None]).reshape(H_C, KV_LORA, V_A)
    full = dict(w_in=w, wg2=wg2, bg=bg, bf=bf, qng=qng.reshape(1, Q_LORA), kvng=kvng.reshape(1, KV_LORA), wuqn=wuqn,
                wuk=wuk, wr=wr, wrs=wrs, wuv=wuv, gnorm=jnp.tile(gla_norm_g, H_A).reshape(1, V_A))
    half_prec = dict(full)
    for n in ("w_in", "wg2", "wuqn", "wuk", "wr", "wrs", "wuv"):
        half_prec[n] = full[n].astype(BF16)
    return half_prec, full


def _rope_tables(pos):
    half = ROPE_C // 2
    inv_freq = ROPE_BASE ** (-jnp.arange(half, dtype=F32) / half)
    ang = pos.astype(F32)[:, None] * inv_freq[None, :]
    n = pos.shape[0]
    cos = jnp.concatenate([jnp.cos(ang), jnp.cos(ang), jnp.ones((n, LANE - ROPE_C), F32)], axis=1)
    sin = jnp.concatenate([jnp.sin(ang), jnp.sin(ang), jnp.zeros((n, LANE - ROPE_C), F32)], axis=1)
    return cos, sin


def _state_to_t(s):
    b = s.shape[0]
    eye = jnp.eye(H_A, dtype=F32)
    s_vk = jnp.swapaxes(s, 2, 3)
    return (s_vk[:, :, :, None, :] * eye[None, :, None, :, None]).reshape(b, V_A, QK_A)


def _state_from_t(s_t):
    b = s_t.shape[0]
    blocks = s_t.reshape(b, H_A, DV_A, H_A, DK_A)
    diag = jnp.stack([blocks[:, hh, :, hh, :] for hh in range(H_A)], axis=1)
    return jnp.swapaxes(diag, 2, 3)


def _round_up(a, b):
    return (a + b - 1) // b * b


def kernel(x_prompt, x_sample, c_prompt, c_sample, cache_fox_k, cache_fox_v, cache_fox_logf, cache_mla_ckv, cache_mla_krope, state_gla, ada_w, ada_b, norm_mix_g, norm_ffn_g, w_in, gla_w_gate2, gla_b_gate, gla_norm_g, fox_b_f, mla_q_norm_g, mla_kv_norm_g, mla_w_uq, mla_w_uk, mla_w_uv, w_out, ffn_w_gate, ffn_w_up, ffn_w_down, moe_router, moe_w_gate, moe_w_up, moe_w_down, final_norm_g):
    bp, tp, _ = x_prompt.shape
    bs, ts, _ = x_sample.shape
    past = cache_fox_k.shape[2]
    np_tok, ns_tok = bp * tp, bs * ts
    n_all = np_tok + ns_tok

    tm_p, tm_s = 256, ns_tok
    tq_fox, tk = 512, 512

    nc = _round_up(bp + bs, 8)
    c_all = jnp.zeros((nc, D_MODEL), F32).at[:bp].set(c_prompt).at[bp:bp + bs].set(c_sample)
    mod = _ada(c_all, ada_w, ada_b)

    cos_p, sin_p = _rope_tables(jnp.arange(tp))
    cos_s, sin_s = _rope_tables(past + jnp.tile(jnp.arange(ts), bs))

    def seq_major(a):
        hh, d = a.shape[1], a.shape[3]
        return jnp.transpose(a.reshape(hh, bs, ts, d), (1, 0, 2, 3))

    def tok_major(a):
        hh, d = a.shape[1], a.shape[3]
        return jnp.transpose(a, (1, 0, 2, 3)).reshape(1, hh, bs * ts, d)

    cl = jnp.transpose(cache_fox_logf.astype(F32), (0, 1, 3, 2)).reshape(DEPTH * bs * H_B, past)
    f_cache, f_cache2 = _cumsum(cl, jnp.zeros((cl.shape[0], 1), F32), 512)
    f_cache = f_cache.reshape(DEPTH, bs * H_B, past)
    f_cache2 = f_cache2.reshape(DEPTH, bs, H_B, past // tk, tk)
    cache_k2d = cache_fox_k.astype(F32).reshape(DEPTH, bs, past, H_B * D_B)
    cache_v2d = cache_fox_v.astype(F32).reshape(DEPTH, bs, past, H_B * D_B)

    xp = x_prompt.reshape(np_tok, D_MODEL)
    xs = x_sample.reshape(ns_tok, D_MODEL)
    p_states = [[] for _ in range(6)]
    s_states = [[] for _ in range(6)]

    for l in range(DEPTH):
        mods = [mod[l, :, i * D_MODEL:(i + 1) * D_MODEL] for i in range(6)]
        mp = [m[:bp].reshape(bp, 1, D_MODEL) for m in mods]
        ms = [jnp.repeat(m[bp:bp + bs], ts, axis=0).reshape(1, ns_tok, D_MODEL) for m in mods]
        pw, pw32 = _pack_mixer_weights(w_in[l], gla_w_gate2[l], gla_b_gate[l], fox_b_f[l], mla_q_norm_g[l],
                                       mla_kv_norm_g[l], mla_w_uq[l], mla_w_uk[l], mla_w_uv[l], gla_norm_g[l])
        wo = w_out[l].astype(BF16)
        g_mix = norm_mix_g[l].reshape(1, D_MODEL)
        g_ffn = norm_ffn_g[l].reshape(1, D_MODEL)
        moe = l % 2 == 1
        if moe:
            router_pad = jnp.pad(moe_router[l // 2], ((0, 0), (0, LANE - N_EXPERTS)))
            hbuf_p, hbuf_s = jnp.zeros((n_all * SUBLANE, LANE), F32), None
        else:
            router_pad = None
            hbuf_p, hbuf_s = jnp.zeros((np_tok, D_MODEL), F32), jnp.zeros((ns_tok, D_MODEL), F32)

        (gq, gk, gv, ag, gla, fk, fv, ckv, kc, small, qs, fqh, fkh, fvh) = _inproj(
            xp, mp[0], mp[1], g_mix, pw, cos_p, sin_p, bp, tp, tm_p, False, True)
        logf = small[:, SM_BF:SM_BF + H_B]
        krope = small[:, SM_KR:SM_KR + ROPE_C]
        ya, s_t = _gla(gq, gk, gv, gla, ag, pw["gnorm"], jnp.zeros((bp, V_A, QK_A), F32), bp, tp, 512, CHUNK, bp, False)
        g_fox = bp * H_B
        fqa, fka = _fox_prep(small, fqh, fkh, bp, tp, 512)
        o_fox = _flash(fqa.reshape(g_fox, tp // tq_fox, tq_fox, 2 * D_B), fka.reshape(g_fox, tp, 2 * D_B),
                       fvh.reshape(g_fox, tp, 2 * D_B), None, None,
                       tq=tq_fox, rep=1, tk=tk, q0=0, mode="causal", kv_len=tp, dv=D_B, hp=H_B)
        o_mla = _flash(qs.reshape(bp, tp // tm_p, H_C * tm_p, 2 * LANE), kc.reshape(bp, tp, 2 * LANE), None, None, None,
                       tq=tm_p, rep=H_C, tk=tk, q0=0, mode="chunk", kv_len=tp, dv=KV_LORA)
        res = _mixout(xp, mp[2], ya, o_fox.reshape(bp, H_B, tp, D_B), o_mla.reshape(np_tok // tm_p, H_C, tm_p, KV_LORA),
                      pw["wuv"], wo, mp[3], mp[4], g_ffn, router_pad, hbuf_p, 0, bp, tp, tm_p, False)
        xp, h_p = res[0], res[1]
        for i, st in enumerate((_state_from_t(s_t), fk.reshape(bp, tp, H_B, D_B), fv.reshape(bp, tp, H_B, D_B),
                                logf.reshape(bp, tp, H_B), ckv.reshape(bp, tp, KV_LORA), krope.reshape(bp, tp, ROPE_C))):
            p_states[i].append(st)

        (gq, gk, gv, ag, gla, fk, fv, ckv, kc, small, qs, fq) = _inproj(
            xs, ms[0], ms[1], g_mix, pw32, cos_s, sin_s, 1, ns_tok, tm_s, True, False)
        logf = small[:, SM_BF:SM_BF + H_B]
        krope = small[:, SM_KR:SM_KR + ROPE_C]
        g_fox = bs * H_B
        f_rows = jnp.transpose(logf.reshape(bs, ts, H_B), (0, 2, 1)).reshape(g_fox, ts)
        f_new = _cumsum(f_rows, f_cache[l][:, past - 1:past], ts)[1]
        fq_col = jnp.transpose(f_new.reshape(bs, H_B, ts), (0, 2, 1))
        ya, s_t = _gla(gq, gk, gv, gla, ag, pw["gnorm"], _state_to_t(state_gla[l].astype(F32)), bs, ts, ts, ts, 4, True)
        o_fox = _fox_decode(fq, cache_k2d, cache_v2d, l, fk, fv, fq_col, f_cache2, f_new, bs, ts, tk)
        o_mla = _mla_decode(seq_major(qs).reshape(bs, H_C * ts, 2 * LANE), cache_mla_ckv.astype(F32),
                            cache_mla_krope.astype(F32), l, ckv, kc, bs, ts, tk)
        res_s = _mixout(xs, ms[2], ya, o_fox, tok_major(o_mla.reshape(bs, H_C, ts, KV_LORA)), pw32["wuv"], w_out[l],
                        ms[3], ms[4], g_ffn, router_pad, h_p if moe else hbuf_s, np_tok if moe else 0,
                        1, ns_tok, tm_s, True)
        xs, h_s = res_s[0], res_s[1]
        for i, st in enumerate((_state_from_t(s_t), fk.reshape(bs, ts, H_B, D_B), fv.reshape(bs, ts, H_B, D_B),
                                logf.reshape(bs, ts, H_B), ckv.reshape(bs, ts, KV_LORA), krope.reshape(bs, ts, ROPE_C))):
            s_states[i].append(st)

        last = l == DEPTH - 1
        fg = final_norm_g.reshape(1, D_MODEL) if last else None
        tm_f = 512
        if l % 2 == 0:
            j = l // 2
            n_tiles = np_tok // tm_f
            y_p = _swiglu_grouped(h_p, jnp.zeros((n_tiles,), jnp.int32), jnp.full((1,), n_tiles, jnp.int32),
                                  ffn_w_gate[j:j + 1].astype(BF16), ffn_w_up[j:j + 1].astype(BF16),
                                  ffn_w_down[j:j + 1].astype(BF16), tm_f, ffn_w_gate.shape[2])
            y_s = _swiglu_grouped(h_s, jnp.zeros((1,), jnp.int32), jnp.ones((1,), jnp.int32),
                                  ffn_w_gate[j:j + 1], ffn_w_up[j:j + 1], ffn_w_down[j:j + 1], tm_s, 1408, precise=True)
            xp = _combine(xp, mp[5], y_p, 0, 0, None, fg, bp, tp, tm_p)
            xs = _combine(xs, ms[5], y_s, 0, 0, None, fg, 1, ns_tok, tm_s)
        else:
            j = l // 2
            h_all = h_s
            ids_p, gates_p = res[2:]
            ids_s, gates_s = res_s[2:]
            ids = jnp.concatenate([ids_p[:, :2], ids_s[:, :2]], axis=0)
            e = jnp.transpose(ids).reshape(-1)
            onehot = (e[:, None] == jnp.arange(N_EXPERTS)[None, :]).astype(jnp.int32)
            rank = jnp.sum((jnp.cumsum(onehot, axis=0) - onehot) * onehot, axis=1)
            counts = jnp.sum(onehot, axis=0)
            padded = (counts + tm_f - 1) // tm_f * tm_f
            ends = jnp.cumsum(padded)
            starts = ends - padded
            pos = starts[e] + rank
            m_pad = _round_up(2 * n_all + N_EXPERTS * (tm_f - 1), tm_f)
            token = jnp.tile(jnp.arange(n_all, dtype=jnp.int32), 2)
            src = jnp.zeros((m_pad,), jnp.int32).at[pos].set(token)
            n_tiles = m_pad // tm_f
            tile_row0 = jnp.arange(n_tiles, dtype=jnp.int32) * tm_f
            tile_expert = jnp.minimum(jnp.sum((ends[None, :] <= tile_row0[:, None]).astype(jnp.int32), axis=1),
                                      N_EXPERTS - 1)
            n_used = (ends[-1] // tm_f).astype(jnp.int32).reshape(1)
            y = _swiglu_gathered(h_all, src * SUBLANE, tile_expert, n_used, moe_w_gate[j].astype(BF16),
                                 moe_w_up[j].astype(BF16), moe_w_down[j].astype(BF16), tm_f, 1792)
            n_back = _round_up(n_all, GATHER_ROWS)
            back = jnp.zeros((2 * n_back,), jnp.int32).at[:n_all].set(pos[:n_all]).at[n_back:n_back + n_all].set(pos[n_all:])
            yg = _tiled(_gather_rows(_untiled(y), back))
            xp = _combine(xp, mp[5], yg, 0, n_back, gates_p, fg, bp, tp, tm_p)
            xs = _combine(xs, ms[5], yg, np_tok, n_back + np_tok, gates_s, fg, 1, ns_tok, tm_s)

    outs_p = [jnp.stack(s, axis=0) for s in p_states]
    outs_s = [jnp.stack(s, axis=0) for s in s_states]
    return (xp.reshape(bp, tp, D_MODEL), xs.reshape(bs, ts, D_MODEL), *outs_p, *outs_s)
```

```python
import functools

import numpy as np
import jax
import jax.numpy as jnp
from jax import lax
from jax.experimental import pallas as pl
from jax.experimental.pallas import tpu as pltpu

F32 = jnp.float32
BF16 = jnp.bfloat16
HIGHEST = lax.Precision.HIGHEST

D_MODEL = 1024
DEPTH = 2
CHUNK = 64
EPS = 1e-6
H_A, DK_A, DV_A = 6, 32, 64
GATE_RANK = 16
GATE_TAU = 16.0
H_B, D_B = 4, 64
H_C, NOPE_C, ROPE_C, V_C = 6, 64, 32, 64
Q_LORA, KV_LORA = 256, 128
ROPE_BASE = 10000.0
N_EXPERTS = 8
IN_SPLITS = (H_A * DK_A, H_A * DK_A, H_A * DV_A, H_A * DV_A, GATE_RANK,
             H_B * D_B, H_B * D_B, H_B * D_B, H_B, Q_LORA, KV_LORA, ROPE_C)

QK_A = H_A * DK_A
V_A = H_A * DV_A
QKV_B = H_B * D_B
LANE = 128
NEG = -1e30

C_GQ, C_GK, C_GV, C_AG = 0, 256, 512, 896
C_FQ, C_FK, C_FV = 1280, 1536, 1792
C_CQ, C_CKV, C_SM = 2048, 2304, 2432
N_PACK = 2560
SM_KR, SM_AR, SM_BF, SM_KRS = 0, 32, 48, 64

GLA_SUB = 16
FLASH_ROW_GROUP = 512
SWIGLU_COLS = 256
LOG2E = float(np.log2(np.e))
VMEM_LIMIT = 56 * 1024 * 1024


def _cparams(sem):
    return pltpu.CompilerParams(dimension_semantics=sem, vmem_limit_bytes=VMEM_LIMIT)


def _log_sigmoid(z):
    return jnp.minimum(z, 0.0) - jnp.log1p(jnp.exp(-jnp.abs(z)))


def _silu(z):
    return z * (1.0 / (1.0 + jnp.exp(-z)))


def _rms(x):
    return x * lax.rsqrt(jnp.mean(x * x, axis=-1, keepdims=True) + EPS)


def _dot(a, b):
    return jnp.dot(a, b, preferred_element_type=F32)


def _dot_nt(a, b):
    return lax.dot_general(a, b, (((1,), (1,)), ((), ())), preferred_element_type=F32)


def _dot_tn(a, b):
    return lax.dot_general(a, b, (((0,), (0,)), ((), ())), preferred_element_type=F32)


def _split2(a):
    a = a.astype(F32)
    hi = a.astype(BF16)
    return hi, (a - hi.astype(F32)).astype(BF16)


def _dot_f32(a, b, dims=(((1,), (0,)), ((), ()))):
    a_hi, a_lo = _split2(a)
    b_hi, b_lo = _split2(b)

    def dg(x, y):
        return lax.dot_general(x, y, dims, preferred_element_type=F32)

    return dg(a_hi, b_hi) + (dg(a_hi, b_lo) + dg(a_lo, b_hi))


def _mm(a, b, dims=(((1,), (0,)), ((), ())), *, precise):
    if precise:
        return _dot_f32(a, b, dims)
    return lax.dot_general(a.astype(BF16), b.astype(BF16), dims, preferred_element_type=F32)


_NT = (((1,), (1,)), ((), ()))
_TN = (((0,), (0,)), ((), ()))

SUBLANE = 8
assert D_MODEL == SUBLANE * LANE


def _is_tiled(shape):
    return len(shape) == 2 and shape[1] == LANE


def _tiled(a):
    return a.reshape(a.shape[0] * SUBLANE, LANE)


def _untiled(a):
    return a.reshape(a.shape[0] // SUBLANE, SUBLANE, LANE)


def _block_rows(tm, arr, index_map):
    if _is_tiled(arr.shape):
        return pl.BlockSpec((tm * SUBLANE, LANE), index_map)
    return pl.BlockSpec((tm, D_MODEL), index_map)


def _load_rows(ref):
    if _is_tiled(ref.shape):
        tm = ref.shape[0] // SUBLANE
        return jnp.concatenate([ref[pl.ds(s, tm, stride=SUBLANE), :] for s in range(SUBLANE)], axis=-1)
    return ref[...]


def _store_rows(ref, val):
    if _is_tiled(ref.shape):
        tm = ref.shape[0] // SUBLANE
        for s in range(SUBLANE):
            ref[pl.ds(s, tm, stride=SUBLANE), :] = val[:, s * LANE:(s + 1) * LANE]
    else:
        ref[...] = val


def _ada_kernel(c_ref, w_ref, b_ref, o_ref):
    s = _silu(c_ref[...])
    o_ref[0] = jnp.dot(s, w_ref[0], precision=HIGHEST, preferred_element_type=F32) + b_ref[0]


def _ada(c_all, ada_w, ada_b):
    nc = c_all.shape[0]
    tn = 1536
    return pl.pallas_call(
        _ada_kernel,
        grid=(DEPTH, 6 * D_MODEL // tn),
        in_specs=[pl.BlockSpec((nc, D_MODEL), lambda l, j: (0, 0)),
                  pl.BlockSpec((1, D_MODEL, tn), lambda l, j: (l, 0, j)),
                  pl.BlockSpec((1, 1, tn), lambda l, j: (l, 0, j))],
        out_specs=pl.BlockSpec((1, nc, tn), lambda l, j: (l, 0, j)),
        out_shape=jax.ShapeDtypeStruct((DEPTH, nc, 6 * D_MODEL), F32),
        compiler_params=_cparams(("arbitrary", "arbitrary")),
        name="ada",
    )(c_all, ada_w, ada_b.reshape(DEPTH, 1, 6 * D_MODEL))


def _inproj_kernel(x_ref, sh_ref, sc_ref, g_ref, w_ref, wg2_ref, bg_ref, bf_ref, qng_ref, kvng_ref,
                   wuqn_ref, wuk_ref, wr_ref, wrs_ref, cos_ref, sin_ref,
                   gq_ref, gk_ref, gv_ref, ag_ref, gla_ref, fk_ref, fv_ref,
                   ckv_ref, kc_ref, small_ref, qs_ref, *fox_refs, precise):
    mm = functools.partial(_mm, precise=precise)
    act = kc_ref.dtype
    x = x_ref[...]
    h = (_rms(x) * g_ref[...]) * (1.0 + sc_ref[0]) + sh_ref[0]
    p = mm(h, w_ref[...])

    gq_ref[...] = p[:, C_GQ:C_GQ + QK_A] * (DK_A ** -0.5)
    gk_ref[...] = p[:, C_GK:C_GK + QK_A]
    gv_ref[...] = p[:, C_GV:C_GV + V_A]
    ag_ref[...] = p[:, C_AG:C_AG + V_A]
    sm = p[:, C_SM:C_SM + LANE]
    z = mm(sm, wg2_ref[...]) + bg_ref[...]
    gla_ref[...] = _log_sigmoid(z[:, :QK_A]) * (1.0 / GATE_TAU)

    fq = p[:, C_FQ:C_FQ + QKV_B] * (D_B ** -0.5 * LOG2E)
    fk = p[:, C_FK:C_FK + QKV_B]
    fv = p[:, C_FV:C_FV + QKV_B]
    fk_ref[...] = fk
    fv_ref[...] = fv
    if len(fox_refs) == 1:
        fox_refs[0][...] = fq
    else:
        fqh_ref, fkh_ref, fvh_ref = fox_refs
        for hh in range(H_B):
            sl = slice(hh * D_B, (hh + 1) * D_B)
            fqh_ref[0, hh] = fq[:, sl].astype(act)
            fkh_ref[0, hh] = fk[:, sl].astype(act)
            fvh_ref[0, hh, :, :D_B] = fv[:, sl].astype(act)
            fvh_ref[0, hh, :, D_B:] = jnp.ones((fv.shape[0], D_B), act)

    cos = cos_ref[...]
    sin = sin_ref[...]
    lane = lax.broadcasted_iota(jnp.int32, sm.shape, 1)
    kr = sm * cos + pltpu.roll(sm, LANE - SM_KRS, 1) * sin
    logf = _log_sigmoid(sm + bf_ref[...])
    small_ref[...] = jnp.where((lane >= SM_BF) & (lane < SM_BF + H_B), logf, kr)

    ckv = _rms(p[:, C_CKV:C_CKV + KV_LORA]) * kvng_ref[...]
    ckv_ref[...] = ckv
    kc_ref[:, :KV_LORA] = ckv.astype(act)
    kc_ref[:, KV_LORA:] = jnp.where(lane < ROPE_C, kr, 0.0).astype(act)
    cqn = _rms(p[:, C_CQ:C_CQ + Q_LORA]) * qng_ref[...]
    nope = mm(cqn, wuqn_ref[...])
    qlat = mm(nope, wuk_ref[...])
    qa = mm(cqn, wr_ref[...])
    qb = mm(cqn, wrs_ref[...])
    scale = (NOPE_C + ROPE_C) ** -0.5 * LOG2E
    for hh in range(H_C):
        sl = slice(hh * LANE, (hh + 1) * LANE)
        qs_ref[0, hh, :, :KV_LORA] = (qlat[:, sl] * scale).astype(act)
        qs_ref[0, hh, :, KV_LORA:] = ((qa[:, sl] * cos + qb[:, sl] * sin) * scale).astype(act)


def _inproj(x2, shift, scale, g, pw, cos_tab, sin_tab, n_seq, t_len, tm, precise, fox_head_major):
    n_tok = n_seq * t_len
    nblk = t_len // tm
    act = F32 if precise else BF16
    mod_rows = shift.shape[1]

    def row(i):
        return (i, 0)

    def seq(i):
        return (i // nblk, 0, 0)

    def const2(i):
        return (0, 0)

    def tab(i):
        return (i % nblk, 0)

    def headmajor(i):
        return (i // nblk, 0, i % nblk, 0)

    sds = jax.ShapeDtypeStruct
    out_shape = (
        sds((n_tok, QK_A), F32), sds((n_tok, QK_A), F32), sds((n_tok, V_A), F32), sds((n_tok, V_A), F32),
        sds((n_tok, QK_A), F32),
        sds((n_tok, QKV_B), F32), sds((n_tok, QKV_B), F32),
        sds((n_tok, KV_LORA), F32), sds((n_tok, 2 * LANE), act), sds((n_tok, LANE), F32),
        sds((n_tok // tm, H_C, tm, 2 * LANE), act),
    )
    out_specs = (
        pl.BlockSpec((tm, QK_A), row), pl.BlockSpec((tm, QK_A), row), pl.BlockSpec((tm, V_A), row),
        pl.BlockSpec((tm, V_A), row), pl.BlockSpec((tm, QK_A), row),
        pl.BlockSpec((tm, QKV_B), row), pl.BlockSpec((tm, QKV_B), row),
        pl.BlockSpec((tm, KV_LORA), row), pl.BlockSpec((tm, 2 * LANE), row), pl.BlockSpec((tm, LANE), row),
        pl.BlockSpec((1, H_C, tm, 2 * LANE), lambda i: (i, 0, 0, 0)),
    )
    if fox_head_major:
        out_shape += (sds((n_seq, H_B, t_len, D_B), act), sds((n_seq, H_B, t_len, D_B), act),
                      sds((n_seq, H_B, t_len, 2 * D_B), act))
        out_specs += (pl.BlockSpec((1, H_B, tm, D_B), headmajor), pl.BlockSpec((1, H_B, tm, D_B), headmajor),
                      pl.BlockSpec((1, H_B, tm, 2 * D_B), headmajor))
    else:
        out_shape += (sds((n_tok, QKV_B), F32),)
        out_specs += (pl.BlockSpec((tm, QKV_B), row),)
    in_specs = [
        pl.BlockSpec((tm, D_MODEL), row), pl.BlockSpec((1, mod_rows, D_MODEL), seq),
        pl.BlockSpec((1, mod_rows, D_MODEL), seq), pl.BlockSpec((1, D_MODEL), const2),
        pl.BlockSpec((D_MODEL, N_PACK), const2), pl.BlockSpec((LANE, 2 * LANE), const2),
        pl.BlockSpec((1, 2 * LANE), const2), pl.BlockSpec((1, LANE), const2),
        pl.BlockSpec((1, Q_LORA), const2), pl.BlockSpec((1, KV_LORA), const2),
        pl.BlockSpec((Q_LORA, H_C * NOPE_C), const2), pl.BlockSpec((H_C * NOPE_C, H_C * KV_LORA), const2),
        pl.BlockSpec((Q_LORA, H_C * LANE), const2), pl.BlockSpec((Q_LORA, H_C * LANE), const2),
        pl.BlockSpec((tm, LANE), tab), pl.BlockSpec((tm, LANE), tab),
    ]
    return pl.pallas_call(
        functools.partial(_inproj_kernel, precise=precise), grid=(n_tok // tm,), in_specs=in_specs,
        out_specs=out_specs, out_shape=out_shape, compiler_params=_cparams(("arbitrary",)), name="inproj",
    )(x2, shift, scale, g, pw["w_in"], pw["wg2"], pw["bg"], pw["bf"], pw["qng"], pw["kvng"],
      pw["wuqn"], pw["wuk"], pw["wr"], pw["wrs"], cos_tab, sin_tab)


def _cumsum_kernel(x_ref, init_ref, o_ref, o2_ref, *, tb):
    n = x_ref.shape[1]
    upper = (lax.broadcasted_iota(jnp.int32, (tb, tb), 0) <= lax.broadcasted_iota(jnp.int32, (tb, tb), 1)).astype(F32)
    carry = init_ref[...]
    for j in range(n // tb):
        blk = jnp.dot(x_ref[:, j * tb:(j + 1) * tb], upper, precision=HIGHEST, preferred_element_type=F32) + carry
        o_ref[:, j * tb:(j + 1) * tb] = blk
        o2_ref[:, j * tb:(j + 1) * tb] = blk * LOG2E
        carry = blk[:, tb - 1:tb]


def _cumsum(x, init, tb):
    return pl.pallas_call(
        functools.partial(_cumsum_kernel, tb=tb),
        out_shape=(jax.ShapeDtypeStruct(x.shape, F32), jax.ShapeDtypeStruct(x.shape, F32)), name="cumsum",
    )(x, init)


def _fox_prep_kernel(sm_ref, q_ref, k_ref, qa_ref, ka_ref, carry_ref, *, tb):
    @pl.when(pl.program_id(1) == 0)
    def _():
        carry_ref[...] = jnp.zeros(carry_ref.shape, F32)

    tril = (lax.broadcasted_iota(jnp.int32, (tb, tb), 0) >= lax.broadcasted_iota(jnp.int32, (tb, tb), 1)).astype(BF16)
    x = sm_ref[...]
    x_hi = x.astype(BF16)
    x_mid = (x - x_hi.astype(F32)).astype(BF16)
    x_lo = (x - x_hi.astype(F32) - x_mid.astype(F32)).astype(BF16)
    cum = _dot(tril, x_hi) + _dot(tril, x_mid) + _dot(tril, x_lo) + carry_ref[...]
    carry_ref[...] = cum[tb - 1:tb]
    lane = lax.broadcasted_iota(jnp.int32, (tb, D_B), 1)
    for hh in range(H_B):
        f = jnp.broadcast_to(cum[:, SM_BF + hh:SM_BF + hh + 1] * LOG2E, (tb, D_B))
        hi = f.astype(BF16).astype(F32)
        mid = (f - hi).astype(BF16).astype(F32)
        lo = (f - hi - mid).astype(BF16).astype(F32)
        terms = jnp.where(lane % 3 == 0, hi, jnp.where(lane % 3 == 1, mid, lo))
        q_extra = jnp.where(lane < 3, terms, jnp.where(lane < 6, 1.0, 0.0))
        k_extra = jnp.where(lane < 3, 1.0, jnp.where(lane < 6, -terms, 0.0))
        qa_ref[0, hh, :, :D_B] = q_ref[0, hh]
        qa_ref[0, hh, :, D_B:] = q_extra.astype(BF16)
        ka_ref[0, hh, :, :D_B] = k_ref[0, hh]
        ka_ref[0, hh, :, D_B:] = k_extra.astype(BF16)


def _fox_prep(small, fqh, fkh, n_seq, t_len, tb):
    nblk = t_len // tb

    def hm(b, j):
        return (b, 0, j, 0)

    return pl.pallas_call(
        functools.partial(_fox_prep_kernel, tb=tb), grid=(n_seq, nblk),
        in_specs=[pl.BlockSpec((tb, LANE), lambda b, j: (b * nblk + j, 0)),
                  pl.BlockSpec((1, H_B, tb, D_B), hm), pl.BlockSpec((1, H_B, tb, D_B), hm)],
        out_specs=(pl.BlockSpec((1, H_B, tb, 2 * D_B), hm), pl.BlockSpec((1, H_B, tb, 2 * D_B), hm)),
        out_shape=(jax.ShapeDtypeStruct((n_seq, H_B, t_len, 2 * D_B), BF16),) * 2,
        scratch_shapes=[pltpu.VMEM((1, LANE), F32)],
        compiler_params=_cparams(("arbitrary", "arbitrary")), name="fox_prep",
    )(small, fqh, fkh)


def _gla_kernel(q_ref, k_ref, v_ref, la_ref, ag_ref, gn_ref, s0_ref, y_ref, sout_ref, s_ref, *, chunk, n_chunks,
                precise):
    mm = functools.partial(_mm, precise=precise)
    c = chunk
    nsub = c // GLA_SUB

    nb = q_ref.shape[0]

    @pl.when(pl.program_id(1) == 0)
    def _():
        s_ref[...] = s0_ref[...]

    lane_qk = lax.broadcasted_iota(jnp.int32, (GLA_SUB, QK_A), 1) // DK_A
    lane_v = lax.broadcasted_iota(jnp.int32, (GLA_SUB, V_A), 1) // DV_A
    bd = (lax.broadcasted_iota(jnp.int32, (V_A, QK_A), 0) // DV_A) == (lax.broadcasted_iota(jnp.int32, (V_A, QK_A), 1) // DK_A)
    tril = (lax.broadcasted_iota(jnp.int32, (c, c), 0) >= lax.broadcasted_iota(jnp.int32, (c, c), 1)).astype(F32)
    hm = (lax.broadcasted_iota(jnp.int32, (V_A, V_A), 0) // DV_A) == (lax.broadcasted_iota(jnp.int32, (V_A, V_A), 1) // DV_A)
    head_mean = jnp.where(hm, 1.0 / DV_A, 0.0).astype(F32)

    def cumsum_rows(la):
        if precise:
            return jnp.dot(tril, la, precision=HIGHEST, preferred_element_type=F32)
        hi = la.astype(BF16)
        lo = (la - hi.astype(F32)).astype(BF16)
        tb16 = tril.astype(BF16)
        return _dot(tb16, hi) + _dot(tb16, lo)

    def chunk_step(bb, r):
        q = q_ref[bb, pl.ds(r, c), :]
        k = k_ref[bb, pl.ds(r, c), :]
        v = v_ref[bb, pl.ds(r, c), :]
        la = la_ref[bb, pl.ds(r, c), :]
        b = cumsum_rows(la)
        s_t = s_ref[bb]
        vb = v if precise else v.astype(BF16)
        o_inter = mm(q * jnp.exp(b), s_t, _NT)
        outs = []
        for i in range(nsub):
            r0 = i * GLA_SUB
            r1 = r0 + GLA_SUB
            bi = b[r0 - 1:r0] if i > 0 else jnp.zeros((1, QK_A), F32)
            qi = q[r0:r1] * jnp.exp(b[r0:r1] - bi)
            kk = k[:r1] * jnp.exp(bi - b[:r1])
            qst = jnp.concatenate([jnp.where(lane_qk == hh, qi, 0.0) for hh in range(H_A)], axis=0)
            att = mm(qst, kk, _NT)
            t_idx = r0 + lax.broadcasted_iota(jnp.int32, att.shape, 0) % GLA_SUB
            s_idx = lax.broadcasted_iota(jnp.int32, att.shape, 1)
            att = jnp.where(s_idx <= t_idx, att, 0.0)
            oi = mm(att, vb[:r1])
            o = jnp.zeros((GLA_SUB, V_A), F32)
            for hh in range(H_A):
                o = o + jnp.where(lane_v == hh, oi[hh * GLA_SUB:(hh + 1) * GLA_SUB], 0.0)
            outs.append(o)
        o = jnp.concatenate(outs, axis=0) + o_inter if nsub > 1 else outs[0] + o_inter
        b_last = b[c - 1:c]
        kd = k * jnp.exp(b_last - b)
        s_ref[bb] = s_t * jnp.exp(b_last) + jnp.where(bd, mm(vb, kd, _TN), 0.0)
        ms = mm(o * o, head_mean)
        y = o * lax.rsqrt(ms + EPS) * gn_ref[...] * _silu(ag_ref[bb, pl.ds(r, c), :])
        y_ref[bb, pl.ds(r, c), :] = y.astype(y_ref.dtype)

    def body(ci, carry):
        r = pl.multiple_of(ci * c, c)
        for bb in range(nb):
            chunk_step(bb, r)
        return carry

    lax.fori_loop(0, n_chunks, body, 0, unroll=2 if n_chunks % 2 == 0 else 1)

    @pl.when(pl.program_id(1) == pl.num_programs(1) - 1)
    def _():
        sout_ref[...] = s_ref[...]


def _gla(gq, gk, gv, gla, ag, gnorm, s0_t, n_seq, t_len, tb, chunk, nb, precise):
    nblk = t_len // tb

    def row(b, j):
        return (b, j, 0)

    def st(b, j):
        return (b, 0, 0)

    def seq3(a):
        return a.reshape(n_seq, t_len, a.shape[-1])

    ya, s_t = pl.pallas_call(
        functools.partial(_gla_kernel, chunk=chunk, n_chunks=tb // chunk, precise=precise),
        grid=(n_seq // nb, nblk),
        in_specs=[pl.BlockSpec((nb, tb, QK_A), row), pl.BlockSpec((nb, tb, QK_A), row), pl.BlockSpec((nb, tb, V_A), row),
                  pl.BlockSpec((nb, tb, QK_A), row), pl.BlockSpec((nb, tb, V_A), row),
                  pl.BlockSpec((1, V_A), lambda b, j: (0, 0)), pl.BlockSpec((nb, V_A, QK_A), st)],
        out_specs=(pl.BlockSpec((nb, tb, V_A), row), pl.BlockSpec((nb, V_A, QK_A), st)),
        out_shape=(jax.ShapeDtypeStruct((n_seq, t_len, V_A), F32 if precise else BF16),
                   jax.ShapeDtypeStruct((n_seq, V_A, QK_A), F32)),
        scratch_shapes=[pltpu.VMEM((nb, V_A, QK_A), F32)],
        compiler_params=_cparams(("arbitrary", "arbitrary")), name="gla",
    )(seq3(gq), seq3(gk), seq3(gv), seq3(gla), seq3(ag), gnorm, s0_t)
    return ya.reshape(n_seq * t_len, V_A), s_t


def _flash_kernel(*refs, tq, rep, hp, tk, q0, mode, kv_len, dv, bias, v_from_k, precise):
    mm = functools.partial(_mm, precise=precise)
    refs = list(refs)
    q_ref = refs.pop(0)
    k_ref = refs.pop(0)
    v_ref = k_ref if v_from_k else refs.pop(0)
    fq_ref = refs.pop(0) if bias else None
    fk_ref = refs.pop(0) if bias else None
    o_ref, m_ref, acc_ref = refs
    i = pl.program_id(1)
    rows = rep * tq
    m_ref[...] = jnp.full(m_ref.shape, NEG, F32)
    acc_ref[...] = jnp.zeros(acc_ref.shape, F32)

    first_q = q0 + i * tq
    last_q = first_q + tq - 1
    if mode == "chunk":
        vis_all = (first_q // CHUNK) * CHUNK + CHUNK - 1
        vis_any = (last_q // CHUNK) * CHUNK + CHUNK - 1
    else:
        vis_all = first_q
        vis_any = last_q
    vis_all = jnp.minimum(vis_all, kv_len - 1)
    vis_any = jnp.minimum(vis_any, kv_len - 1)
    n_full = (vis_all + 1) // tk
    n_any = vis_any // tk + 1

    grp = min(rows, FLASH_ROW_GROUP)

    def step(jb, masked):
        for hh in range(hp):
            head_step(hh, jb, masked)

    def head_step(hh, jb, masked):
        k_start = pl.multiple_of(jb * tk, tk)
        k = k_ref[hh, pl.ds(k_start, tk), :]
        if v_from_k:
            v = jnp.where(lax.broadcasted_iota(jnp.int32, k.shape, 1) < dv, k, jnp.ones_like(k))
        else:
            v = v_ref[hh, pl.ds(k_start, tk), :]
        for g in range(rows // grp):
            rs = slice(g * grp, (g + 1) * grp)
            s = mm(q_ref[hh, 0, rs, :], k, _NT)
            if bias:
                s = s + fq_ref[hh, 0, rs, :] - fk_ref[hh, jb]
            rs = slice(hh * rows + g * grp, hh * rows + (g + 1) * grp)
            if masked:
                qpos = first_q + (g * grp + lax.broadcasted_iota(jnp.int32, (grp, tk), 0)) % tq
                kpos = k_start + lax.broadcasted_iota(jnp.int32, (grp, tk), 1)
                if mode == "chunk":
                    ok = (kpos // CHUNK) <= (qpos // CHUNK)
                else:
                    ok = kpos <= qpos
                ok = ok & (kpos < kv_len)
                s = jnp.where(ok, s, NEG)
            chunks = [s[:, c * LANE:(c + 1) * LANE] for c in range(tk // LANE)]
            smax = chunks[0]
            for ch in chunks[1:]:
                smax = jnp.maximum(smax, ch)
            m_prev = m_ref[rs, :]
            m_new = jnp.maximum(m_prev, jnp.max(smax, axis=-1, keepdims=True))
            alpha = jnp.exp2(m_prev - m_new)
            p = jnp.concatenate([jnp.exp2(ch - m_new) for ch in chunks], axis=1)
            acc = acc_ref[rs, :]
            alpha_w = alpha if acc.shape[1] == LANE else jnp.concatenate([alpha] * (acc.shape[1] // LANE), axis=1)
            acc_ref[rs, :] = alpha_w * acc + mm(p, v)
            m_ref[rs, :] = m_new

    def loop(lo, hi, masked):
        def body(jb, carry):
            step(jb, masked)
            return carry
        lax.fori_loop(lo, hi, body, 0)

    loop(0, n_full, False)
    loop(n_full, n_any, True)

    acc = acc_ref[...]
    if dv == LANE:
        out = acc[:, :dv] / acc[:, dv:]
    else:
        out = (acc / pltpu.roll(acc, LANE - dv, 1))[:, :dv]
    for hh in range(hp):
        o_ref[hh, 0] = out[hh * rows:(hh + 1) * rows].astype(o_ref.dtype)


def _flash(q, k, v, fq, fk, *, tq, rep, tk, q0, mode, kv_len, dv, hp=1, precise=False):
    g, nq, rows, dqk = q.shape
    t_k = k.shape[1]
    nk = t_k // tk
    bias = fq is not None
    v_from_k = v is None

    def qmap(b, i):
        return (b, i, 0, 0)

    def kmap(b, i):
        return (b, 0, 0)

    in_specs = [pl.BlockSpec((hp, 1, rows, dqk), qmap), pl.BlockSpec((hp, t_k, dqk), kmap)]
    args = [q, k]
    dva = dqk if v_from_k else v.shape[2]
    assert dva % LANE == 0 and dva > dv
    if not v_from_k:
        in_specs.append(pl.BlockSpec((hp, t_k, dva), kmap))
        args.append(v)
    if bias:
        in_specs += [pl.BlockSpec((hp, 1, rows, 1), qmap), pl.BlockSpec((hp, nk, 1, tk), lambda b, i: (b, 0, 0, 0))]
        args += [fq, fk.reshape(g, nk, 1, tk)]
    return pl.pallas_call(
        functools.partial(_flash_kernel, tq=tq, rep=rep, hp=hp, tk=tk, q0=q0, mode=mode, kv_len=kv_len, dv=dv,
                          bias=bias, v_from_k=v_from_k, precise=precise),
        grid=(g // hp, nq), in_specs=in_specs,
        out_specs=pl.BlockSpec((hp, 1, rows, dv), qmap),
        out_shape=jax.ShapeDtypeStruct((g, nq, rows, dv), F32 if precise else BF16),
        scratch_shapes=[pltpu.VMEM((hp * rows, LANE), F32), pltpu.VMEM((hp * rows, dva), F32)],
        compiler_params=_cparams(("arbitrary", "arbitrary")), name="flash_" + mode,
    )(*args)


_hdot = _dot_f32


def _online_softmax_step(s, v, m_ref, l_ref, acc_ref):
    m_prev = m_ref[...]
    m_new = jnp.maximum(m_prev, jnp.max(s, axis=-1, keepdims=True))
    alpha = jnp.exp2(m_prev - m_new)
    p = jnp.exp2(s - m_new)
    l_ref[...] = alpha * l_ref[...] + jnp.sum(p, axis=-1, keepdims=True)
    acc_ref[...] = alpha * acc_ref[...] + _hdot(p, v)
    m_ref[...] = m_new


def _fox_decode_kernel(q_ref, kc_ref, vc_ref, kn_ref, vn_ref, fq_ref, fkc_ref, fkn_ref, o_ref, m_ref, l_ref, acc_ref,
                       *, tk, past, ts):
    rows = H_B * ts
    q = q_ref[...]
    lane_head = lax.broadcasted_iota(jnp.int32, q.shape, 1) // D_B
    q_st = jnp.concatenate([jnp.where(lane_head == hh, q, 0.0) for hh in range(H_B)], axis=0)
    fq = fq_ref[0]
    fq_st = jnp.concatenate([fq[:, hh:hh + 1] for hh in range(H_B)], axis=0)
    m_ref[...] = jnp.full(m_ref.shape, NEG, F32)
    l_ref[...] = jnp.zeros(l_ref.shape, F32)
    acc_ref[...] = jnp.zeros(acc_ref.shape, F32)

    def cached(jb, carry):
        k0 = pl.multiple_of(jb * tk, tk)
        fk_st = jnp.concatenate([jnp.broadcast_to(fkc_ref[0, 0, hh, pl.ds(jb, 1), :], (ts, tk)) for hh in range(H_B)],
                                axis=0)
        s = _hdot(q_st, kc_ref[0, 0, pl.ds(k0, tk), :], _NT) + fq_st - fk_st
        _online_softmax_step(s, vc_ref[0, 0, pl.ds(k0, tk), :], m_ref, l_ref, acc_ref)
        return carry

    lax.fori_loop(0, past // tk, cached, 0)
    fkn = fkn_ref[0]
    fk_st = jnp.concatenate([jnp.broadcast_to(fkn[hh:hh + 1, :], (ts, ts)) for hh in range(H_B)], axis=0)
    s = _hdot(q_st, kn_ref[...], _NT) + fq_st - fk_st
    t_idx = lax.broadcasted_iota(jnp.int32, s.shape, 0) % ts
    s_idx = lax.broadcasted_iota(jnp.int32, s.shape, 1)
    _online_softmax_step(jnp.where(s_idx <= t_idx, s, NEG), vn_ref[...], m_ref, l_ref, acc_ref)
    res = acc_ref[...] / l_ref[...]
    out = jnp.zeros((ts, H_B * D_B), F32)
    for hh in range(H_B):
        out = out + jnp.where(lane_head == hh, res[hh * ts:(hh + 1) * ts], 0.0)
    o_ref[...] = out


def _fox_decode(q, cache_k, cache_v, layer, k_new, v_new, fq_col, f_cache, f_new, n_seq, ts, tk):
    past = cache_k.shape[2]
    width = H_B * D_B
    rows = H_B * ts

    def tok(b):
        return (b, 0)

    def cache(b):
        return (layer, b, 0, 0)

    return pl.pallas_call(
        functools.partial(_fox_decode_kernel, tk=tk, past=past, ts=ts), grid=(n_seq,),
        in_specs=[pl.BlockSpec((ts, width), tok), pl.BlockSpec((1, 1, past, width), cache),
                  pl.BlockSpec((1, 1, past, width), cache), pl.BlockSpec((ts, width), tok), pl.BlockSpec((ts, width), tok),
                  pl.BlockSpec((1, ts, H_B), lambda b: (b, 0, 0)),
                  pl.BlockSpec((1, 1, H_B, past // tk, tk), lambda b: (layer, b, 0, 0, 0)),
                  pl.BlockSpec((1, H_B, ts), lambda b: (b, 0, 0))],
        out_specs=pl.BlockSpec((ts, width), tok),
        out_shape=jax.ShapeDtypeStruct((n_seq * ts, width), F32),
        scratch_shapes=[pltpu.VMEM((rows, 1), F32), pltpu.VMEM((rows, 1), F32), pltpu.VMEM((rows, width), F32)],
        compiler_params=_cparams(("arbitrary",)), name="fox_decode",
    )(q, cache_k, cache_v, k_new, v_new, fq_col, f_cache, f_new.reshape(n_seq, H_B, ts))


def _mla_decode_kernel(q_ref, cc_ref, cr_ref, cn_ref, kn_ref, o_ref, m_ref, l_ref, acc_ref, *, tk, past, ts):
    q = q_ref[0]
    q_lat = q[:, :KV_LORA]
    q_rope = q[:, KV_LORA:KV_LORA + ROPE_C]
    m_ref[...] = jnp.full(m_ref.shape, NEG, F32)
    l_ref[...] = jnp.zeros(l_ref.shape, F32)
    acc_ref[...] = jnp.zeros(acc_ref.shape, F32)

    def cached(jb, carry):
        k0 = pl.multiple_of(jb * tk, tk)
        ck = cc_ref[0, 0, pl.ds(k0, tk), :]
        s = _hdot(q_lat, ck, _NT) + _hdot(q_rope, cr_ref[0, 0, pl.ds(k0, tk), :], _NT)
        _online_softmax_step(s, ck, m_ref, l_ref, acc_ref)
        return carry

    lax.fori_loop(0, past // tk, cached, 0)
    cn = cn_ref[...]
    s = _hdot(q_lat, cn, _NT) + _hdot(q_rope, kn_ref[:, KV_LORA:KV_LORA + ROPE_C], _NT)
    q_chunk = (past + lax.broadcasted_iota(jnp.int32, s.shape, 0) % ts) // CHUNK
    k_chunk = (past + lax.broadcasted_iota(jnp.int32, s.shape, 1)) // CHUNK
    _online_softmax_step(jnp.where(k_chunk <= q_chunk, s, NEG), cn, m_ref, l_ref, acc_ref)
    o_ref[0] = acc_ref[...] / l_ref[...]


def _mla_decode(q, cache_ckv, cache_krope, layer, ckv_new, kc_new, n_seq, ts, tk):
    past = cache_ckv.shape[2]
    rows = H_C * ts

    def tok(b):
        return (b, 0)

    def cache(b):
        return (layer, b, 0, 0)

    return pl.pallas_call(
        functools.partial(_mla_decode_kernel, tk=tk, past=past, ts=ts), grid=(n_seq,),
        in_specs=[pl.BlockSpec((1, rows, 2 * LANE), lambda b: (b, 0, 0)),
                  pl.BlockSpec((1, 1, past, KV_LORA), cache), pl.BlockSpec((1, 1, past, ROPE_C), cache),
                  pl.BlockSpec((ts, KV_LORA), tok), pl.BlockSpec((ts, 2 * LANE), tok)],
        out_specs=pl.BlockSpec((1, rows, KV_LORA), lambda b: (b, 0, 0)),
        out_shape=jax.ShapeDtypeStruct((n_seq, rows, KV_LORA), F32),
        scratch_shapes=[pltpu.VMEM((rows, 1), F32), pltpu.VMEM((rows, 1), F32), pltpu.VMEM((rows, KV_LORA), F32)],
        compiler_params=_cparams(("arbitrary",)), name="mla_decode",
    )(q, cache_ckv, cache_krope, ckv_new, kc_new)


def _mixout_kernel(*refs, precise, route):
    x_ref, gate_ref, ya_ref, of_ref, ol_ref, wuv_ref, wo_ref, sh_ref, sc_ref, g_ref = refs[:10]
    rest = list(refs[10:])
    wr_ref = rest.pop(0) if route else None
    rest.pop(0)
    o_ref, h_ref = rest[:2]
    ids_ref, gates_ref = rest[2:] if route else (None, None)
    mm = functools.partial(_mm, precise=precise)
    acc = mm(ya_ref[...], wo_ref[:V_A, :])
    if len(of_ref.shape) == 2:
        acc = acc + mm(of_ref[...], wo_ref[V_A:V_A + QKV_B, :])
    else:
        for hh in range(H_B):
            r0 = V_A + hh * D_B
            acc = acc + mm(of_ref[0, hh], wo_ref[r0:r0 + D_B, :])
    yc = mm(ol_ref[0, 0], wuv_ref[0])
    for hh in range(1, H_C):
        yc = yc + mm(ol_ref[0, hh], wuv_ref[hh])
    acc = acc + mm(yc, wo_ref[V_A + QKV_B:, :])
    x_new = x_ref[...] + gate_ref[0] * acc
    o_ref[...] = x_new
    _ffn_input(x_new, sh_ref, sc_ref, g_ref, wr_ref, h_ref, ids_ref, gates_ref)


def _mixout(x2, gate, ya, ofox, olat, wuv, wo, shift, scale, g_ffn, router_pad, hbuf, row0, n_seq, t_len, tm, precise):
    nblk = t_len // tm
    n_tok = n_seq * t_len
    mod_rows = gate.shape[1]
    route = router_pad is not None
    off = row0 // tm

    def row(i):
        return (i, 0)

    def seq(i):
        return (i // nblk, 0, 0)

    def const2(i):
        return (0, 0)

    mod_spec = pl.BlockSpec((1, mod_rows, D_MODEL), seq)
    in_specs = [pl.BlockSpec((tm, D_MODEL), row), mod_spec, pl.BlockSpec((tm, V_A), row),
                pl.BlockSpec((tm, QKV_B), row) if ofox.ndim == 2 else
                pl.BlockSpec((1, H_B, tm, D_B), lambda i: (i // nblk, 0, i % nblk, 0)),
                pl.BlockSpec((1, H_C, tm, KV_LORA), lambda i: (i, 0, 0, 0)),
                pl.BlockSpec((H_C, KV_LORA, V_A), lambda i: (0, 0, 0)),
                pl.BlockSpec((D_MODEL, D_MODEL), const2),
                mod_spec, mod_spec, pl.BlockSpec((1, D_MODEL), const2)]
    args = [x2, gate, ya, ofox, olat, wuv, wo, shift, scale, g_ffn]
    out_shape = [jax.ShapeDtypeStruct((n_tok, D_MODEL), F32), jax.ShapeDtypeStruct(hbuf.shape, F32)]
    out_specs = [pl.BlockSpec((tm, D_MODEL), row), _block_rows(tm, hbuf, lambda i: (off + i, 0))]
    if route:
        in_specs.append(pl.BlockSpec((D_MODEL, LANE), const2))
        args.append(router_pad)
        out_shape += [jax.ShapeDtypeStruct((n_tok, LANE), jnp.int32), jax.ShapeDtypeStruct((n_tok, LANE), F32)]
        out_specs += [pl.BlockSpec((tm, LANE), row), pl.BlockSpec((tm, LANE), row)]
    in_specs.append(pl.BlockSpec(memory_space=pl.ANY))
    args.append(hbuf)
    return pl.pallas_call(
        functools.partial(_mixout_kernel, precise=precise, route=route), grid=(n_tok // tm,),
        in_specs=in_specs, out_specs=tuple(out_specs), out_shape=tuple(out_shape),
        input_output_aliases={len(args) - 1: 1},
        compiler_params=_cparams(("arbitrary",)), name="mixout",
    )(*args)


def _ffn_input(x, sh_ref, sc_ref, g_ref, wr_ref, h_ref, ids_ref, gates_ref):
    h = (_rms(x) * g_ref[...]) * (1.0 + sc_ref[0]) + sh_ref[0]
    _store_rows(h_ref, h)
    if wr_ref is not None:
        logits = _dot_f32(h, wr_ref[...])
        lane = lax.broadcasted_iota(jnp.int32, logits.shape, 1)
        logits = jnp.where(lane < N_EXPERTS, logits, NEG)
        m1 = jnp.max(logits, axis=-1, keepdims=True)
        i1 = jnp.min(jnp.where(logits == m1, lane, LANE), axis=-1, keepdims=True)
        rest = jnp.where(lane == i1, NEG, logits)
        m2 = jnp.max(rest, axis=-1, keepdims=True)
        i2 = jnp.min(jnp.where(rest == m2, lane, LANE), axis=-1, keepdims=True)
        e2 = jnp.exp(m2 - m1)
        g1 = 1.0 / (1.0 + e2)
        g2 = e2 / (1.0 + e2)
        ids_ref[...] = jnp.where(lane == 0, i1, i2)
        gates_ref[...] = jnp.where(lane == 0, g1, g2)


GATHER_ROWS = 512
GATHER_UNROLL = 32


def _gather_kernel(idx_ref, src_ref, out_ref, sem):
    def issue(c, carry):
        base = pl.multiple_of(c * GATHER_UNROLL, GATHER_UNROLL)
        for r in range(GATHER_UNROLL):
            pltpu.make_async_copy(src_ref.at[idx_ref[0, 0, base + r]], out_ref.at[base + r], sem).start(priority=r % 2)
        return carry

    lax.fori_loop(0, GATHER_ROWS // GATHER_UNROLL, issue, 0)
    pltpu.make_async_copy(src_ref.at[pl.ds(0, GATHER_ROWS)], out_ref, sem).wait()


def _gather_rows(src, idx):
    m = idx.shape[0]
    return pl.pallas_call(
        _gather_kernel, grid=(m // GATHER_ROWS,),
        in_specs=[pl.BlockSpec((1, 1, GATHER_ROWS), lambda i: (i, 0, 0), memory_space=pltpu.SMEM),
                  pl.BlockSpec(memory_space=pl.ANY)],
        out_specs=pl.BlockSpec((GATHER_ROWS,) + src.shape[1:], lambda i: (i, 0, 0)),
        out_shape=jax.ShapeDtypeStruct((m,) + src.shape[1:], src.dtype),
        scratch_shapes=[pltpu.SemaphoreType.DMA(())],
        compiler_params=pltpu.CompilerParams(dimension_semantics=("arbitrary",)), name="gather_rows",
    )(idx.reshape(m // GATHER_ROWS, 1, GATHER_ROWS), src)


def _swiglu_hidden_block(xs_ref, wg_ref, wu_ref, wd_ref, acc_ref, mm):
    x = xs_ref[...]
    tf = wg_ref.shape[2]
    cw = SWIGLU_COLS if tf % SWIGLU_COLS == 0 else tf
    part = None
    for c0 in range(0, tf, cw):
        a = mm(x, wg_ref[0, :, c0:c0 + cw])
        u = mm(x, wu_ref[0, :, c0:c0 + cw])
        d = mm(_silu(a) * u, wd_ref[0, c0:c0 + cw, :])
        part = d if part is None else part + d
    acc_ref[...] += part


def _swiglu_kernel(te_ref, nt_ref, x_ref, wg_ref, wu_ref, wd_ref, o_ref, acc_ref, xs_ref, *, precise):
    mm = functools.partial(_mm, precise=precise)
    i = pl.program_id(0)
    j = pl.program_id(1)

    @pl.when(i < nt_ref[0])
    def _():
        @pl.when(j == 0)
        def _():
            acc_ref[...] = jnp.zeros(acc_ref.shape, F32)
            xs_ref[...] = _load_rows(x_ref).astype(xs_ref.dtype)

        _swiglu_hidden_block(xs_ref, wg_ref, wu_ref, wd_ref, acc_ref, mm)

        @pl.when(j == pl.num_programs(1) - 1)
        def _():
            _store_rows(o_ref, acc_ref[...])

    @pl.when((i >= nt_ref[0]) & (j == pl.num_programs(1) - 1))
    def _():
        o_ref[...] = jnp.zeros(o_ref.shape, F32)


def _swiglu_gather_kernel(te_ref, nt_ref, idx_ref, idx_next_ref, h_ref, wg_ref, wu_ref, wd_ref, o_ref,
                          acc_ref, xs_ref, xbuf_ref, sem_ref):
    mm = functools.partial(_mm, precise=False)
    i = pl.program_id(0)
    j = pl.program_id(1)
    nt = nt_ref[0]
    tm = acc_ref.shape[0]
    slot = i % 2

    def start_rows(ids_ref, dst_slot):
        def issue(c, carry):
            base = pl.multiple_of(c * GATHER_UNROLL, GATHER_UNROLL)
            for r in range(GATHER_UNROLL):
                src0 = pl.multiple_of(ids_ref[0, 0, base + r], SUBLANE)
                dst0 = pl.multiple_of((base + r) * SUBLANE, SUBLANE)
                pltpu.make_async_copy(h_ref.at[pl.ds(src0, SUBLANE), :], xbuf_ref.at[dst_slot, pl.ds(dst0, SUBLANE), :],
                                      sem_ref.at[dst_slot]).start(priority=r % 2)
            return carry

        lax.fori_loop(0, tm // GATHER_UNROLL, issue, 0)

    @pl.when((i == 0) & (j == 0) & (nt > 0))
    def _():
        start_rows(idx_ref, 0)

    @pl.when(i < nt)
    def _():
        @pl.when(j == 0)
        def _():
            pltpu.make_async_copy(h_ref.at[pl.ds(0, tm * SUBLANE), :], xbuf_ref.at[slot], sem_ref.at[slot]).wait()
            acc_ref[...] = jnp.zeros(acc_ref.shape, F32)
            xs_ref[...] = jnp.concatenate([xbuf_ref[slot, pl.ds(s, tm, stride=SUBLANE), :] for s in range(SUBLANE)],
                                          axis=-1).astype(xs_ref.dtype)

            @pl.when(i + 1 < nt)
            def _():
                start_rows(idx_next_ref, 1 - slot)

        _swiglu_hidden_block(xs_ref, wg_ref, wu_ref, wd_ref, acc_ref, mm)

        @pl.when(j == pl.num_programs(1) - 1)
        def _():
            _store_rows(o_ref, acc_ref[...])

    @pl.when((i >= nt) & (j == pl.num_programs(1) - 1))
    def _():
        o_ref[...] = jnp.zeros(o_ref.shape, F32)


def _swiglu_gathered(h_tiled, src_rows8, tile_expert, n_tiles_used, wg, wu, wd, tm, tf):
    m = src_rows8.shape[0]
    n_tiles = m // tm
    f = wg.shape[2]
    idx = src_rows8.reshape(n_tiles, 1, tm)
    grid_spec = pltpu.PrefetchScalarGridSpec(
        num_scalar_prefetch=2, grid=(n_tiles, f // tf),
        in_specs=[pl.BlockSpec((1, 1, tm), lambda i, j, te, nt: (i, 0, 0), memory_space=pltpu.SMEM),
                  pl.BlockSpec((1, 1, tm), lambda i, j, te, nt: (jnp.minimum(i + 1, n_tiles - 1), 0, 0),
                               memory_space=pltpu.SMEM),
                  pl.BlockSpec(memory_space=pl.ANY),
                  pl.BlockSpec((1, D_MODEL, tf), lambda i, j, te, nt: (te[i], 0, j)),
                  pl.BlockSpec((1, D_MODEL, tf), lambda i, j, te, nt: (te[i], 0, j)),
                  pl.BlockSpec((1, tf, D_MODEL), lambda i, j, te, nt: (te[i], j, 0))],
        out_specs=pl.BlockSpec((tm * SUBLANE, LANE), lambda i, j, te, nt: (i, 0)),
        scratch_shapes=[pltpu.VMEM((tm, D_MODEL), F32), pltpu.VMEM((tm, D_MODEL), BF16),
                        pltpu.VMEM((2, tm * SUBLANE, LANE), F32), pltpu.SemaphoreType.DMA((2,))])
    return pl.pallas_call(
        _swiglu_gather_kernel, grid_spec=grid_spec,
        out_shape=jax.ShapeDtypeStruct((m * SUBLANE, LANE), F32),
        compiler_params=_cparams(("arbitrary", "arbitrary")), name="swiglu_gather",
    )(tile_expert, n_tiles_used, idx, idx, h_tiled, wg, wu, wd)


def _swiglu_grouped(x, tile_expert, n_tiles_used, wg, wu, wd, tm, tf, precise=False):
    m = x.shape[0] // SUBLANE if _is_tiled(x.shape) else x.shape[0]
    f = wg.shape[2]
    grid_spec = pltpu.PrefetchScalarGridSpec(
        num_scalar_prefetch=2, grid=(m // tm, f // tf),
        in_specs=[_block_rows(tm, x, lambda i, j, te, nt: (i, 0)),
                  pl.BlockSpec((1, D_MODEL, tf), lambda i, j, te, nt: (te[i], 0, j)),
                  pl.BlockSpec((1, D_MODEL, tf), lambda i, j, te, nt: (te[i], 0, j)),
                  pl.BlockSpec((1, tf, D_MODEL), lambda i, j, te, nt: (te[i], j, 0))],
        out_specs=_block_rows(tm, x, lambda i, j, te, nt: (i, 0)),
        scratch_shapes=[pltpu.VMEM((tm, D_MODEL), F32), pltpu.VMEM((tm, D_MODEL), F32 if precise else BF16)])
    return pl.pallas_call(
        functools.partial(_swiglu_kernel, precise=precise), grid_spec=grid_spec,
        out_shape=jax.ShapeDtypeStruct(x.shape, F32),
        compiler_params=_cparams(("arbitrary", "arbitrary")), name="swiglu",
    )(tile_expert, n_tiles_used, x, wg, wu, wd)


def _combine_kernel(*refs, moe, final):
    refs = list(refs)
    x_ref = refs.pop(0)
    gate_ref = refs.pop(0)
    y1_ref = refs.pop(0)
    if moe:
        y2_ref = refs.pop(0)
        gates_ref = refs.pop(0)
    fg_ref = refs.pop(0) if final else None
    o_ref = refs.pop(0)
    y = _load_rows(y1_ref)
    if moe:
        gts = gates_ref[...]
        y = gts[:, 0:1] * y + gts[:, 1:2] * _load_rows(y2_ref)
    out = x_ref[...] + gate_ref[0] * y
    if final:
        out = _rms(out) * fg_ref[...]
    o_ref[...] = out


def _combine(x2, gate, y, y_row0, y2_row0, gates, final_g, n_seq, t_len, tm):
    nblk = t_len // tm
    n_tok = n_seq * t_len
    moe = gates is not None
    final = final_g is not None

    def row(i):
        return (i, 0)

    in_specs = [pl.BlockSpec((tm, D_MODEL), row), pl.BlockSpec((1, gate.shape[1], D_MODEL), lambda i: (i // nblk, 0, 0)),
                _block_rows(tm, y, lambda i: (y_row0 // tm + i, 0))]
    args = [x2, gate, y]
    if moe:
        in_specs += [_block_rows(tm, y, lambda i: (y2_row0 // tm + i, 0)), pl.BlockSpec((tm, LANE), row)]
        args += [y, gates]
    if final:
        in_specs.append(pl.BlockSpec((1, D_MODEL), lambda i: (0, 0)))
        args.append(final_g)
    return pl.pallas_call(
        functools.partial(_combine_kernel, moe=moe, final=final), grid=(n_tok // tm,),
        in_specs=in_specs, out_specs=pl.BlockSpec((tm, D_MODEL), row),
        out_shape=jax.ShapeDtypeStruct((n_tok, D_MODEL), F32),
        compiler_params=_cparams(("arbitrary",)), name="combine",
    )(*args)


def _pack_mixer_weights(w_in, w_gate2, b_gate, fox_b_f, qng, kvng, w_uq, w_uk, w_uv, gla_norm_g):
    offs = np.concatenate([[0], np.cumsum(IN_SPLITS)])
    cols = {n: (int(offs[i]), int(offs[i + 1])) for i, n in enumerate(
        ("gq", "gk", "gv", "ag", "ar", "fq", "fk", "fv", "bf", "cq", "ckv", "kr"))}

    def seg(n):
        return w_in[:, cols[n][0]:cols[n][1]]

    half = ROPE_C // 2
    kr = seg("kr")

    def zcols(n):
        return jnp.zeros((D_MODEL, n), F32)

    small = jnp.concatenate([kr, seg("ar"), seg("bf"), zcols(SM_KRS - SM_BF - H_B), -kr[:, half:], kr[:, :half],
                             zcols(LANE - SM_KRS - ROPE_C)], axis=1)
    w = jnp.concatenate([seg("gq"), zcols(C_GK - QK_A), seg("gk"), zcols(C_GV - C_GK - QK_A), seg("gv"), seg("ag"),
                         seg("fq"), seg("fk"), seg("fv"), seg("cq"), seg("ckv"), small], axis=1)
    assert w.shape == (D_MODEL, N_PACK) and small.shape[1] == LANE
    wg2 = jnp.pad(w_gate2, ((SM_AR, LANE - SM_AR - GATE_RANK), (0, 2 * LANE - QK_A)))
    bg = jnp.pad(b_gate, (0, 2 * LANE - QK_A)).reshape(1, 2 * LANE)
    bf = jnp.pad(fox_b_f, (SM_BF, LANE - SM_BF - H_B)).reshape(1, LANE)
    uq = w_uq.reshape(Q_LORA, H_C, NOPE_C + ROPE_C)
    wuqn = uq[:, :, :NOPE_C].reshape(Q_LORA, H_C * NOPE_C)
    x1 = uq[:, :, NOPE_C:NOPE_C + half]
    x2 = uq[:, :, NOPE_C + half:]
    pad = jnp.zeros((Q_LORA, H_C, LANE - ROPE_C), F32)
    wr = jnp.concatenate([x1, x2, pad], axis=2).reshape(Q_LORA, H_C * LANE)
    wrs = jnp.concatenate([-x2, x1, pad], axis=2).reshape(Q_LORA, H_C * LANE)
    eye = jnp.eye(H_C, dtype=F32)
    wuk = (jnp.transpose(w_uk, (1, 2, 0))[:, :, None, :] * eye[:, None, :, None]).reshape(H_C * NOPE_C, H_C * KV_LORA)
    wuv = (jnp.transpose(w_uv, (1, 0, 2))[:, :, None, :] * eye[:, None, :, None]).reshape(H_C, KV_LORA, V_A)
    full = dict(w_in=w, wg2=wg2, bg=bg, bf=bf, qng=qng.reshape(1, Q_LORA), kvng=kvng.reshape(1, KV_LORA), wuqn=wuqn,
                wuk=wuk, wr=wr, wrs=wrs, wuv=wuv, gnorm=jnp.tile(gla_norm_g, H_A).reshape(1, V_A))
    half_prec = dict(full)
    for n in ("w_in", "wg2", "wuqn", "wuk", "wr", "wrs", "wuv"):
        half_prec[n] = full[n].astype(BF16)
    return half_prec, full


def _rope_tables(pos):
    half = ROPE_C // 2
    inv_freq = ROPE_BASE ** (-jnp.arange(half, dtype=F32) / half)
    ang = pos.astype(F32)[:, None] * inv_freq[None, :]
    n = pos.shape[0]
    cos = jnp.concatenate([jnp.cos(ang), jnp.cos(ang), jnp.ones((n, LANE - ROPE_C), F32)], axis=1)
    sin = jnp.concatenate([jnp.sin(ang), jnp.sin(ang), jnp.zeros((n, LANE - ROPE_C), F32)], axis=1)
    return cos, sin


def _state_to_t(s):
    b = s.shape[0]
    eye = jnp.eye(H_A, dtype=F32)
    s_vk = jnp.swapaxes(s, 2, 3)
    return (s_vk[:, :, :, None, :] * eye[None, :, None, :, None]).reshape(b, V_A, QK_A)


def _state_from_t(s_t):
    b = s_t.shape[0]
    blocks = s_t.reshape(b, H_A, DV_A, H_A, DK_A)
    diag = jnp.stack([blocks[:, hh, :, hh, :] for hh in range(H_A)], axis=1)
    return jnp.swapaxes(diag, 2, 3)


def _round_up(a, b):
    return (a + b - 1) // b * b


def kernel(x_prompt, x_sample, c_prompt, c_sample, cache_fox_k, cache_fox_v, cache_fox_logf, cache_mla_ckv, cache_mla_krope, state_gla, ada_w, ada_b, norm_mix_g, norm_ffn_g, w_in, gla_w_gate2, gla_b_gate, gla_norm_g, fox_b_f, mla_q_norm_g, mla_kv_norm_g, mla_w_uq, mla_w_uk, mla_w_uv, w_out, ffn_w_gate, ffn_w_up, ffn_w_down, moe_router, moe_w_gate, moe_w_up, moe_w_down, final_norm_g):
    bp, tp, _ = x_prompt.shape
    bs, ts, _ = x_sample.shape
    past = cache_fox_k.shape[2]
    np_tok, ns_tok = bp * tp, bs * ts
    n_all = np_tok + ns_tok

    tm_p, tm_s = 256, ns_tok
    tq_fox, tk = 512, 512

    nc = _round_up(bp + bs, 8)
    c_all = jnp.zeros((nc, D_MODEL), F32).at[:bp].set(c_prompt).at[bp:bp + bs].set(c_sample)
    mod = _ada(c_all, ada_w, ada_b)

    cos_p, sin_p = _rope_tables(jnp.arange(tp))
    cos_s, sin_s = _rope_tables(past + jnp.tile(jnp.arange(ts), bs))

    def seq_major(a):
        hh, d = a.shape[1], a.shape[3]
        return jnp.transpose(a.reshape(hh, bs, ts, d), (1, 0, 2, 3))

    def tok_major(a):
        hh, d = a.shape[1], a.shape[3]
        return jnp.transpose(a, (1, 0, 2, 3)).reshape(1, hh, bs * ts, d)

    cl = jnp.transpose(cache_fox_logf.astype(F32), (0, 1, 3, 2)).reshape(DEPTH * bs * H_B, past)
    f_cache, f_cache2 = _cumsum(cl, jnp.zeros((cl.shape[0], 1), F32), 512)
    f_cache = f_cache.reshape(DEPTH, bs * H_B, past)
    f_cache2 = f_cache2.reshape(DEPTH, bs, H_B, past // tk, tk)
    cache_k2d = cache_fox_k.astype(F32).reshape(DEPTH, bs, past, H_B * D_B)
    cache_v2d = cache_fox_v.astype(F32).reshape(DEPTH, bs, past, H_B * D_B)

    xp = x_prompt.reshape(np_tok, D_MODEL)
    xs = x_sample.reshape(ns_tok, D_MODEL)
    p_states = [[] for _ in range(6)]
    s_states = [[] for _ in range(6)]

    for l in range(DEPTH):
        mods = [mod[l, :, i * D_MODEL:(i + 1) * D_MODEL] for i in range(6)]
        mp = [m[:bp].reshape(bp, 1, D_MODEL) for m in mods]
        ms = [jnp.repeat(m[bp:bp + bs], ts, axis=0).reshape(1, ns_tok, D_MODEL) for m in mods]
        pw, pw32 = _pack_mixer_weights(w_in[l], gla_w_gate2[l], gla_b_gate[l], fox_b_f[l], mla_q_norm_g[l],
                                       mla_kv_norm_g[l], mla_w_uq[l], mla_w_uk[l], mla_w_uv[l], gla_norm_g[l])
        wo = w_out[l].astype(BF16)
        g_mix = norm_mix_g[l].reshape(1, D_MODEL)
        g_ffn = norm_ffn_g[l].reshape(1, D_MODEL)
        moe = l % 2 == 1
        if moe:
            router_pad = jnp.pad(moe_router[l // 2], ((0, 0), (0, LANE - N_EXPERTS)))
            hbuf_p, hbuf_s = jnp.zeros((n_all * SUBLANE, LANE), F32), None
        else:
            router_pad = None
            hbuf_p, hbuf_s = jnp.zeros((np_tok, D_MODEL), F32), jnp.zeros((ns_tok, D_MODEL), F32)

        (gq, gk, gv, ag, gla, fk, fv, ckv, kc, small, qs, fqh, fkh, fvh) = _inproj(
            xp, mp[0], mp[1], g_mix, pw, cos_p, sin_p, bp, tp, tm_p, False, True)
        logf = small[:, SM_BF:SM_BF + H_B]
        krope = small[:, SM_KR:SM_KR + ROPE_C]
        ya, s_t = _gla(gq, gk, gv, gla, ag, pw["gnorm"], jnp.zeros((bp, V_A, QK_A), F32), bp, tp, 512, CHUNK, bp, False)
        g_fox = bp * H_B
        fqa, fka = _fox_prep(small, fqh, fkh, bp, tp, 512)
        o_fox = _flash(fqa.reshape(g_fox, tp // tq_fox, tq_fox, 2 * D_B), fka.reshape(g_fox, tp, 2 * D_B),
                       fvh.reshape(g_fox, tp, 2 * D_B), None, None,
                       tq=tq_fox, rep=1, tk=tk, q0=0, mode="causal", kv_len=tp, dv=D_B, hp=H_B)
        o_mla = _flash(qs.reshape(bp, tp // tm_p, H_C * tm_p, 2 * LANE), kc.reshape(bp, tp, 2 * LANE), None, None, None,
                       tq=tm_p, rep=H_C, tk=tk, q0=0, mode="chunk", kv_len=tp, dv=KV_LORA)
        res = _mixout(xp, mp[2], ya, o_fox.reshape(bp, H_B, tp, D_B), o_mla.reshape(np_tok // tm_p, H_C, tm_p, KV_LORA),
                      pw["wuv"], wo, mp[3], mp[4], g_ffn, router_pad, hbuf_p, 0, bp, tp, tm_p, False)
        xp, h_p = res[0], res[1]
        for i, st in enumerate((_state_from_t(s_t), fk.reshape(bp, tp, H_B, D_B), fv.reshape(bp, tp, H_B, D_B),
                                logf.reshape(bp, tp, H_B), ckv.reshape(bp, tp, KV_LORA), krope.reshape(bp, tp, ROPE_C))):
            p_states[i].append(st)

        (gq, gk, gv, ag, gla, fk, fv, ckv, kc, small, qs, fq) = _inproj(
            xs, ms[0], ms[1], g_mix, pw32, cos_s, sin_s, 1, ns_tok, tm_s, True, False)
        logf = small[:, SM_BF:SM_BF + H_B]
        krope = small[:, SM_KR:SM_KR + ROPE_C]
        g_fox = bs * H_B
        f_rows = jnp.transpose(logf.reshape(bs, ts, H_B), (0, 2, 1)).reshape(g_fox, ts)
        f_new = _cumsum(f_rows, f_cache[l][:, past - 1:past], ts)[1]
        fq_col = jnp.transpose(f_new.reshape(bs, H_B, ts), (0, 2, 1))
        ya, s_t = _gla(gq, gk, gv, gla, ag, pw["gnorm"], _state_to_t(state_gla[l].astype(F32)), bs, ts, ts, ts, 4, True)
        o_fox = _fox_decode(fq, cache_k2d, cache_v2d, l, fk, fv, fq_col, f_cache2, f_new, bs, ts, tk)
        o_mla = _mla_decode(seq_major(qs).reshape(bs, H_C * ts, 2 * LANE), cache_mla_ckv.astype(F32),
                            cache_mla_krope.astype(F32), l, ckv, kc, bs, ts, tk)
        res_s = _mixout(xs, ms[2], ya, o_fox, tok_major(o_mla.reshape(bs, H_C, ts, KV_LORA)), pw32["wuv"], w_out[l],
                        ms[3], ms[4], g_ffn, router_pad, h_p if moe else hbuf_s, np_tok if moe else 0,
                        1, ns_tok, tm_s, True)
        xs, h_s = res_s[0], res_s[1]
        for i, st in enumerate((_state_from_t(s_t), fk.reshape(bs, ts, H_B, D_B), fv.reshape(bs, ts, H_B, D_B),
                                logf.reshape(bs, ts, H_B), ckv.reshape(bs, ts, KV_LORA), krope.reshape(bs, ts, ROPE_C))):
            s_states[i].append(st)

        last = l == DEPTH - 1
        fg = final_norm_g.reshape(1, D_MODEL) if last else None
        tm_f = 512
        if l % 2 == 0:
            j = l // 2
            n_tiles = np_tok // tm_f
            y_p = _swiglu_grouped(h_p, jnp.zeros((n_tiles,), jnp.int32), jnp.full((1,), n_tiles, jnp.int32),
                                  ffn_w_gate[j:j + 1].astype(BF16), ffn_w_up[j:j + 1].astype(BF16),
                                  ffn_w_down[j:j + 1].astype(BF16), tm_f, ffn_w_gate.shape[2])
            y_s = _swiglu_grouped(h_s, jnp.zeros((1,), jnp.int32), jnp.ones((1,), jnp.int32),
                                  ffn_w_gate[j:j + 1], ffn_w_up[j:j + 1], ffn_w_down[j:j + 1], tm_s, 1408, precise=True)
            xp = _combine(xp, mp[5], y_p, 0, 0, None, fg, bp, tp, tm_p)
            xs = _combine(xs, ms[5], y_s, 0, 0, None, fg, 1, ns_tok, tm_s)
        else:
            j = l // 2
            h_all = h_s
            ids_p, gates_p = res[2:]
            ids_s, gates_s = res_s[2:]
            ids = jnp.concatenate([ids_p[:, :2], ids_s[:, :2]], axis=0)
            e = jnp.transpose(ids).reshape(-1)
            onehot = (e[:, None] == jnp.arange(N_EXPERTS)[None, :]).astype(jnp.int32)
            rank = jnp.sum((jnp.cumsum(onehot, axis=0) - onehot) * onehot, axis=1)
            counts = jnp.sum(onehot, axis=0)
            padded = (counts + tm_f - 1) // tm_f * tm_f
            ends = jnp.cumsum(padded)
            starts = ends - padded
            pos = starts[e] + rank
            m_pad = _round_up(2 * n_all + N_EXPERTS * (tm_f - 1), tm_f)
            token = jnp.tile(jnp.arange(n_all, dtype=jnp.int32), 2)
            src = jnp.zeros((m_pad,), jnp.int32).at[pos].set(token)
            n_tiles = m_pad // tm_f
            tile_row0 = jnp.arange(n_tiles, dtype=jnp.int32) * tm_f
            tile_expert = jnp.minimum(jnp.sum((ends[None, :] <= tile_row0[:, None]).astype(jnp.int32), axis=1),
                                      N_EXPERTS - 1)
            n_used = (ends[-1] // tm_f).astype(jnp.int32).reshape(1)
            y = _swiglu_gathered(h_all, src * SUBLANE, tile_expert, n_used, moe_w_gate[j].astype(BF16),
                                 moe_w_up[j].astype(BF16), moe_w_down[j].astype(BF16), tm_f, 1792)
            n_back = _round_up(n_all, GATHER_ROWS)
            back = jnp.zeros((2 * n_back,), jnp.int32).at[:n_all].set(pos[:n_all]).at[n_back:n_back + n_all].set(pos[n_all:])
            yg = _tiled(_gather_rows(_untiled(y), back))
            xp = _combine(xp, mp[5], yg, 0, n_back, gates_p, fg, bp, tp, tm_p)
            xs = _combine(xs, ms[5], yg, np_tok, n_back + np_tok, gates_s, fg, 1, ns_tok, tm_s)

    outs_p = [jnp.stack(s, axis=0) for s in p_states]
    outs_s = [jnp.stack(s, axis=0) for s in s_states]
    return (xp.reshape(bp, tp, D_MODEL), xs.reshape(bs, ts, D_MODEL), *outs_p, *outs_s)
```

```python
import functools

import numpy as np
import jax
import jax.numpy as jnp
from jax import lax
from jax.experimental import pallas as pl
from jax.experimental.pallas import tpu as pltpu

F32 = jnp.float32
BF16 = jnp.bfloat16
HIGHEST = lax.Precision.HIGHEST

D_MODEL = 1024
DEPTH = 2
CHUNK = 64
EPS = 1e-6
H_A, DK_A, DV_A = 6, 32, 64
GATE_RANK = 16
GATE_TAU = 16.0
H_B, D_B = 4, 64
H_C, NOPE_C, ROPE_C, V_C = 6, 64, 32, 64
Q_LORA, KV_LORA = 256, 128
ROPE_BASE = 10000.0
N_EXPERTS = 8
IN_SPLITS = (H_A * DK_A, H_A * DK_A, H_A * DV_A, H_A * DV_A, GATE_RANK,
             H_B * D_B, H_B * D_B, H_B * D_B, H_B, Q_LORA, KV_LORA, ROPE_C)

QK_A = H_A * DK_A
V_A = H_A * DV_A
QKV_B = H_B * D_B
LANE = 128
NEG = -1e30

C_GQ, C_GK, C_GV, C_AG = 0, 256, 512, 896
C_FQ, C_FK, C_FV = 1280, 1536, 1792
C_CQ, C_CKV, C_SM = 2048, 2304, 2432
N_PACK = 2560
SM_KR, SM_AR, SM_BF, SM_KRS = 0, 32, 48, 64

GLA_SUB = 16
FLASH_ROW_GROUP = 512
SWIGLU_COLS = 256
LOG2E = float(np.log2(np.e))
VMEM_LIMIT = 56 * 1024 * 1024


def _cparams(sem):
    return pltpu.CompilerParams(dimension_semantics=sem, vmem_limit_bytes=VMEM_LIMIT)


def _log_sigmoid(z):
    return jnp.minimum(z, 0.0) - jnp.log1p(jnp.exp(-jnp.abs(z)))


def _silu(z):
    return z * (1.0 / (1.0 + jnp.exp(-z)))


def _rms(x):
    return x * lax.rsqrt(jnp.mean(x * x, axis=-1, keepdims=True) + EPS)


def _dot(a, b):
    return jnp.dot(a, b, preferred_element_type=F32)


def _dot_nt(a, b):
    return lax.dot_general(a, b, (((1,), (1,)), ((), ())), preferred_element_type=F32)


def _dot_tn(a, b):
    return lax.dot_general(a, b, (((0,), (0,)), ((), ())), preferred_element_type=F32)


def _split2(a):
    a = a.astype(F32)
    hi = a.astype(BF16)
    return hi, (a - hi.astype(F32)).astype(BF16)


def _dot_f32(a, b, dims=(((1,), (0,)), ((), ()))):
    a_hi, a_lo = _split2(a)
    b_hi, b_lo = _split2(b)

    def dg(x, y):
        return lax.dot_general(x, y, dims, preferred_element_type=F32)

    return dg(a_hi, b_hi) + (dg(a_hi, b_lo) + dg(a_lo, b_hi))


def _mm(a, b, dims=(((1,), (0,)), ((), ())), *, precise):
    if precise:
        return _dot_f32(a, b, dims)
    return lax.dot_general(a.astype(BF16), b.astype(BF16), dims, preferred_element_type=F32)


_NT = (((1,), (1,)), ((), ()))
_TN = (((0,), (0,)), ((), ()))

SUBLANE = 8
assert D_MODEL == SUBLANE * LANE


def _is_tiled(shape):
    return len(shape) == 2 and shape[1] == LANE


def _tiled(a):
    return a.reshape(a.shape[0] * SUBLANE, LANE)


def _untiled(a):
    return a.reshape(a.shape[0] // SUBLANE, SUBLANE, LANE)


def _block_rows(tm, arr, index_map):
    if _is_tiled(arr.shape):
        return pl.BlockSpec((tm * SUBLANE, LANE), index_map)
    return pl.BlockSpec((tm, D_MODEL), index_map)


def _load_rows(ref):
    if _is_tiled(ref.shape):
        tm = ref.shape[0] // SUBLANE
        return jnp.concatenate([ref[pl.ds(s, tm, stride=SUBLANE), :] for s in range(SUBLANE)], axis=-1)
    return ref[...]


def _store_rows(ref, val):
    if _is_tiled(ref.shape):
        tm = ref.shape[0] // SUBLANE
        for s in range(SUBLANE):
            ref[pl.ds(s, tm, stride=SUBLANE), :] = val[:, s * LANE:(s + 1) * LANE]
    else:
        ref[...] = val


def _ada_kernel(c_ref, w_ref, b_ref, o_ref):
    s = _silu(c_ref[...])
    o_ref[0] = jnp.dot(s, w_ref[0], precision=HIGHEST, preferred_element_type=F32) + b_ref[0]


def _ada(c_all, ada_w, ada_b):
    nc = c_all.shape[0]
    tn = 1536
    return pl.pallas_call(
        _ada_kernel,
        grid=(DEPTH, 6 * D_MODEL // tn),
        in_specs=[pl.BlockSpec((nc, D_MODEL), lambda l, j: (0, 0)),
                  pl.BlockSpec((1, D_MODEL, tn), lambda l, j: (l, 0, j)),
                  pl.BlockSpec((1, 1, tn), lambda l, j: (l, 0, j))],
        out_specs=pl.BlockSpec((1, nc, tn), lambda l, j: (l, 0, j)),
        out_shape=jax.ShapeDtypeStruct((DEPTH, nc, 6 * D_MODEL), F32),
        compiler_params=_cparams(("arbitrary", "arbitrary")),
        name="ada",
    )(c_all, ada_w, ada_b.reshape(DEPTH, 1, 6 * D_MODEL))


def _inproj_kernel(x_ref, sh_ref, sc_ref, g_ref, w_ref, wg2_ref, bg_ref, bf_ref, qng_ref, kvng_ref,
                   wuqn_ref, wuk_ref, wr_ref, wrs_ref, cos_ref, sin_ref,
                   gq_ref, gk_ref, gv_ref, ag_ref, gla_ref, fk_ref, fv_ref,
                   ckv_ref, kc_ref, small_ref, qs_ref, *fox_refs, precise):
    mm = functools.partial(_mm, precise=precise)
    act = kc_ref.dtype
    x = x_ref[...]
    h = (_rms(x) * g_ref[...]) * (1.0 + sc_ref[0]) + sh_ref[0]
    p = mm(h, w_ref[...])

    gq_ref[...] = p[:, C_GQ:C_GQ + QK_A] * (DK_A ** -0.5)
    gk_ref[...] = p[:, C_GK:C_GK + QK_A]
    gv_ref[...] = p[:, C_GV:C_GV + V_A]
    ag_ref[...] = p[:, C_AG:C_AG + V_A]
    sm = p[:, C_SM:C_SM + LANE]
    z = mm(sm, wg2_ref[...]) + bg_ref[...]
    gla_ref[...] = _log_sigmoid(z[:, :QK_A]) * (1.0 / GATE_TAU)

    fq = p[:, C_FQ:C_FQ + QKV_B] * (D_B ** -0.5 * LOG2E)
    fk = p[:, C_FK:C_FK + QKV_B]
    fv = p[:, C_FV:C_FV + QKV_B]
    fk_ref[...] = fk
    fv_ref[...] = fv
    if len(fox_refs) == 1:
        fox_refs[0][...] = fq
    else:
        fqh_ref, fkh_ref, fvh_ref = fox_refs
        for hh in range(H_B):
            sl = slice(hh * D_B, (hh + 1) * D_B)
            fqh_ref[0, hh] = fq[:, sl].astype(act)
            fkh_ref[0, hh] = fk[:, sl].astype(act)
            fvh_ref[0, hh, :, :D_B] = fv[:, sl].astype(act)
            fvh_ref[0, hh, :, D_B:] = jnp.ones((fv.shape[0], D_B), act)

    cos = cos_ref[...]
    sin = sin_ref[...]
    lane = lax.broadcasted_iota(jnp.int32, sm.shape, 1)
    kr = sm * cos + pltpu.roll(sm, LANE - SM_KRS, 1) * sin
    logf = _log_sigmoid(sm + bf_ref[...])
    small_ref[...] = jnp.where((lane >= SM_BF) & (lane < SM_BF + H_B), logf, kr)

    ckv = _rms(p[:, C_CKV:C_CKV + KV_LORA]) * kvng_ref[...]
    ckv_ref[...] = ckv
    kc_ref[:, :KV_LORA] = ckv.astype(act)
    kc_ref[:, KV_LORA:] = jnp.where(lane < ROPE_C, kr, 0.0).astype(act)
    cqn = _rms(p[:, C_CQ:C_CQ + Q_LORA]) * qng_ref[...]
    nope = mm(cqn, wuqn_ref[...])
    qlat = mm(nope, wuk_ref[...])
    qa = mm(cqn, wr_ref[...])
    qb = mm(cqn, wrs_ref[...])
    scale = (NOPE_C + ROPE_C) ** -0.5 * LOG2E
    for hh in range(H_C):
        sl = slice(hh * LANE, (hh + 1) * LANE)
        qs_ref[0, hh, :, :KV_LORA] = (qlat[:, sl] * scale).astype(act)
        qs_ref[0, hh, :, KV_LORA:] = ((qa[:, sl] * cos + qb[:, sl] * sin) * scale).astype(act)


def _inproj(x2, shift, scale, g, pw, cos_tab, sin_tab, n_seq, t_len, tm, precise, fox_head_major):
    n_tok = n_seq * t_len
    nblk = t_len // tm
    act = F32 if precise else BF16
    mod_rows = shift.shape[1]

    def row(i):
        return (i, 0)

    def seq(i):
        return (i // nblk, 0, 0)

    def const2(i):
        return (0, 0)

    def tab(i):
        return (i % nblk, 0)

    def headmajor(i):
        return (i // nblk, 0, i % nblk, 0)

    sds = jax.ShapeDtypeStruct
    out_shape = (
        sds((n_tok, QK_A), F32), sds((n_tok, QK_A), F32), sds((n_tok, V_A), F32), sds((n_tok, V_A), F32),
        sds((n_tok, QK_A), F32),
        sds((n_tok, QKV_B), F32), sds((n_tok, QKV_B), F32),
        sds((n_tok, KV_LORA), F32), sds((n_tok, 2 * LANE), act), sds((n_tok, LANE), F32),
        sds((n_tok // tm, H_C, tm, 2 * LANE), act),
    )
    out_specs = (
        pl.BlockSpec((tm, QK_A), row), pl.BlockSpec((tm, QK_A), row), pl.BlockSpec((tm, V_A), row),
        pl.BlockSpec((tm, V_A), row), pl.BlockSpec((tm, QK_A), row),
        pl.BlockSpec((tm, QKV_B), row), pl.BlockSpec((tm, QKV_B), row),
        pl.BlockSpec((tm, KV_LORA), row), pl.BlockSpec((tm, 2 * LANE), row), pl.BlockSpec((tm, LANE), row),
        pl.BlockSpec((1, H_C, tm, 2 * LANE), lambda i: (i, 0, 0, 0)),
    )
    if fox_head_major:
        out_shape += (sds((n_seq, H_B, t_len, D_B), act), sds((n_seq, H_B, t_len, D_B), act),
                      sds((n_seq, H_B, t_len, 2 * D_B), act))
        out_specs += (pl.BlockSpec((1, H_B, tm, D_B), headmajor), pl.BlockSpec((1, H_B, tm, D_B), headmajor),
                      pl.BlockSpec((1, H_B, tm, 2 * D_B), headmajor))
    else:
        out_shape += (sds((n_tok, QKV_B), F32),)
        out_specs += (pl.BlockSpec((tm, QKV_B), row),)
    in_specs = [
        pl.BlockSpec((tm, D_MODEL), row), pl.BlockSpec((1, mod_rows, D_MODEL), seq),
        pl.BlockSpec((1, mod_rows, D_MODEL), seq), pl.BlockSpec((1, D_MODEL), const2),
        pl.BlockSpec((D_MODEL, N_PACK), const2), pl.BlockSpec((LANE, 2 * LANE), const2),
        pl.BlockSpec((1, 2 * LANE), const2), pl.BlockSpec((1, LANE), const2),
        pl.BlockSpec((1, Q_LORA), const2), pl.BlockSpec((1, KV_LORA), const2),
        pl.BlockSpec((Q_LORA, H_C * NOPE_C), const2), pl.BlockSpec((H_C * NOPE_C, H_C * KV_LORA), const2),
        pl.BlockSpec((Q_LORA, H_C * LANE), const2), pl.BlockSpec((Q_LORA, H_C * LANE), const2),
        pl.BlockSpec((tm, LANE), tab), pl.BlockSpec((tm, LANE), tab),
    ]
    return pl.pallas_call(
        functools.partial(_inproj_kernel, precise=precise), grid=(n_tok // tm,), in_specs=in_specs,
        out_specs=out_specs, out_shape=out_shape, compiler_params=_cparams(("arbitrary",)), name="inproj",
    )(x2, shift, scale, g, pw["w_in"], pw["wg2"], pw["bg"], pw["bf"], pw["qng"], pw["kvng"],
      pw["wuqn"], pw["wuk"], pw["wr"], pw["wrs"], cos_tab, sin_tab)


def _cumsum_kernel(x_ref, init_ref, o_ref, o2_ref, *, tb):
    n = x_ref.shape[1]
    upper = (lax.broadcasted_iota(jnp.int32, (tb, tb), 0) <= lax.broadcasted_iota(jnp.int32, (tb, tb), 1)).astype(F32)
    carry = init_ref[...]
    for j in range(n // tb):
        blk = jnp.dot(x_ref[:, j * tb:(j + 1) * tb], upper, precision=HIGHEST, preferred_element_type=F32) + carry
        o_ref[:, j * tb:(j + 1) * tb] = blk
        o2_ref[:, j * tb:(j + 1) * tb] = blk * LOG2E
        carry = blk[:, tb - 1:tb]


def _cumsum(x, init, tb):
    return pl.pallas_call(
        functools.partial(_cumsum_kernel, tb=tb),
        out_shape=(jax.ShapeDtypeStruct(x.shape, F32), jax.ShapeDtypeStruct(x.shape, F32)), name="cumsum",
    )(x, init)


def _fox_prep_kernel(sm_ref, q_ref, k_ref, qa_ref, ka_ref, carry_ref, *, tb):
    @pl.when(pl.program_id(1) == 0)
    def _():
        carry_ref[...] = jnp.zeros(carry_ref.shape, F32)

    tril = (lax.broadcasted_iota(jnp.int32, (tb, tb), 0) >= lax.broadcasted_iota(jnp.int32, (tb, tb), 1)).astype(BF16)
    x = sm_ref[...]
    x_hi = x.astype(BF16)
    x_mid = (x - x_hi.astype(F32)).astype(BF16)
    x_lo = (x - x_hi.astype(F32) - x_mid.astype(F32)).astype(BF16)
    cum = _dot(tril, x_hi) + _dot(tril, x_mid) + _dot(tril, x_lo) + carry_ref[...]
    carry_ref[...] = cum[tb - 1:tb]
    lane = lax.broadcasted_iota(jnp.int32, (tb, D_B), 1)
    for hh in range(H_B):
        f = jnp.broadcast_to(cum[:, SM_BF + hh:SM_BF + hh + 1] * LOG2E, (tb, D_B))
        hi = f.astype(BF16).astype(F32)
        mid = (f - hi).astype(BF16).astype(F32)
        lo = (f - hi - mid).astype(BF16).astype(F32)
        terms = jnp.where(lane % 3 == 0, hi, jnp.where(lane % 3 == 1, mid, lo))
        q_extra = jnp.where(lane < 3, terms, jnp.where(lane < 6, 1.0, 0.0))
        k_extra = jnp.where(lane < 3, 1.0, jnp.where(lane < 6, -terms, 0.0))
        qa_ref[0, hh, :, :D_B] = q_ref[0, hh]
        qa_ref[0, hh, :, D_B:] = q_extra.astype(BF16)
        ka_ref[0, hh, :, :D_B] = k_ref[0, hh]
        ka_ref[0, hh, :, D_B:] = k_extra.astype(BF16)


def _fox_prep(small, fqh, fkh, n_seq, t_len, tb):
    nblk = t_len // tb

    def hm(b, j):
        return (b, 0, j, 0)

    return pl.pallas_call(
        functools.partial(_fox_prep_kernel, tb=tb), grid=(n_seq, nblk),
        in_specs=[pl.BlockSpec((tb, LANE), lambda b, j: (b * nblk + j, 0)),
                  pl.BlockSpec((1, H_B, tb, D_B), hm), pl.BlockSpec((1, H_B, tb, D_B), hm)],
        out_specs=(pl.BlockSpec((1, H_B, tb, 2 * D_B), hm), pl.BlockSpec((1, H_B, tb, 2 * D_B), hm)),
        out_shape=(jax.ShapeDtypeStruct((n_seq, H_B, t_len, 2 * D_B), BF16),) * 2,
        scratch_shapes=[pltpu.VMEM((1, LANE), F32)],
        compiler_params=_cparams(("arbitrary", "arbitrary")), name="fox_prep",
    )(small, fqh, fkh)


def _gla_kernel(q_ref, k_ref, v_ref, la_ref, ag_ref, gn_ref, s0_ref, y_ref, sout_ref, s_ref, *, chunk, n_chunks,
                precise):
    mm = functools.partial(_mm, precise=precise)
    c = chunk
    nsub = c // GLA_SUB

    nb = q_ref.shape[0]

    @pl.when(pl.program_id(1) == 0)
    def _():
        s_ref[...] = s0_ref[...]

    lane_qk = lax.broadcasted_iota(jnp.int32, (GLA_SUB, QK_A), 1) // DK_A
    lane_v = lax.broadcasted_iota(jnp.int32, (GLA_SUB, V_A), 1) // DV_A
    bd = (lax.broadcasted_iota(jnp.int32, (V_A, QK_A), 0) // DV_A) == (lax.broadcasted_iota(jnp.int32, (V_A, QK_A), 1) // DK_A)
    tril = (lax.broadcasted_iota(jnp.int32, (c, c), 0) >= lax.broadcasted_iota(jnp.int32, (c, c), 1)).astype(F32)
    hm = (lax.broadcasted_iota(jnp.int32, (V_A, V_A), 0) // DV_A) == (lax.broadcasted_iota(jnp.int32, (V_A, V_A), 1) // DV_A)
    head_mean = jnp.where(hm, 1.0 / DV_A, 0.0).astype(F32)

    def cumsum_rows(la):
        if precise:
            return jnp.dot(tril, la, precision=HIGHEST, preferred_element_type=F32)
        hi = la.astype(BF16)
        lo = (la - hi.astype(F32)).astype(BF16)
        tb16 = tril.astype(BF16)
        return _dot(tb16, hi) + _dot(tb16, lo)

    def chunk_step(bb, r):
        q = q_ref[bb, pl.ds(r, c), :]
        k = k_ref[bb, pl.ds(r, c), :]
        v = v_ref[bb, pl.ds(r, c), :]
        la = la_ref[bb, pl.ds(r, c), :]
        b = cumsum_rows(la)
        s_t = s_ref[bb]
        vb = v if precise else v.astype(BF16)
        o_inter = mm(q * jnp.exp(b), s_t, _NT)
        outs = []
        for i in range(nsub):
            r0 = i * GLA_SUB
            r1 = r0 + GLA_SUB
            bi = b[r0 - 1:r0] if i > 0 else jnp.zeros((1, QK_A), F32)
            qi = q[r0:r1] * jnp.exp(b[r0:r1] - bi)
            kk = k[:r1] * jnp.exp(bi - b[:r1])
            qst = jnp.concatenate([jnp.where(lane_qk == hh, qi, 0.0) for hh in range(H_A)], axis=0)
            att = mm(qst, kk, _NT)
            t_idx = r0 + lax.broadcasted_iota(jnp.int32, att.shape, 0) % GLA_SUB
            s_idx = lax.broadcasted_iota(jnp.int32, att.shape, 1)
            att = jnp.where(s_idx <= t_idx, att, 0.0)
            oi = mm(att, vb[:r1])
            o = jnp.zeros((GLA_SUB, V_A), F32)
            for hh in range(H_A):
                o = o + jnp.where(lane_v == hh, oi[hh * GLA_SUB:(hh + 1) * GLA_SUB], 0.0)
            outs.append(o)
        o = jnp.concatenate(outs, axis=0) + o_inter if nsub > 1 else outs[0] + o_inter
        b_last = b[c - 1:c]
        kd = k * jnp.exp(b_last - b)
        s_ref[bb] = s_t * jnp.exp(b_last) + jnp.where(bd, mm(vb, kd, _TN), 0.0)
        ms = mm(o * o, head_mean)
        y = o * lax.rsqrt(ms + EPS) * gn_ref[...] * _silu(ag_ref[bb, pl.ds(r, c), :])
        y_ref[bb, pl.ds(r, c), :] = y.astype(y_ref.dtype)

    def body(ci, carry):
        r = pl.multiple_of(ci * c, c)
        for bb in range(nb):
            chunk_step(bb, r)
        return carry

    lax.fori_loop(0, n_chunks, body, 0, unroll=2 if n_chunks % 2 == 0 else 1)

    @pl.when(pl.program_id(1) == pl.num_programs(1) - 1)
    def _():
        sout_ref[...] = s_ref[...]


def _gla(gq, gk, gv, gla, ag, gnorm, s0_t, n_seq, t_len, tb, chunk, nb, precise):
    nblk = t_len // tb

    def row(b, j):
        return (b, j, 0)

    def st(b, j):
        return (b, 0, 0)

    def seq3(a):
        return a.reshape(n_seq, t_len, a.shape[-1])

    ya, s_t = pl.pallas_call(
        functools.partial(_gla_kernel, chunk=chunk, n_chunks=tb // chunk, precise=precise),
        grid=(n_seq // nb, nblk),
        in_specs=[pl.BlockSpec((nb, tb, QK_A), row), pl.BlockSpec((nb, tb, QK_A), row), pl.BlockSpec((nb, tb, V_A), row),
                  pl.BlockSpec((nb, tb, QK_A), row), pl.BlockSpec((nb, tb, V_A), row),
                  pl.BlockSpec((1, V_A), lambda b, j: (0, 0)), pl.BlockSpec((nb, V_A, QK_A), st)],
        out_specs=(pl.BlockSpec((nb, tb, V_A), row), pl.BlockSpec((nb, V_A, QK_A), st)),
        out_shape=(jax.ShapeDtypeStruct((n_seq, t_len, V_A), F32 if precise else BF16),
                   jax.ShapeDtypeStruct((n_seq, V_A, QK_A), F32)),
        scratch_shapes=[pltpu.VMEM((nb, V_A, QK_A), F32)],
        compiler_params=_cparams(("arbitrary", "arbitrary")), name="gla",
    )(seq3(gq), seq3(gk), seq3(gv), seq3(gla), seq3(ag), gnorm, s0_t)
    return ya.reshape(n_seq * t_len, V_A), s_t


def _flash_kernel(*refs, tq, rep, hp, tk, q0, mode, kv_len, dv, bias, v_from_k, precise):
    mm = functools.partial(_mm, precise=precise)
    refs = list(refs)
    q_ref = refs.pop(0)
    k_ref = refs.pop(0)
    v_ref = k_ref if v_from_k else refs.pop(0)
    fq_ref = refs.pop(0) if bias else None
    fk_ref = refs.pop(0) if bias else None
    o_ref, m_ref, acc_ref = refs
    i = pl.program_id(1)
    rows = rep * tq
    m_ref[...] = jnp.full(m_ref.shape, NEG, F32)
    acc_ref[...] = jnp.zeros(acc_ref.shape, F32)

    first_q = q0 + i * tq
    last_q = first_q + tq - 1
    if mode == "chunk":
        vis_all = (first_q // CHUNK) * CHUNK + CHUNK - 1
        vis_any = (last_q // CHUNK) * CHUNK + CHUNK - 1
    else:
        vis_all = first_q
        vis_any = last_q
    vis_all = jnp.minimum(vis_all, kv_len - 1)
    vis_any = jnp.minimum(vis_any, kv_len - 1)
    n_full = (vis_all + 1) // tk
    n_any = vis_any // tk + 1

    grp = min(rows, FLASH_ROW_GROUP)

    def step(jb, masked):
        for hh in range(hp):
            head_step(hh, jb, masked)

    def head_step(hh, jb, masked):
        k_start = pl.multiple_of(jb * tk, tk)
        k = k_ref[hh, pl.ds(k_start, tk), :]
        if v_from_k:
            v = jnp.where(lax.broadcasted_iota(jnp.int32, k.shape, 1) < dv, k, jnp.ones_like(k))
        else:
            v = v_ref[hh, pl.ds(k_start, tk), :]
        for g in range(rows // grp):
            rs = slice(g * grp, (g + 1) * grp)
            s = mm(q_ref[hh, 0, rs, :], k, _NT)
            if bias:
                s = s + fq_ref[hh, 0, rs, :] - fk_ref[hh, jb]
            rs = slice(hh * rows + g * grp, hh * rows + (g + 1) * grp)
            if masked:
                qpos = first_q + (g * grp + lax.broadcasted_iota(jnp.int32, (grp, tk), 0)) % tq
                kpos = k_start + lax.broadcasted_iota(jnp.int32, (grp, tk), 1)
                if mode == "chunk":
                    ok = (kpos // CHUNK) <= (qpos // CHUNK)
                else:
                    ok = kpos <= qpos
                ok = ok & (kpos < kv_len)
                s = jnp.where(ok, s, NEG)
            chunks = [s[:, c * LANE:(c + 1) * LANE] for c in range(tk // LANE)]
            smax = chunks[0]
            for ch in chunks[1:]:
                smax = jnp.maximum(smax, ch)
            m_prev = m_ref[rs, :]
            m_new = jnp.maximum(m_prev, jnp.max(smax, axis=-1, keepdims=True))
            alpha = jnp.exp2(m_prev - m_new)
            p = jnp.concatenate([jnp.exp2(ch - m_new) for ch in chunks], axis=1)
            acc = acc_ref[rs, :]
            alpha_w = alpha if acc.shape[1] == LANE else jnp.concatenate([alpha] * (acc.shape[1] // LANE), axis=1)
            acc_ref[rs, :] = alpha_w * acc + mm(p, v)
            m_ref[rs, :] = m_new

    def loop(lo, hi, masked):
        def body(jb, carry):
            step(jb, masked)
            return carry
        lax.fori_loop(lo, hi, body, 0)

    loop(0, n_full, False)
    loop(n_full, n_any, True)

    acc = acc_ref[...]
    if dv == LANE:
        out = acc[:, :dv] / acc[:, dv:]
    else:
        out = (acc / pltpu.roll(acc, LANE - dv, 1))[:, :dv]
    for hh in range(hp):
        o_ref[hh, 0] = out[hh * rows:(hh + 1) * rows].astype(o_ref.dtype)


def _flash(q, k, v, fq, fk, *, tq, rep, tk, q0, mode, kv_len, dv, hp=1, precise=False):
    g, nq, rows, dqk = q.shape
    t_k = k.shape[1]
    nk = t_k // tk
    bias = fq is not None
    v_from_k = v is None

    def qmap(b, i):
        return (b, i, 0, 0)

    def kmap(b, i):
        return (b, 0, 0)

    in_specs = [pl.BlockSpec((hp, 1, rows, dqk), qmap), pl.BlockSpec((hp, t_k, dqk), kmap)]
    args = [q, k]
    dva = dqk if v_from_k else v.shape[2]
    assert dva % LANE == 0 and dva > dv
    if not v_from_k:
        in_specs.append(pl.BlockSpec((hp, t_k, dva), kmap))
        args.append(v)
    if bias:
        in_specs += [pl.BlockSpec((hp, 1, rows, 1), qmap), pl.BlockSpec((hp, nk, 1, tk), lambda b, i: (b, 0, 0, 0))]
        args += [fq, fk.reshape(g, nk, 1, tk)]
    return pl.pallas_call(
        functools.partial(_flash_kernel, tq=tq, rep=rep, hp=hp, tk=tk, q0=q0, mode=mode, kv_len=kv_len, dv=dv,
                          bias=bias, v_from_k=v_from_k, precise=precise),
        grid=(g // hp, nq), in_specs=in_specs,
        out_specs=pl.BlockSpec((hp, 1, rows, dv), qmap),
        out_shape=jax.ShapeDtypeStruct((g, nq, rows, dv), F32 if precise else BF16),
        scratch_shapes=[pltpu.VMEM((hp * rows, LANE), F32), pltpu.VMEM((hp * rows, dva), F32)],
        compiler_params=_cparams(("arbitrary", "arbitrary")), name="flash_" + mode,
    )(*args)


_hdot = _dot_f32


def _online_softmax_step(s, v, m_ref, l_ref, acc_ref):
    m_prev = m_ref[...]
    m_new = jnp.maximum(m_prev, jnp.max(s, axis=-1, keepdims=True))
    alpha = jnp.exp2(m_prev - m_new)
    p = jnp.exp2(s - m_new)
    l_ref[...] = alpha * l_ref[...] + jnp.sum(p, axis=-1, keepdims=True)
    acc_ref[...] = alpha * acc_ref[...] + _hdot(p, v)
    m_ref[...] = m_new


def _fox_decode_kernel(q_ref, kc_ref, vc_ref, kn_ref, vn_ref, fq_ref, fkc_ref, fkn_ref, o_ref, m_ref, l_ref, acc_ref,
                       *, tk, past, ts):
    rows = H_B * ts
    q = q_ref[...]
    lane_head = lax.broadcasted_iota(jnp.int32, q.shape, 1) // D_B
    q_st = jnp.concatenate([jnp.where(lane_head == hh, q, 0.0) for hh in range(H_B)], axis=0)
    fq = fq_ref[0]
    fq_st = jnp.concatenate([fq[:, hh:hh + 1] for hh in range(H_B)], axis=0)
    m_ref[...] = jnp.full(m_ref.shape, NEG, F32)
    l_ref[...] = jnp.zeros(l_ref.shape, F32)
    acc_ref[...] = jnp.zeros(acc_ref.shape, F32)

    def cached(jb, carry):
        k0 = pl.multiple_of(jb * tk, tk)
        fk_st = jnp.concatenate([jnp.broadcast_to(fkc_ref[0, 0, hh, pl.ds(jb, 1), :], (ts, tk)) for hh in range(H_B)],
                                axis=0)
        s = _hdot(q_st, kc_ref[0, 0, pl.ds(k0, tk), :], _NT) + fq_st - fk_st
        _online_softmax_step(s, vc_ref[0, 0, pl.ds(k0, tk), :], m_ref, l_ref, acc_ref)
        return carry

    lax.fori_loop(0, past // tk, cached, 0)
    fkn = fkn_ref[0]
    fk_st = jnp.concatenate([jnp.broadcast_to(fkn[hh:hh + 1, :], (ts, ts)) for hh in range(H_B)], axis=0)
    s = _hdot(q_st, kn_ref[...], _NT) + fq_st - fk_st
    t_idx = lax.broadcasted_iota(jnp.int32, s.shape, 0) % ts
    s_idx = lax.broadcasted_iota(jnp.int32, s.shape, 1)
    _online_softmax_step(jnp.where(s_idx <= t_idx, s, NEG), vn_ref[...], m_ref, l_ref, acc_ref)
    res = acc_ref[...] / l_ref[...]
    out = jnp.zeros((ts, H_B * D_B), F32)
    for hh in range(H_B):
        out = out + jnp.where(lane_head == hh, res[hh * ts:(hh + 1) * ts], 0.0)
    o_ref[...] = out


def _fox_decode(q, cache_k, cache_v, layer, k_new, v_new, fq_col, f_cache, f_new, n_seq, ts, tk):
    past = cache_k.shape[2]
    width = H_B * D_B
    rows = H_B * ts

    def tok(b):
        return (b, 0)

    def cache(b):
        return (layer, b, 0, 0)

    return pl.pallas_call(
        functools.partial(_fox_decode_kernel, tk=tk, past=past, ts=ts), grid=(n_seq,),
        in_specs=[pl.BlockSpec((ts, width), tok), pl.BlockSpec((1, 1, past, width), cache),
                  pl.BlockSpec((1, 1, past, width), cache), pl.BlockSpec((ts, width), tok), pl.BlockSpec((ts, width), tok),
                  pl.BlockSpec((1, ts, H_B), lambda b: (b, 0, 0)),
                  pl.BlockSpec((1, 1, H_B, past // tk, tk), lambda b: (layer, b, 0, 0, 0)),
                  pl.BlockSpec((1, H_B, ts), lambda b: (b, 0, 0))],
        out_specs=pl.BlockSpec((ts, width), tok),
        out_shape=jax.ShapeDtypeStruct((n_seq * ts, width), F32),
        scratch_shapes=[pltpu.VMEM((rows, 1), F32), pltpu.VMEM((rows, 1), F32), pltpu.VMEM((rows, width), F32)],
        compiler_params=_cparams(("arbitrary",)), name="fox_decode",
    )(q, cache_k, cache_v, k_new, v_new, fq_col, f_cache, f_new.reshape(n_seq, H_B, ts))


def _mla_decode_kernel(q_ref, cc_ref, cr_ref, cn_ref, kn_ref, o_ref, m_ref, l_ref, acc_ref, *, tk, past, ts):
    q = q_ref[0]
    q_lat = q[:, :KV_LORA]
    q_rope = q[:, KV_LORA:KV_LORA + ROPE_C]
    m_ref[...] = jnp.full(m_ref.shape, NEG, F32)
    l_ref[...] = jnp.zeros(l_ref.shape, F32)
    acc_ref[...] = jnp.zeros(acc_ref.shape, F32)

    def cached(jb, carry):
        k0 = pl.multiple_of(jb * tk, tk)
        ck = cc_ref[0, 0, pl.ds(k0, tk), :]
        s = _hdot(q_lat, ck, _NT) + _hdot(q_rope, cr_ref[0, 0, pl.ds(k0, tk), :], _NT)
        _online_softmax_step(s, ck, m_ref, l_ref, acc_ref)
        return carry

    lax.fori_loop(0, past // tk, cached, 0)
    cn = cn_ref[...]
    s = _hdot(q_lat, cn, _NT) + _hdot(q_rope, kn_ref[:, KV_LORA:KV_LORA + ROPE_C], _NT)
    q_chunk = (past + lax.broadcasted_iota(jnp.int32, s.shape, 0) % ts) // CHUNK
    k_chunk = (past + lax.broadcasted_iota(jnp.int32, s.shape, 1)) // CHUNK
    _online_softmax_step(jnp.where(k_chunk <= q_chunk, s, NEG), cn, m_ref, l_ref, acc_ref)
    o_ref[0] = acc_ref[...] / l_ref[...]


def _mla_decode(q, cache_ckv, cache_krope, layer, ckv_new, kc_new, n_seq, ts, tk):
    past = cache_ckv.shape[2]
    rows = H_C * ts

    def tok(b):
        return (b, 0)

    def cache(b):
        return (layer, b, 0, 0)

    return pl.pallas_call(
        functools.partial(_mla_decode_kernel, tk=tk, past=past, ts=ts), grid=(n_seq,),
        in_specs=[pl.BlockSpec((1, rows, 2 * LANE), lambda b: (b, 0, 0)),
                  pl.BlockSpec((1, 1, past, KV_LORA), cache), pl.BlockSpec((1, 1, past, ROPE_C), cache),
                  pl.BlockSpec((ts, KV_LORA), tok), pl.BlockSpec((ts, 2 * LANE), tok)],
        out_specs=pl.BlockSpec((1, rows, KV_LORA), lambda b: (b, 0, 0)),
        out_shape=jax.ShapeDtypeStruct((n_seq, rows, KV_LORA), F32),
        scratch_shapes=[pltpu.VMEM((rows, 1), F32), pltpu.VMEM((rows, 1), F32), pltpu.VMEM((rows, KV_LORA), F32)],
        compiler_params=_cparams(("arbitrary",)), name="mla_decode",
    )(q, cache_ckv, cache_krope, ckv_new, kc_new)


def _mixout_kernel(*refs, precise, route):
    x_ref, gate_ref, ya_ref, of_ref, ol_ref, wuv_ref, wo_ref, sh_ref, sc_ref, g_ref = refs[:10]
    rest = list(refs[10:])
    wr_ref = rest.pop(0) if route else None
    rest.pop(0)
    o_ref, h_ref = rest[:2]
    ids_ref, gates_ref = rest[2:] if route else (None, None)
    mm = functools.partial(_mm, precise=precise)
    acc = mm(ya_ref[...], wo_ref[:V_A, :])
    if len(of_ref.shape) == 2:
        acc = acc + mm(of_ref[...], wo_ref[V_A:V_A + QKV_B, :])
    else:
        for hh in range(H_B):
            r0 = V_A + hh * D_B
            acc = acc + mm(of_ref[0, hh], wo_ref[r0:r0 + D_B, :])
    yc = mm(ol_ref[0, 0], wuv_ref[0])
    for hh in range(1, H_C):
        yc = yc + mm(ol_ref[0, hh], wuv_ref[hh])
    acc = acc + mm(yc, wo_ref[V_A + QKV_B:, :])
    x_new = x_ref[...] + gate_ref[0] * acc
    o_ref[...] = x_new
    _ffn_input(x_new, sh_ref, sc_ref, g_ref, wr_ref, h_ref, ids_ref, gates_ref)


def _mixout(x2, gate, ya, ofox, olat, wuv, wo, shift, scale, g_ffn, router_pad, hbuf, row0, n_seq, t_len, tm, precise):
    nblk = t_len // tm
    n_tok = n_seq * t_len
    mod_rows = gate.shape[1]
    route = router_pad is not None
    off = row0 // tm

    def row(i):
        return (i, 0)

    def seq(i):
        return (i // nblk, 0, 0)

    def const2(i):
        return (0, 0)

    mod_spec = pl.BlockSpec((1, mod_rows, D_MODEL), seq)
    in_specs = [pl.BlockSpec((tm, D_MODEL), row), mod_spec, pl.BlockSpec((tm, V_A), row),
                pl.BlockSpec((tm, QKV_B), row) if ofox.ndim == 2 else
                pl.BlockSpec((1, H_B, tm, D_B), lambda i: (i // nblk, 0, i % nblk, 0)),
                pl.BlockSpec((1, H_C, tm, KV_LORA), lambda i: (i, 0, 0, 0)),
                pl.BlockSpec((H_C, KV_LORA, V_A), lambda i: (0, 0, 0)),
                pl.BlockSpec((D_MODEL, D_MODEL), const2),
                mod_spec, mod_spec, pl.BlockSpec((1, D_MODEL), const2)]
    args = [x2, gate, ya, ofox, olat, wuv, wo, shift, scale, g_ffn]
    out_shape = [jax.ShapeDtypeStruct((n_tok, D_MODEL), F32), jax.ShapeDtypeStruct(hbuf.shape, F32)]
    out_specs = [pl.BlockSpec((tm, D_MODEL), row), _block_rows(tm, hbuf, lambda i: (off + i, 0))]
    if route:
        in_specs.append(pl.BlockSpec((D_MODEL, LANE), const2))
        args.append(router_pad)
        out_shape += [jax.ShapeDtypeStruct((n_tok, LANE), jnp.int32), jax.ShapeDtypeStruct((n_tok, LANE), F32)]
        out_specs += [pl.BlockSpec((tm, LANE), row), pl.BlockSpec((tm, LANE), row)]
    in_specs.append(pl.BlockSpec(memory_space=pl.ANY))
    args.append(hbuf)
    return pl.pallas_call(
        functools.partial(_mixout_kernel, precise=precise, route=route), grid=(n_tok // tm,),
        in_specs=in_specs, out_specs=tuple(out_specs), out_shape=tuple(out_shape),
        input_output_aliases={len(args) - 1: 1},
        compiler_params=_cparams(("arbitrary",)), name="mixout",
    )(*args)


def _ffn_input(x, sh_ref, sc_ref, g_ref, wr_ref, h_ref, ids_ref, gates_ref):
    h = (_rms(x) * g_ref[...]) * (1.0 + sc_ref[0]) + sh_ref[0]
    _store_rows(h_ref, h)
    if wr_ref is not None:
        logits = _dot_f32(h, wr_ref[...])
        lane = lax.broadcasted_iota(jnp.int32, logits.shape, 1)
        logits = jnp.where(lane < N_EXPERTS, logits, NEG)
        m1 = jnp.max(logits, axis=-1, keepdims=True)
        i1 = jnp.min(jnp.where(logits == m1, lane, LANE), axis=-1, keepdims=True)
        rest = jnp.where(lane == i1, NEG, logits)
        m2 = jnp.max(rest, axis=-1, keepdims=True)
        i2 = jnp.min(jnp.where(rest == m2, lane, LANE), axis=-1, keepdims=True)
        e2 = jnp.exp(m2 - m1)
        g1 = 1.0 / (1.0 + e2)
        g2 = e2 / (1.0 + e2)
        ids_ref[...] = jnp.where(lane == 0, i1, i2)
        gates_ref[...] = jnp.where(lane == 0, g1, g2)


GATHER_ROWS = 512
GATHER_UNROLL = 32


def _gather_kernel(idx_ref, src_ref, out_ref, sem):
    def issue(c, carry):
        base = pl.multiple_of(c * GATHER_UNROLL, GATHER_UNROLL)
        for r in range(GATHER_UNROLL):
            pltpu.make_async_copy(src_ref.at[idx_ref[0, 0, base + r]], out_ref.at[base + r], sem).start(priority=r % 2)
        return carry

    lax.fori_loop(0, GATHER_ROWS // GATHER_UNROLL, issue, 0)
    pltpu.make_async_copy(src_ref.at[pl.ds(0, GATHER_ROWS)], out_ref, sem).wait()


def _gather_rows(src, idx):
    m = idx.shape[0]
    return pl.pallas_call(
        _gather_kernel, grid=(m // GATHER_ROWS,),
        in_specs=[pl.BlockSpec((1, 1, GATHER_ROWS), lambda i: (i, 0, 0), memory_space=pltpu.SMEM),
                  pl.BlockSpec(memory_space=pl.ANY)],
        out_specs=pl.BlockSpec((GATHER_ROWS,) + src.shape[1:], lambda i: (i, 0, 0)),
        out_shape=jax.ShapeDtypeStruct((m,) + src.shape[1:], src.dtype),
        scratch_shapes=[pltpu.SemaphoreType.DMA(())],
        compiler_params=pltpu.CompilerParams(dimension_semantics=("arbitrary",)), name="gather_rows",
    )(idx.reshape(m // GATHER_ROWS, 1, GATHER_ROWS), src)


def _swiglu_hidden_block(xs_ref, wg_ref, wu_ref, wd_ref, acc_ref, mm):
    x = xs_ref[...]
    tf = wg_ref.shape[2]
    cw = SWIGLU_COLS if tf % SWIGLU_COLS == 0 else tf
    part = None
    for c0 in range(0, tf, cw):
        a = mm(x, wg_ref[0, :, c0:c0 + cw])
        u = mm(x, wu_ref[0, :, c0:c0 + cw])
        d = mm(_silu(a) * u, wd_ref[0, c0:c0 + cw, :])
        part = d if part is None else part + d
    acc_ref[...] += part


def _swiglu_kernel(te_ref, nt_ref, x_ref, wg_ref, wu_ref, wd_ref, o_ref, acc_ref, xs_ref, *, precise):
    mm = functools.partial(_mm, precise=precise)
    i = pl.program_id(0)
    j = pl.program_id(1)

    @pl.when(i < nt_ref[0])
    def _():
        @pl.when(j == 0)
        def _():
            acc_ref[...] = jnp.zeros(acc_ref.shape, F32)
            xs_ref[...] = _load_rows(x_ref).astype(xs_ref.dtype)

        _swiglu_hidden_block(xs_ref, wg_ref, wu_ref, wd_ref, acc_ref, mm)

        @pl.when(j == pl.num_programs(1) - 1)
        def _():
            _store_rows(o_ref, acc_ref[...])

    @pl.when((i >= nt_ref[0]) & (j == pl.num_programs(1) - 1))
    def _():
        o_ref[...] = jnp.zeros(o_ref.shape, F32)


def _swiglu_gather_kernel(te_ref, nt_ref, idx_ref, idx_next_ref, h_ref, wg_ref, wu_ref, wd_ref, o_ref,
                          acc_ref, xs_ref, xbuf_ref, sem_ref):
    mm = functools.partial(_mm, precise=False)
    i = pl.program_id(0)
    j = pl.program_id(1)
    nt = nt_ref[0]
    tm = acc_ref.shape[0]
    slot = i % 2

    def start_rows(ids_ref, dst_slot):
        def issue(c, carry):
            base = pl.multiple_of(c * GATHER_UNROLL, GATHER_UNROLL)
            for r in range(GATHER_UNROLL):
                src0 = pl.multiple_of(ids_ref[0, 0, base + r], SUBLANE)
                dst0 = pl.multiple_of((base + r) * SUBLANE, SUBLANE)
                pltpu.make_async_copy(h_ref.at[pl.ds(src0, SUBLANE), :], xbuf_ref.at[dst_slot, pl.ds(dst0, SUBLANE), :],
                                      sem_ref.at[dst_slot]).start(priority=r % 2)
            return carry

        lax.fori_loop(0, tm // GATHER_UNROLL, issue, 0)

    @pl.when((i == 0) & (j == 0) & (nt > 0))
    def _():
        start_rows(idx_ref, 0)

    @pl.when(i < nt)
    def _():
        @pl.when(j == 0)
        def _():
            pltpu.make_async_copy(h_ref.at[pl.ds(0, tm * SUBLANE), :], xbuf_ref.at[slot], sem_ref.at[slot]).wait()
            acc_ref[...] = jnp.zeros(acc_ref.shape, F32)
            xs_ref[...] = jnp.concatenate([xbuf_ref[slot, pl.ds(s, tm, stride=SUBLANE), :] for s in range(SUBLANE)],
                                          axis=-1).astype(xs_ref.dtype)

            @pl.when(i + 1 < nt)
            def _():
                start_rows(idx_next_ref, 1 - slot)

        _swiglu_hidden_block(xs_ref, wg_ref, wu_ref, wd_ref, acc_ref, mm)

        @pl.when(j == pl.num_programs(1) - 1)
        def _():
            _store_rows(o_ref, acc_ref[...])

    @pl.when((i >= nt) & (j == pl.num_programs(1) - 1))
    def _():
        o_ref[...] = jnp.zeros(o_ref.shape, F32)


def _swiglu_gathered(h_tiled, src_rows8, tile_expert, n_tiles_used, wg, wu, wd, tm, tf):
    m = src_rows8.shape[0]
    n_tiles = m // tm
    f = wg.shape[2]
    idx = src_rows8.reshape(n_tiles, 1, tm)
    grid_spec = pltpu.PrefetchScalarGridSpec(
        num_scalar_prefetch=2, grid=(n_tiles, f // tf),
        in_specs=[pl.BlockSpec((1, 1, tm), lambda i, j, te, nt: (i, 0, 0), memory_space=pltpu.SMEM),
                  pl.BlockSpec((1, 1, tm), lambda i, j, te, nt: (jnp.minimum(i + 1, n_tiles - 1), 0, 0),
                               memory_space=pltpu.SMEM),
                  pl.BlockSpec(memory_space=pl.ANY),
                  pl.BlockSpec((1, D_MODEL, tf), lambda i, j, te, nt: (te[i], 0, j)),
                  pl.BlockSpec((1, D_MODEL, tf), lambda i, j, te, nt: (te[i], 0, j)),
                  pl.BlockSpec((1, tf, D_MODEL), lambda i, j, te, nt: (te[i], j, 0))],
        out_specs=pl.BlockSpec((tm * SUBLANE, LANE), lambda i, j, te, nt: (i, 0)),
        scratch_shapes=[pltpu.VMEM((tm, D_MODEL), F32), pltpu.VMEM((tm, D_MODEL), BF16),
                        pltpu.VMEM((2, tm * SUBLANE, LANE), F32), pltpu.SemaphoreType.DMA((2,))])
    return pl.pallas_call(
        _swiglu_gather_kernel, grid_spec=grid_spec,
        out_shape=jax.ShapeDtypeStruct((m * SUBLANE, LANE), F32),
        compiler_params=_cparams(("arbitrary", "arbitrary")), name="swiglu_gather",
    )(tile_expert, n_tiles_used, idx, idx, h_tiled, wg, wu, wd)


def _swiglu_grouped(x, tile_expert, n_tiles_used, wg, wu, wd, tm, tf, precise=False):
    m = x.shape[0] // SUBLANE if _is_tiled(x.shape) else x.shape[0]
    f = wg.shape[2]
    grid_spec = pltpu.PrefetchScalarGridSpec(
        num_scalar_prefetch=2, grid=(m // tm, f // tf),
        in_specs=[_block_rows(tm, x, lambda i, j, te, nt: (i, 0)),
                  pl.BlockSpec((1, D_MODEL, tf), lambda i, j, te, nt: (te[i], 0, j)),
                  pl.BlockSpec((1, D_MODEL, tf), lambda i, j, te, nt: (te[i], 0, j)),
                  pl.BlockSpec((1, tf, D_MODEL), lambda i, j, te, nt: (te[i], j, 0))],
        out_specs=_block_rows(tm, x, lambda i, j, te, nt: (i, 0)),
        scratch_shapes=[pltpu.VMEM((tm, D_MODEL), F32), pltpu.VMEM((tm, D_MODEL), F32 if precise else BF16)])
    return pl.pallas_call(
        functools.partial(_swiglu_kernel, precise=precise), grid_spec=grid_spec,
        out_shape=jax.ShapeDtypeStruct(x.shape, F32),
        compiler_params=_cparams(("arbitrary", "arbitrary")), name="swiglu",
    )(tile_expert, n_tiles_used, x, wg, wu, wd)


def _combine_kernel(*refs, moe, final):
    refs = list(refs)
    x_ref = refs.pop(0)
    gate_ref = refs.pop(0)
    y1_ref = refs.pop(0)
    if moe:
        y2_ref = refs.pop(0)
        gates_ref = refs.pop(0)
    fg_ref = refs.pop(0) if final else None
    o_ref = refs.pop(0)
    y = _load_rows(y1_ref)
    if moe:
        gts = gates_ref[...]
        y = gts[:, 0:1] * y + gts[:, 1:2] * _load_rows(y2_ref)
    out = x_ref[...] + gate_ref[0] * y
    if final:
        out = _rms(out) * fg_ref[...]
    o_ref[...] = out


def _combine(x2, gate, y, y_row0, y2_row0, gates, final_g, n_seq, t_len, tm):
    nblk = t_len // tm
    n_tok = n_seq * t_len
    moe = gates is not None
    final = final_g is not None

    def row(i):
        return (i, 0)

    in_specs = [pl.BlockSpec((tm, D_MODEL), row), pl.BlockSpec((1, gate.shape[1], D_MODEL), lambda i: (i // nblk, 0, 0)),
                _block_rows(tm, y, lambda i: (y_row0 // tm + i, 0))]
    args = [x2, gate, y]
    if moe:
        in_specs += [_block_rows(tm, y, lambda i: (y2_row0 // tm + i, 0)), pl.BlockSpec((tm, LANE), row)]
        args += [y, gates]
    if final:
        in_specs.append(pl.BlockSpec((1, D_MODEL), lambda i: (0, 0)))
        args.append(final_g)
    return pl.pallas_call(
        functools.partial(_combine_kernel, moe=moe, final=final), grid=(n_tok // tm,),
        in_specs=in_specs, out_specs=pl.BlockSpec((tm, D_MODEL), row),
        out_shape=jax.ShapeDtypeStruct((n_tok, D_MODEL), F32),
        compiler_params=_cparams(("arbitrary",)), name="combine",
    )(*args)


def _pack_mixer_weights(w_in, w_gate2, b_gate, fox_b_f, qng, kvng, w_uq, w_uk, w_uv, gla_norm_g):
    offs = np.concatenate([[0], np.cumsum(IN_SPLITS)])
    cols = {n: (int(offs[i]), int(offs[i + 1])) for i, n in enumerate(
        ("gq", "gk", "gv", "ag", "ar", "fq", "fk", "fv", "bf", "cq", "ckv", "kr"))}

    def seg(n):
        return w_in[:, cols[n][0]:cols[n][1]]

    half = ROPE_C // 2
    kr = seg("kr")

    def zcols(n):
        return jnp.zeros((D_MODEL, n), F32)

    small = jnp.concatenate([kr, seg("ar"), seg("bf"), zcols(SM_KRS - SM_BF - H_B), -kr[:, half:], kr[:, :half],
                             zcols(LANE - SM_KRS - ROPE_C)], axis=1)
    w = jnp.concatenate([seg("gq"), zcols(C_GK - QK_A), seg("gk"), zcols(C_GV - C_GK - QK_A), seg("gv"), seg("ag"),
                         seg("fq"), seg("fk"), seg("fv"), seg("cq"), seg("ckv"), small], axis=1)
    assert w.shape == (D_MODEL, N_PACK) and small.shape[1] == LANE
    wg2 = jnp.pad(w_gate2, ((SM_AR, LANE - SM_AR - GATE_RANK), (0, 2 * LANE - QK_A)))
    bg = jnp.pad(b_gate, (0, 2 * LANE - QK_A)).reshape(1, 2 * LANE)
    bf = jnp.pad(fox_b_f, (SM_BF, LANE - SM_BF - H_B)).reshape(1, LANE)
    uq = w_uq.reshape(Q_LORA, H_C, NOPE_C + ROPE_C)
    wuqn = uq[:, :, :NOPE_C].reshape(Q_LORA, H_C * NOPE_C)
    x1 = uq[:, :, NOPE_C:NOPE_C + half]
    x2 = uq[:, :, NOPE_C + half:]
    pad = jnp.zeros((Q_LORA, H_C, LANE - ROPE_C), F32)
    wr = jnp.concatenate([x1, x2, pad], axis=2).reshape(Q_LORA, H_C * LANE)
    wrs = jnp.concatenate([-x2, x1, pad], axis=2).reshape(Q_LORA, H_C * LANE)
    eye = jnp.eye(H_C, dtype=F32)
    wuk = (jnp.transpose(w_uk, (1, 2, 0))[:, :, None, :] * eye[:, None, :, None]).reshape(H_C * NOPE_C, H_C * KV_LORA)
    wuv = (jnp.transpose(w_uv, (1, 0, 2))[:, :, None, :] * eye[:, None, :, None]).reshape(H_C, KV_LORA, V_A)
    full = dict(w_in=w, wg2=wg2, bg=bg, bf=bf, qng=qng.reshape(1, Q_LORA), kvng=kvng.reshape(1, KV_LORA), wuqn=wuqn,
                wuk=wuk, wr=wr, wrs=wrs, wuv=wuv, gnorm=jnp.tile(gla_norm_g, H_A).reshape(1, V_A))
    half_prec = dict(full)
    for n in ("w_in", "wg2", "wuqn", "wuk", "wr", "wrs", "wuv"):
        half_prec[n] = full[n].astype(BF16)
    return half_prec, full


def _rope_tables(pos):
    half = ROPE_C // 2
    inv_freq = ROPE_BASE ** (-jnp.arange(half, dtype=F32) / half)
    ang = pos.astype(F32)[:, None] * inv_freq[None, :]
    n = pos.shape[0]
    cos = jnp.concatenate([jnp.cos(ang), jnp.cos(ang), jnp.ones((n, LANE - ROPE_C), F32)], axis=1)
    sin = jnp.concatenate([jnp.sin(ang), jnp.sin(ang), jnp.zeros((n, LANE - ROPE_C), F32)], axis=1)
    return cos, sin


def _state_to_t(s):
    b = s.shape[0]
    eye = jnp.eye(H_A, dtype=F32)
    s_vk = jnp.swapaxes(s, 2, 3)
    return (s_vk[:, :, :, None, :] * eye[None, :, None, :, None]).reshape(b, V_A, QK_A)


def _state_from_t(s_t):
    b = s_t.shape[0]
    blocks = s_t.reshape(b, H_A, DV_A, H_A, DK_A)
    diag = jnp.stack([blocks[:, hh, :, hh, :] for hh in range(H_A)], axis=1)
    return jnp.swapaxes(diag, 2, 3)


def _round_up(a, b):
    return (a + b - 1) // b * b


def kernel(x_prompt, x_sample, c_prompt, c_sample, cache_fox_k, cache_fox_v, cache_fox_logf, cache_mla_ckv, cache_mla_krope, state_gla, ada_w, ada_b, norm_mix_g, norm_ffn_g, w_in, gla_w_gate2, gla_b_gate, gla_norm_g, fox_b_f, mla_q_norm_g, mla_kv_norm_g, mla_w_uq, mla_w_uk, mla_w_uv, w_out, ffn_w_gate, ffn_w_up, ffn_w_down, moe_router, moe_w_gate, moe_w_up, moe_w_down, final_norm_g):
    bp, tp, _ = x_prompt.shape
    bs, ts, _ = x_sample.shape
    past = cache_fox_k.shape[2]
    np_tok, ns_tok = bp * tp, bs * ts
    n_all = np_tok + ns_tok

    tm_p, tm_s = 512, ns_tok
    tq_fox, tk = 512, 512

    nc = _round_up(bp + bs, 8)
    c_all = jnp.zeros((nc, D_MODEL), F32).at[:bp].set(c_prompt).at[bp:bp + bs].set(c_sample)
    mod = _ada(c_all, ada_w, ada_b)

    cos_p, sin_p = _rope_tables(jnp.arange(tp))
    cos_s, sin_s = _rope_tables(past + jnp.tile(jnp.arange(ts), bs))

    def seq_major(a):
        hh, d = a.shape[1], a.shape[3]
        return jnp.transpose(a.reshape(hh, bs, ts, d), (1, 0, 2, 3))

    def tok_major(a):
        hh, d = a.shape[1], a.shape[3]
        return jnp.transpose(a, (1, 0, 2, 3)).reshape(1, hh, bs * ts, d)

    cl = jnp.transpose(cache_fox_logf.astype(F32), (0, 1, 3, 2)).reshape(DEPTH * bs * H_B, past)
    f_cache, f_cache2 = _cumsum(cl, jnp.zeros((cl.shape[0], 1), F32), 512)
    f_cache = f_cache.reshape(DEPTH, bs * H_B, past)
    f_cache2 = f_cache2.reshape(DEPTH, bs, H_B, past // tk, tk)
    cache_k2d = cache_fox_k.astype(F32).reshape(DEPTH, bs, past, H_B * D_B)
    cache_v2d = cache_fox_v.astype(F32).reshape(DEPTH, bs, past, H_B * D_B)

    xp = x_prompt.reshape(np_tok, D_MODEL)
    xs = x_sample.reshape(ns_tok, D_MODEL)
    p_states = [[] for _ in range(6)]
    s_states = [[] for _ in range(6)]

    for l in range(DEPTH):
        mods = [mod[l, :, i * D_MODEL:(i + 1) * D_MODEL] for i in range(6)]
        mp = [m[:bp].reshape(bp, 1, D_MODEL) for m in mods]
        ms = [jnp.repeat(m[bp:bp + bs], ts, axis=0).reshape(1, ns_tok, D_MODEL) for m in mods]
        pw, pw32 = _pack_mixer_weights(w_in[l], gla_w_gate2[l], gla_b_gate[l], fox_b_f[l], mla_q_norm_g[l],
                                       mla_kv_norm_g[l], mla_w_uq[l], mla_w_uk[l], mla_w_uv[l], gla_norm_g[l])
        wo = w_out[l].astype(BF16)
        g_mix = norm_mix_g[l].reshape(1, D_MODEL)
        g_ffn = norm_ffn_g[l].reshape(1, D_MODEL)
        moe = l % 2 == 1
        if moe:
            router_pad = jnp.pad(moe_router[l // 2], ((0, 0), (0, LANE - N_EXPERTS)))
            hbuf_p, hbuf_s = jnp.zeros((n_all * SUBLANE, LANE), F32), None
        else:
            router_pad = None
            hbuf_p, hbuf_s = jnp.zeros((np_tok, D_MODEL), F32), jnp.zeros((ns_tok, D_MODEL), F32)

        (gq, gk, gv, ag, gla, fk, fv, ckv, kc, small, qs, fqh, fkh, fvh) = _inproj(
            xp, mp[0], mp[1], g_mix, pw, cos_p, sin_p, bp, tp, tm_p, False, True)
        logf = small[:, SM_BF:SM_BF + H_B]
        krope = small[:, SM_KR:SM_KR + ROPE_C]
        ya, s_t = _gla(gq, gk, gv, gla, ag, pw["gnorm"], jnp.zeros((bp, V_A, QK_A), F32), bp, tp, 512, CHUNK, bp, False)
        g_fox = bp * H_B
        fqa, fka = _fox_prep(small, fqh, fkh, bp, tp, 512)
        o_fox = _flash(fqa.reshape(g_fox, tp // tq_fox, tq_fox, 2 * D_B), fka.reshape(g_fox, tp, 2 * D_B),
                       fvh.reshape(g_fox, tp, 2 * D_B), None, None,
                       tq=tq_fox, rep=1, tk=tk, q0=0, mode="causal", kv_len=tp, dv=D_B, hp=H_B)
        o_mla = _flash(qs.reshape(bp, tp // tm_p, H_C * tm_p, 2 * LANE), kc.reshape(bp, tp, 2 * LANE), None, None, None,
                       tq=tm_p, rep=H_C, tk=tk, q0=0, mode="chunk", kv_len=tp, dv=KV_LORA)
        res = _mixout(xp, mp[2], ya, o_fox.reshape(bp, H_B, tp, D_B), o_mla.reshape(np_tok // tm_p, H_C, tm_p, KV_LORA),
                      pw["wuv"], wo, mp[3], mp[4], g_ffn, router_pad, hbuf_p, 0, bp, tp, tm_p, False)
        xp, h_p = res[0], res[1]
        for i, st in enumerate((_state_from_t(s_t), fk.reshape(bp, tp, H_B, D_B), fv.reshape(bp, tp, H_B, D_B),
                                logf.reshape(bp, tp, H_B), ckv.reshape(bp, tp, KV_LORA), krope.reshape(bp, tp, ROPE_C))):
            p_states[i].append(st)

        (gq, gk, gv, ag, gla, fk, fv, ckv, kc, small, qs, fq) = _inproj(
            xs, ms[0], ms[1], g_mix, pw32, cos_s, sin_s, 1, ns_tok, tm_s, True, False)
        logf = small[:, SM_BF:SM_BF + H_B]
        krope = small[:, SM_KR:SM_KR + ROPE_C]
        g_fox = bs * H_B
        f_rows = jnp.transpose(logf.reshape(bs, ts, H_B), (0, 2, 1)).reshape(g_fox, ts)
        f_new = _cumsum(f_rows, f_cache[l][:, past - 1:past], ts)[1]
        fq_col = jnp.transpose(f_new.reshape(bs, H_B, ts), (0, 2, 1))
        ya, s_t = _gla(gq, gk, gv, gla, ag, pw["gnorm"], _state_to_t(state_gla[l].astype(F32)), bs, ts, ts, ts, 4, True)
        o_fox = _fox_decode(fq, cache_k2d, cache_v2d, l, fk, fv, fq_col, f_cache2, f_new, bs, ts, tk)
        o_mla = _mla_decode(seq_major(qs).reshape(bs, H_C * ts, 2 * LANE), cache_mla_ckv.astype(F32),
                            cache_mla_krope.astype(F32), l, ckv, kc, bs, ts, tk)
        res_s = _mixout(xs, ms[2], ya, o_fox, tok_major(o_mla.reshape(bs, H_C, ts, KV_LORA)), pw32["wuv"], w_out[l],
                        ms[3], ms[4], g_ffn, router_pad, h_p if moe else hbuf_s, np_tok if moe else 0,
                        1, ns_tok, tm_s, True)
        xs, h_s = res_s[0], res_s[1]
        for i, st in enumerate((_state_from_t(s_t), fk.reshape(bs, ts, H_B, D_B), fv.reshape(bs, ts, H_B, D_B),
                                logf.reshape(bs, ts, H_B), ckv.reshape(bs, ts, KV_LORA), krope.reshape(bs, ts, ROPE_C))):
            s_states[i].append(st)

        last = l == DEPTH - 1
        fg = final_norm_g.reshape(1, D_MODEL) if last else None
        tm_f = 512
        if l % 2 == 0:
            j = l // 2
            n_tiles = np_tok // tm_f
            y_p = _swiglu_grouped(h_p, jnp.zeros((n_tiles,), jnp.int32), jnp.full((1,), n_tiles, jnp.int32),
                                  ffn_w_gate[j:j + 1].astype(BF16), ffn_w_up[j:j + 1].astype(BF16),
                                  ffn_w_down[j:j + 1].astype(BF16), tm_f, ffn_w_gate.shape[2])
            y_s = _swiglu_grouped(h_s, jnp.zeros((1,), jnp.int32), jnp.ones((1,), jnp.int32),
                                  ffn_w_gate[j:j + 1], ffn_w_up[j:j + 1], ffn_w_down[j:j + 1], tm_s, 1408, precise=True)
            xp = _combine(xp, mp[5], y_p, 0, 0, None, fg, bp, tp, tm_p)
            xs = _combine(xs, ms[5], y_s, 0, 0, None, fg, 1, ns_tok, tm_s)
        else:
            j = l // 2
            h_all = h_s
            ids_p, gates_p = res[2:]
            ids_s, gates_s = res_s[2:]
            ids = jnp.concatenate([ids_p[:, :2], ids_s[:, :2]], axis=0)
            e = jnp.transpose(ids).reshape(-1)
            onehot = (e[:, None] == jnp.arange(N_EXPERTS)[None, :]).astype(jnp.int32)
            rank = jnp.sum((jnp.cumsum(onehot, axis=0) - onehot) * onehot, axis=1)
            counts = jnp.sum(onehot, axis=0)
            padded = (counts + tm_f - 1) // tm_f * tm_f
            ends = jnp.cumsum(padded)
            starts = ends - padded
            pos = starts[e] + rank
            m_pad = _round_up(2 * n_all + N_EXPERTS * (tm_f - 1), tm_f)
            token = jnp.tile(jnp.arange(n_all, dtype=jnp.int32), 2)
            src = jnp.zeros((m_pad,), jnp.int32).at[pos].set(token)
            n_tiles = m_pad // tm_f
            tile_row0 = jnp.arange(n_tiles, dtype=jnp.int32) * tm_f
            tile_expert = jnp.minimum(jnp.sum((ends[None, :] <= tile_row0[:, None]).astype(jnp.int32), axis=1),
                                      N_EXPERTS - 1)
            n_used = (ends[-1] // tm_f).astype(jnp.int32).reshape(1)
            y = _swiglu_gathered(h_all, src * SUBLANE, tile_expert, n_used, moe_w_gate[j].astype(BF16),
                                 moe_w_up[j].astype(BF16), moe_w_down[j].astype(BF16), tm_f, 1792)
            n_back = _round_up(n_all, GATHER_ROWS)
            back = jnp.zeros((2 * n_back,), jnp.int32).at[:n_all].set(pos[:n_all]).at[n_back:n_back + n_all].set(pos[n_all:])
            yg = _tiled(_gather_rows(_untiled(y), back))
            xp = _combine(xp, mp[5], yg, 0, n_back, gates_p, fg, bp, tp, tm_p)
            xs = _combine(xs, ms[5], yg, np_tok, n_back + np_tok, gates_s, fg, 1, ns_tok, tm_s)

    outs_p = [jnp.stack(s, axis=0) for s in p_states]
    outs_s = [jnp.stack(s, axis=0) for s in s_states]
    return (xp.reshape(bp, tp, D_MODEL), xs.reshape(bs, ts, D_MODEL), *outs_p, *outs_s)
```
